```python
import math
import jax, jax.numpy as jnp
from jax import lax
import numpy as np

D_MODEL = 1024
BATCH = 2
SEQ = 16384
DEPTH = 4

CHUNK = 64
N_META = 16
N_MIXERS = 3
N_POOL_LAYERS = (DEPTH + 2) // 3
N_DIFF_LAYERS = (DEPTH + 1) // 3
N_CONV_LAYERS = DEPTH // 3
POOL_WINDOWS = (2, 4, 8, 16)
POOL_GROUP = D_MODEL // len(POOL_WINDOWS)
DA_HEADS = 8
DA_HEAD_DIM = D_MODEL // (2 * DA_HEADS)
Q_BLOCK = 128
REL_BUCKETS = 32
REL_MAX_DIST = 128
CONV_WIDTH = 31
FFN_HIDDEN = ((8 * D_MODEL + 3 * 256 - 1) // (3 * 256)) * 256
RMS_EPS = 1e-6
LN_EPS = 1e-5
SUBLN_EPS = 1e-5

kernel_name = 'hybrid_pool_diffattn_conformer_trunk'


def rmsnorm(x, g, eps=RMS_EPS):
    xf = x.astype(jnp.float32)
    y = xf * lax.rsqrt(jnp.mean(xf * xf, axis=-1, keepdims=True) + eps)
    return (y * g.astype(jnp.float32)).astype(x.dtype)


def layernorm(x, g, b, eps=LN_EPS):
    xf = x.astype(jnp.float32)
    mu = jnp.mean(xf, axis=-1, keepdims=True)
    var = jnp.mean(jnp.square(xf - mu), axis=-1, keepdims=True)
    y = (xf - mu) * lax.rsqrt(var + eps)
    return (y * g.astype(jnp.float32) + b.astype(jnp.float32)).astype(x.dtype)


def chunk_id(pos):
    return jnp.where(pos < N_META, 0, 1 + (pos - N_META) // CHUNK)


def t5_bucket(rel):
    half = REL_BUCKETS // 2
    max_exact = half // 2
    offset = jnp.where(rel > 0, half, 0)
    n = jnp.abs(rel)
    log_ratio = jnp.log(jnp.maximum(n, 1).astype(jnp.float32) / max_exact) / math.log(REL_MAX_DIST / max_exact)
    large = jnp.minimum(max_exact + (log_ratio * (half - max_exact)).astype(jnp.int32), half - 1)
    return offset + jnp.where(n < max_exact, n, large)


def trailing_mean(xg, win):
    L = xg.shape[1]
    cs0 = jnp.pad(jnp.cumsum(xg.astype(jnp.float32), axis=1), ((0, 0), (1, 0), (0, 0)))
    lagged = jnp.pad(cs0, ((0, 0), (win, 0), (0, 0)))[:, 1:L + 1]
    count = jnp.minimum(jnp.arange(1, L + 1), win).astype(jnp.float32)[None, :, None]
    return (cs0[:, 1:] - lagged) / count


def pool_mixer(h, w, b, scale):
    B_, L, _ = h.shape
    groups = []
    for g, win in enumerate(POOL_WINDOWS):
        hg = h[..., g * POOL_GROUP:(g + 1) * POOL_GROUP]
        groups.append(trailing_mean(hg, win).astype(h.dtype) - hg)
    pooled = jnp.stack(groups, axis=2)
    mixed = jnp.einsum('blgc,gcd->blgd', pooled, w).reshape(B_, L, D_MODEL)
    return (mixed + b) * scale


def diff_attention(h, w_qkv, w_o, lq1, lk1, lq2, lk2, subln_g, rel_bias_table, lam_init):
    B_, L, _ = h.shape
    q, k, v = jnp.split(h @ w_qkv, 3, axis=-1)
    Lp = -(-L // Q_BLOCK) * Q_BLOCK
    pad = ((0, 0), (0, Lp - L), (0, 0))
    q = jnp.pad(q, pad).reshape(B_, Lp, DA_HEADS, 2, DA_HEAD_DIM)
    k = jnp.pad(k, pad).reshape(B_, Lp, DA_HEADS, 2, DA_HEAD_DIM)
    v = jnp.pad(v, pad).reshape(B_, Lp, DA_HEADS, 2 * DA_HEAD_DIM)
    f32 = jnp.float32
    lam = (jnp.exp(jnp.sum(lq1.astype(f32) * lk1.astype(f32)))
           - jnp.exp(jnp.sum(lq2.astype(f32) * lk2.astype(f32))) + lam_init)
    key_pos = jnp.arange(Lp, dtype=jnp.int32)
    key_chunk = chunk_id(key_pos)
    n_blk = Lp // Q_BLOCK
    qb = q.reshape(B_, n_blk, Q_BLOCK, DA_HEADS, 2, DA_HEAD_DIM).transpose(1, 0, 2, 3, 4, 5)
    scale = DA_HEAD_DIM ** -0.5

    def one_block(args):
        qi, blk = args
        q_pos = blk * Q_BLOCK + jnp.arange(Q_BLOCK, dtype=jnp.int32)
        s = jnp.einsum('bqhcd,bkhcd->bhcqk', qi, k).astype(f32) * scale
        bias = rel_bias_table.astype(f32)[t5_bucket(key_pos[None, :] - q_pos[:, None])]
        s = s + bias.transpose(2, 0, 1)[None, :, None]
        mask = key_chunk[None, :] <= chunk_id(q_pos)[:, None]
        p = jax.nn.softmax(jnp.where(mask, s, -jnp.inf), axis=-1)
        wts = p[:, :, 0] - lam * p[:, :, 1]
        return jnp.einsum('bhqk,bkhe->bqhe', wts.astype(v.dtype), v)

    o = lax.map(one_block, (qb, jnp.arange(n_blk, dtype=jnp.int32)))
    o = o.transpose(1, 0, 2, 3, 4).reshape(B_, Lp, DA_HEADS, 2 * DA_HEAD_DIM)[:, :L]
    o = rmsnorm(o, subln_g, SUBLN_EPS) * (1.0 - lam_init)
    return o.reshape(B_, L, D_MODEL) @ w_o


def conformer_conv(h, w_in, b_in, dw_w, dw_b, ln_g, ln_b, w_out, b_out):
    a, gate = jnp.split(h @ w_in + b_in, 2, axis=-1)
    u = a * jax.nn.sigmoid(gate)
    u = jnp.pad(u, ((0, 0), (CONV_WIDTH - 1, 0), (0, 0)))
    u = lax.conv_general_dilated(u, dw_w[:, None, :], window_strides=(1,), padding='VALID',
                                 dimension_numbers=('NWC', 'WIO', 'NWC'),
                                 feature_group_count=D_MODEL) + dw_b
    u = jax.nn.silu(layernorm(u, ln_g, ln_b))
    return u @ w_out + b_out


def swiglu_ffn(h, w_gate, w_up, w_down):
    return (jax.nn.silu(h @ w_gate) * (h @ w_up)) @ w_down


def _normal(k, shape, scale):
    return jax.random.normal(k, shape, jnp.float32) * scale


def setup_inputs(seed: int = 0) -> dict:
    key = jax.random.key(seed)
    ks = jax.random.split(key, 32)
    D, F = D_MODEL, FFN_HIDDEN
    return {
        'x': _normal(ks[0], (BATCH, SEQ, D), 1.0),
        'meta_tokens': _normal(ks[1], (N_META, D), 1.0),
        'norm_mix_g': 1.0 + _normal(ks[2], (DEPTH, D), 0.05),
        'norm_ffn_g': 1.0 + _normal(ks[3], (DEPTH, D), 0.05),
        'final_norm_g': 1.0 + _normal(ks[4], (D,), 0.05),
        'rel_bias_table': _normal(ks[5], (REL_BUCKETS, DA_HEADS), 0.5),
        'pool_w': _normal(ks[6], (N_POOL_LAYERS, len(POOL_WINDOWS), POOL_GROUP, POOL_GROUP), POOL_GROUP ** -0.5),
        'pool_b': _normal(ks[7], (N_POOL_LAYERS, D), 0.02),
        'pool_scale': 0.5 + _normal(ks[8], (N_POOL_LAYERS, D), 0.05),
        'attn_w_qkv': _normal(ks[9], (N_DIFF_LAYERS, D, 3 * D), D ** -0.5),
        'attn_w_o': _normal(ks[10], (N_DIFF_LAYERS, D, D), D ** -0.5),
        'attn_lambda_q1': _normal(ks[11], (N_DIFF_LAYERS, DA_HEAD_DIM), 0.1),
        'attn_lambda_k1': _normal(ks[12], (N_DIFF_LAYERS, DA_HEAD_DIM), 0.1),
        'attn_lambda_q2': _normal(ks[13], (N_DIFF_LAYERS, DA_HEAD_DIM), 0.1),
        'attn_lambda_k2': _normal(ks[14], (N_DIFF_LAYERS, DA_HEAD_DIM), 0.1),
        'attn_subln_g': 1.0 + _normal(ks[15], (N_DIFF_LAYERS, 2 * DA_HEAD_DIM), 0.05),
        'conv_w_in': _normal(ks[16], (N_CONV_LAYERS, D, 2 * D), D ** -0.5),
        'conv_b_in': _normal(ks[17], (N_CONV_LAYERS, 2 * D), 0.02),
        'conv_dw_w': _normal(ks[18], (N_CONV_LAYERS, CONV_WIDTH, D), CONV_WIDTH ** -0.5),
        'conv_dw_b': _normal(ks[19], (N_CONV_LAYERS, D), 0.02),
        'conv_ln_g': 1.0 + _normal(ks[20], (N_CONV_LAYERS, D), 0.05),
        'conv_ln_b': _normal(ks[21], (N_CONV_LAYERS, D), 0.02),
        'conv_w_out': _normal(ks[22], (N_CONV_LAYERS, D, D), D ** -0.5),
        'conv_b_out': _normal(ks[23], (N_CONV_LAYERS, D), 0.02),
        'ffn_w_gate': _normal(ks[24], (DEPTH, D, F), D ** -0.5),
        'ffn_w_up': _normal(ks[25], (DEPTH, D, F), D ** -0.5),
        'ffn_w_down': _normal(ks[26], (DEPTH, F, D), F ** -0.5),
    }


def reference(x, meta_tokens, norm_mix_g, norm_ffn_g, final_norm_g, rel_bias_table,
              pool_w, pool_b, pool_scale,
              attn_w_qkv, attn_w_o, attn_lambda_q1, attn_lambda_k1, attn_lambda_q2, attn_lambda_k2, attn_subln_g,
              conv_w_in, conv_b_in, conv_dw_w, conv_dw_b, conv_ln_g, conv_ln_b, conv_w_out, conv_b_out,
              ffn_w_gate, ffn_w_up, ffn_w_down):
    B_ = x.shape[0]
    meta = jnp.broadcast_to(meta_tokens[None].astype(x.dtype), (B_, N_META, D_MODEL))
    h = jnp.concatenate([meta, x], axis=1)
    for i in range(DEPTH):
        mixer, j = i % N_MIXERS, i // N_MIXERS
        y = rmsnorm(h, norm_mix_g[i])
        if mixer == 0:
            y = pool_mixer(y, pool_w[j], pool_b[j], pool_scale[j])
        elif mixer == 1:
            lam_init = 0.8 - 0.6 * math.exp(-0.3 * i)
            y = diff_attention(y, attn_w_qkv[j], attn_w_o[j], attn_lambda_q1[j], attn_lambda_k1[j],
                               attn_lambda_q2[j], attn_lambda_k2[j], attn_subln_g[j], rel_bias_table, lam_init)
        else:
            y = conformer_conv(y, conv_w_in[j], conv_b_in[j], conv_dw_w[j], conv_dw_b[j],
                               conv_ln_g[j], conv_ln_b[j], conv_w_out[j], conv_b_out[j])
        h = h + y
        h = h + swiglu_ffn(rmsnorm(h, norm_ffn_g[i]), ffn_w_gate[i], ffn_w_up[i], ffn_w_down[i])
    return rmsnorm(h[:, N_META:], final_norm_g)
```

```python
import functools
import math

import jax
import jax.numpy as jnp
from jax import lax
from jax.experimental import pallas as pl
from jax.experimental.pallas import tpu as pltpu

F32 = jnp.float32
BF16 = jnp.bfloat16

N_META = 16
CHUNK = 64
POOL_WINDOWS = (2, 4, 8, 16)
DA_HEADS = 8
REL_BUCKETS = 32
REL_MAX_DIST = 128
CONV_WIDTH = 31
RMS_EPS = 1e-6
LN_EPS = 1e-5
SUBLN_EPS = 1e-5
N_MIXERS = 3

LANES = 128
SUBLANES = 8
MXU_DIM = 256
VMEM_LIMIT_BYTES = 56 * 1024 * 1024

T_SEQ = 512
META_BLK = 128
POOL_HALO = 16
CONV_HALO = 32
CONV_ROWS = 32
NEG = -1e30
LOG2E = math.log2(math.e)
FAR_BUCKET = REL_BUCKETS // 2 - 1
N_BIAS_BLK = 2 * (T_SEQ // META_BLK) + 3


def _ffn_chunks(f):
    step = 3 * MXU_DIM
    return [(s, min(s + step, f)) for s in range(0, f, step)]


def _rms(x, g, eps):
    ms = jnp.mean(x * x, axis=-1, keepdims=True)
    return (x * lax.rsqrt(ms + eps)) * g


def _ffn(hmid, g_ref, wg_ref, wu_ref, wd_ref):
    x = _rms(hmid, g_ref[...], RMS_EPS).astype(BF16)
    acc = None
    for f0, f1 in _ffn_chunks(wg_ref.shape[1]):
        gate = jnp.dot(x, wg_ref[:, f0:f1], preferred_element_type=F32)
        up = jnp.dot(x, wu_ref[:, f0:f1], preferred_element_type=F32)
        act = ((gate * jax.nn.sigmoid(gate)) * up).astype(BF16)
        part = jnp.dot(act, wd_ref[f0:f1, :], preferred_element_type=F32)
        acc = part if acc is None else acc + part
    return hmid + acc


def _positions(tile_idx, rows):
    r = lax.broadcasted_iota(jnp.int32, (rows, 1), 0)
    return tile_idx * rows + r - (T_SEQ - N_META)


def _pool_ffn_kernel(h_ref, gm_ref, pw_ref, pb_ref, ps_ref, gf_ref, wg_ref, wu_ref, wd_ref, *rest,
                     final):
    if final:
        fg_ref, o_ref, halo_ref = rest
    else:
        o_ref, halo_ref = rest
    i = pl.program_id(1)
    h = h_ref[0]
    rows, d = h.shape
    pos = _positions(i, rows)
    y = jnp.where(pos >= 0, _rms(h, gm_ref[...], RMS_EPS), 0.0)

    @pl.when(i == 0)
    def _():
        halo_ref[...] = jnp.zeros_like(halo_ref)

    ext = jnp.concatenate([halo_ref[...], y], axis=0)
    halo_ref[...] = y[rows - POOL_HALO:, :]

    group = d // len(POOL_WINDOWS)
    mixed = []
    for gi, win in enumerate(POOL_WINDOWS):
        c0 = gi * group
        s = ext[:, c0:c0 + group]
        shift = 1
        while shift < win:
            s = s + pltpu.roll(s, shift, axis=0)
            shift *= 2
        cnt = jnp.clip(pos + 1, 1, win).astype(F32)
        pooled = s[POOL_HALO:, :] / cnt - y[:, c0:c0 + group]
        mixed.append(jnp.dot(pooled.astype(BF16), pw_ref[gi], preferred_element_type=F32))
    mixed = jnp.concatenate(mixed, axis=1)
    hmid = h + (mixed + pb_ref[...]) * ps_ref[...]
    out = _ffn(hmid, gf_ref, wg_ref, wu_ref, wd_ref)
    if final:
        out = _rms(out, fg_ref[...], RMS_EPS)
    o_ref[0] = out


def _const_spec(shape):
    nd = len(shape)
    return pl.BlockSpec(shape, lambda *_: (0,) * nd, pipeline_mode=pl.Buffered(1))


def _pool_ffn(h, gm, pw, pb, ps, gf, wg, wu, wd, final_g=None):
    b, lh, d = h.shape
    nt = lh // T_SEQ
    final = final_g is not None
    tile = pl.BlockSpec((1, T_SEQ, d), lambda bi, i: (bi, i, 0))
    in_specs = [tile, _const_spec(gm.shape), _const_spec(pw.shape), _const_spec(pb.shape),
                _const_spec(ps.shape), _const_spec(gf.shape), _const_spec(wg.shape),
                _const_spec(wu.shape), _const_spec(wd.shape)]
    args = [h, gm, pw, pb, ps, gf, wg, wu, wd]
    if final:
        in_specs.append(_const_spec(final_g.shape))
        args.append(final_g)
        out_spec = pl.BlockSpec((1, T_SEQ, d), lambda bi, i: (bi, jnp.maximum(i - 1, 0), 0))
        out_shape = jax.ShapeDtypeStruct((b, lh - T_SEQ, d), F32)
    else:
        out_spec = tile
        out_shape = jax.ShapeDtypeStruct((b, lh, d), F32)
    return pl.pallas_call(
        functools.partial(_pool_ffn_kernel, final=final),
        grid=(b, nt),
        in_specs=in_specs,
        out_specs=out_spec,
        out_shape=out_shape,
        scratch_shapes=[pltpu.VMEM((POOL_HALO, d), F32)],
        compiler_params=pltpu.CompilerParams(
            dimension_semantics=("arbitrary", "arbitrary"), vmem_limit_bytes=VMEM_LIMIT_BYTES),
        name="pool_ffn_final" if final else "pool_ffn",
    )(*args)


def _qkv_kernel(h_ref, g_ref, w_ref, q_ref, k_ref, vt_ref, *, qscale):
    h = h_ref[0]
    d = h.shape[1]
    y = _rms(h, g_ref[...], RMS_EPS).astype(BF16)
    q = jnp.dot(y, w_ref[:, 0:d], preferred_element_type=F32) * qscale
    k = jnp.dot(y, w_ref[:, d:2 * d], preferred_element_type=F32)
    v = jnp.dot(y, w_ref[:, 2 * d:3 * d], preferred_element_type=F32)
    q_ref[0] = q.astype(BF16)
    k_ref[0] = k.astype(BF16)
    vt_ref[0, 0] = v.T.astype(BF16)


def _qkv(h, g, w, qscale):
    b, lh, d = h.shape
    nt = lh // T_SEQ
    tile = pl.BlockSpec((1, T_SEQ, d), lambda bi, i: (bi, i, 0))
    return pl.pallas_call(
        functools.partial(_qkv_kernel, qscale=qscale),
        grid=(b, nt),
        in_specs=[tile, _const_spec(g.shape), _const_spec(w.shape)],
        out_specs=[tile, tile, pl.BlockSpec((1, 1, d, T_SEQ), lambda bi, i: (bi, i, 0, 0))],
        out_shape=[jax.ShapeDtypeStruct((b, lh, d), BF16), jax.ShapeDtypeStruct((b, lh, d), BF16),
                   jax.ShapeDtypeStruct((b, nt, d, T_SEQ), BF16)],
        compiler_params=pltpu.CompilerParams(
            dimension_semantics=("arbitrary", "arbitrary"), vmem_limit_bytes=VMEM_LIMIT_BYTES),
        name="qkv",
    )(h, g, w)


def _t5_bucket(rel):
    half = REL_BUCKETS // 2
    max_exact = half // 2
    offset = jnp.where(rel > 0, half, 0)
    n = jnp.abs(rel)
    log_ratio = jnp.log(jnp.maximum(n, 1).astype(F32) / max_exact) / math.log(REL_MAX_DIST / max_exact)
    large = jnp.minimum(max_exact + (log_ratio * (half - max_exact)).astype(jnp.int32), half - 1)
    return offset + jnp.where(n < max_exact, n, large)


def _bias_buckets():
    t = T_SEQ
    kl = lax.broadcasted_iota(jnp.int32, (t, t), 0)
    ql = lax.broadcasted_iota(jnp.int32, (t, t), 1)
    prev = _t5_bucket(kl - t - ql)
    diag = jnp.where(kl // CHUNK <= ql // CHUNK, _t5_bucket(kl - ql), -1)
    kr = lax.broadcasted_iota(jnp.int32, (META_BLK, t), 0)
    qc = lax.broadcasted_iota(jnp.int32, (META_BLK, t), 1)
    mk = kr - (META_BLK - N_META)
    mq = qc - (t - N_META)
    lead = jnp.where(mq >= 0, _t5_bucket(mk - mq), FAR_BUCKET)
    first = _t5_bucket(mk - N_META - qc)
    later = jnp.full((META_BLK, t), FAR_BUCKET, jnp.int32)
    meta = [jnp.where(mk >= 0, m, -1) for m in (lead, first, later)]
    nb = t // META_BLK
    return jnp.concatenate([prev.reshape(nb, META_BLK, t), diag.reshape(nb, META_BLK, t),
                            jnp.stack(meta)], axis=0)


def _bias_kernel(tab_ref, bk_ref, o_ref):
    hh = pl.program_id(0)
    far = tab_ref[FAR_BUCKET, hh]
    for blk in range(bk_ref.shape[0]):
        b = bk_ref[blk]
        acc = jnp.zeros(b.shape, F32)
        for t in range(REL_BUCKETS):
            acc = jnp.where(b == t, tab_ref[t, hh] - far, acc)
        o_ref[0, blk] = jnp.where(b < 0, NEG, acc * LOG2E)


def _bias_tiles(table, buckets):
    nb, r, c = buckets.shape
    heads = table.shape[1]
    return pl.pallas_call(
        _bias_kernel,
        grid=(heads,),
        in_specs=[pl.BlockSpec(memory_space=pltpu.SMEM),
                  pl.BlockSpec((nb, r, c), lambda hh: (0, 0, 0))],
        out_specs=pl.BlockSpec((1, nb, r, c), lambda hh: (hh, 0, 0, 0)),
        out_shape=jax.ShapeDtypeStruct((heads, nb, r, c), F32),
        compiler_params=pltpu.CompilerParams(dimension_semantics=("arbitrary",)),
        name="rel_bias",
    )(table, buckets)


def _attn_kernel(q_ref, k_ref, vt_ref, bnear_ref, bmeta_ref, lam_ref, sg_ref, o_ref,
                 qzz_ref, m_ref, l_ref, acc_ref, *, lam_init):
    a = pl.program_id(2)
    t = q_ref.shape[1]
    hd = q_ref.shape[2]

    qt = q_ref[0].astype(F32).T
    row = lax.broadcasted_iota(jnp.int32, (hd, t), 0)
    q1 = jnp.where(row < hd // 2, qt, 0.0)
    q2 = jnp.where(row >= hd // 2, qt, 0.0)
    qzz_ref[...] = jnp.concatenate([q1, q2], axis=1).astype(BF16)

    def scores(kt, bias):
        s = jnp.dot(kt, qzz_ref[...], preferred_element_type=F32)
        if bias is not None:
            s = s + jnp.concatenate([bias, bias], axis=1)
        return s

    def accumulate(s, vt):
        m_old = m_ref[...]
        m_new = jnp.maximum(m_old, jnp.max(s, axis=0, keepdims=True))
        alpha = jnp.exp2(m_old - m_new)
        p = jnp.exp2(s - m_new)
        l_ref[...] = alpha * l_ref[...] + jnp.sum(p, axis=0, keepdims=True)
        acc_ref[...] = alpha * acc_ref[...] + jnp.dot(vt, p.astype(BF16), preferred_element_type=F32)
        m_ref[...] = m_new

    s0 = scores(k_ref[0, t - META_BLK:t, :], bmeta_ref[0, 0])
    m0 = jnp.max(s0, axis=0, keepdims=True)
    p0 = jnp.exp2(s0 - m0)
    m_ref[...] = m0
    l_ref[...] = jnp.sum(p0, axis=0, keepdims=True)
    acc_ref[...] = jnp.dot(vt_ref[0, 0, :, t - META_BLK:t], p0.astype(BF16),
                           preferred_element_type=F32)

    def key_tile(j):
        return k_ref[0, pl.ds(pl.multiple_of(j * t, t), t), :]

    def far_body(j, carry):
        accumulate(scores(key_tile(j), None), vt_ref[0, j])
        return carry

    lax.fori_loop(1, jnp.maximum(a - 1, 1), far_body, 0)

    nb = t // META_BLK

    @pl.when(a >= 2)
    def _():
        bias = bnear_ref[0, 0:nb].reshape(t, t)
        accumulate(scores(key_tile(a - 1), bias), vt_ref[0, a - 1])

    @pl.when(a >= 1)
    def _():
        bias = bnear_ref[0, nb:2 * nb].reshape(t, t)
        accumulate(scores(key_tile(a), bias), vt_ref[0, a])

    lv = lam_ref[...]
    lam = (jnp.exp(jnp.sum(lv[0:1] * lv[1:2], axis=1, keepdims=True))
           - jnp.exp(jnp.sum(lv[2:3] * lv[3:4], axis=1, keepdims=True)) + lam_init)
    on = acc_ref[...] / l_ref[...]
    o = on[:, :t] - lam * on[:, t:]
    o = (o * lax.rsqrt(jnp.mean(o * o, axis=0, keepdims=True) + SUBLN_EPS)) * sg_ref[...]
    o = o * (1.0 - lam_init)
    o_ref[0] = o.T.astype(BF16)


def _attention(q, k, vt, bias, lamv, sg, lam_init):
    b, lh, d = q.shape
    nt = lh // T_SEQ
    hd = d // DA_HEADS
    nb = T_SEQ // META_BLK
    qtile = pl.BlockSpec((1, T_SEQ, hd), lambda bi, hh, a: (bi, a, hh))
    return pl.pallas_call(
        functools.partial(_attn_kernel, lam_init=lam_init),
        grid=(b, DA_HEADS, nt),
        in_specs=[
            qtile,
            pl.BlockSpec((1, lh, hd), lambda bi, hh, a: (bi, 0, hh)),
            pl.BlockSpec((1, nt, hd, T_SEQ), lambda bi, hh, a: (bi, 0, hh, 0)),
            pl.BlockSpec((1, 2 * nb, META_BLK, T_SEQ), lambda bi, hh, a: (hh, 0, 0, 0)),
            pl.BlockSpec((1, 1, META_BLK, T_SEQ),
                         lambda bi, hh, a: (hh, 2 * nb + jnp.minimum(a, 2), 0, 0)),
            pl.BlockSpec(lamv.shape, lambda bi, hh, a: (0, 0)),
            pl.BlockSpec(sg.shape, lambda bi, hh, a: (0, 0)),
        ],
        out_specs=qtile,
        out_shape=jax.ShapeDtypeStruct((b, lh, d), BF16),
        scratch_shapes=[pltpu.VMEM((hd, 2 * T_SEQ), BF16), pltpu.VMEM((1, 2 * T_SEQ), F32),
                        pltpu.VMEM((1, 2 * T_SEQ), F32), pltpu.VMEM((hd, 2 * T_SEQ), F32)],
        compiler_params=pltpu.CompilerParams(
            dimension_semantics=("arbitrary", "arbitrary", "arbitrary"),
            vmem_limit_bytes=VMEM_LIMIT_BYTES),
        name="diff_attn",
    )(q, k, vt, bias, bias, lamv, sg)


def _attn_out_ffn_kernel(h_ref, o_ref, wo_ref, gf_ref, wg_ref, wu_ref, wd_ref, out_ref):
    hmid = h_ref[...] + jnp.dot(o_ref[...], wo_ref[...], preferred_element_type=F32)
    out_ref[...] = _ffn(hmid, gf_ref, wg_ref, wu_ref, wd_ref)


def _attn_out_ffn(h, o, wo, gf, wg, wu, wd):
    b, lh, d = h.shape
    rows = b * lh
    tile_f32 = pl.BlockSpec((T_SEQ, d), lambda i: (i, 0))
    out = pl.pallas_call(
        _attn_out_ffn_kernel,
        grid=(rows // T_SEQ,),
        in_specs=[tile_f32, pl.BlockSpec((T_SEQ, d), lambda i: (i, 0)), _const_spec(wo.shape),
                  _const_spec(gf.shape), _const_spec(wg.shape), _const_spec(wu.shape),
                  _const_spec(wd.shape)],
        out_specs=tile_f32,
        out_shape=jax.ShapeDtypeStruct((rows, d), F32),
        compiler_params=pltpu.CompilerParams(
            dimension_semantics=("arbitrary",), vmem_limit_bytes=VMEM_LIMIT_BYTES),
        name="attn_out_ffn",
    )(h.reshape(rows, d), o.reshape(rows, d), wo, gf, wg, wu, wd)
    return out.reshape(b, lh, d)


def _conv_ffn_kernel(h_ref, gm_ref, win_ref, bin_ref, dww_ref, dwb_ref, lng_ref, lnb_ref,
                     wout_ref, bout_ref, gf_ref, wg_ref, wu_ref, wd_ref, o_ref, uext_ref, conv_ref):
    i = pl.program_id(1)
    h = h_ref[0]
    rows, d = h.shape
    pos = _positions(i, rows)
    y = _rms(h, gm_ref[...], RMS_EPS).astype(BF16)
    ag = jnp.dot(y, win_ref[...], preferred_element_type=F32) + bin_ref[...]
    u = ag[:, :d] * jax.nn.sigmoid(ag[:, d:])
    u = jnp.where(pos >= 0, u, 0.0)

    @pl.when(i == 0)
    def _():
        uext_ref[0:CONV_HALO, :] = jnp.zeros((CONV_HALO, d), F32)
        uext_ref[CONV_HALO + rows:, :] = jnp.zeros((SUBLANES, d), F32)

    uext_ref[CONV_HALO:CONV_HALO + rows, :] = u

    base = CONV_HALO - (CONV_WIDTH - 1)
    span = CONV_ROWS + SUBLANES
    n_oct = (base + CONV_WIDTH - 1) // SUBLANES + 1

    def conv_body(c, carry):
        r0 = pl.multiple_of(c * CONV_ROWS, CONV_ROWS)
        for lb in range(d // LANES):
            cs = slice(lb * LANES, (lb + 1) * LANES)
            um = [uext_ref[pl.ds(r0 + SUBLANES * m, span), cs] for m in range(n_oct)]
            out = None
            for sh in range(SUBLANES):
                part = None
                for m in range(n_oct):
                    j = SUBLANES * m + sh - base
                    if 0 <= j < CONV_WIDTH:
                        term = dww_ref[j:j + 1, cs] * um[m]
                        part = term if part is None else part + term
                if sh:
                    part = pltpu.roll(part, span - sh, axis=0)
                out = part if out is None else out + part
            conv_ref[pl.ds(r0, CONV_ROWS), cs] = out[:CONV_ROWS] + dwb_ref[:, cs]
        return carry

    lax.fori_loop(0, rows // CONV_ROWS, conv_body, 0)
    uext_ref[0:CONV_HALO, :] = uext_ref[rows:rows + CONV_HALO, :]

    c = conv_ref[...]
    mu = jnp.mean(c, axis=-1, keepdims=True)
    cc = c - mu
    var = jnp.mean(cc * cc, axis=-1, keepdims=True)
    z = (cc * lax.rsqrt(var + LN_EPS)) * lng_ref[...] + lnb_ref[...]
    z = (z * jax.nn.sigmoid(z)).astype(BF16)
    hmid = h + jnp.dot(z, wout_ref[...], preferred_element_type=F32) + bout_ref[...]
    o_ref[0] = _ffn(hmid, gf_ref, wg_ref, wu_ref, wd_ref)


def _conv_ffn(h, gm, win, bin_, dww, dwb, lng, lnb, wout, bout, gf, wg, wu, wd):
    b, lh, d = h.shape
    nt = lh // T_SEQ
    tile = pl.BlockSpec((1, T_SEQ, d), lambda bi, i: (bi, i, 0))
    consts = [gm, win, bin_, dww, dwb, lng, lnb, wout, bout, gf, wg, wu, wd]
    return pl.pallas_call(
        _conv_ffn_kernel,
        grid=(b, nt),
        in_specs=[tile] + [_const_spec(c.shape) for c in consts],
        out_specs=tile,
        out_shape=jax.ShapeDtypeStruct((b, lh, d), F32),
        scratch_shapes=[pltpu.VMEM((T_SEQ + CONV_HALO + SUBLANES, d), F32), pltpu.VMEM((T_SEQ, d), F32)],
        compiler_params=pltpu.CompilerParams(
            dimension_semantics=("arbitrary", "arbitrary"), vmem_limit_bytes=VMEM_LIMIT_BYTES),
        name="conv_ffn",
    )(h, *consts)


def kernel(x, meta_tokens, norm_mix_g, norm_ffn_g, final_norm_g, rel_bias_table, pool_w, pool_b, pool_scale, attn_w_qkv, attn_w_o, attn_lambda_q1, attn_lambda_k1, attn_lambda_q2, attn_lambda_k2, attn_subln_g, conv_w_in, conv_b_in, conv_dw_w, conv_dw_b, conv_ln_g, conv_ln_b, conv_w_out, conv_b_out, ffn_w_gate, ffn_w_up, ffn_w_down):
    b, seq, d = x.shape
    depth = norm_mix_g.shape[0]
    assert seq % T_SEQ == 0 and d % (LANES * len(POOL_WINDOWS)) == 0
    assert depth % N_MIXERS == 1, "the last layer must be a pooling layer (it applies the final norm)"

    row = lambda v: v.reshape(1, -1).astype(F32)
    meta = jnp.broadcast_to(meta_tokens[None].astype(F32), (b, N_META, d))
    h = jnp.concatenate([jnp.zeros((b, T_SEQ - N_META, d), F32), meta, x], axis=1)

    bias = None
    for i in range(depth):
        mixer, j = i % N_MIXERS, i // N_MIXERS
        gm, gf = row(norm_mix_g[i]), row(norm_ffn_g[i])
        wg, wu, wd = (w[i].astype(BF16) for w in (ffn_w_gate, ffn_w_up, ffn_w_down))
        if mixer == 0:
            final_g = row(final_norm_g) if i == depth - 1 else None
            h = _pool_ffn(h, gm, pool_w[j].astype(BF16), row(pool_b[j]), row(pool_scale[j]),
                          gf, wg, wu, wd, final_g)
        elif mixer == 1:
            lam_init = 0.8 - 0.6 * math.exp(-0.3 * i)
            hd = d // DA_HEADS
            if bias is None:
                bias = _bias_tiles(rel_bias_table.astype(F32), _bias_buckets())
            qscale = (hd // 2) ** -0.5 * LOG2E
            q, k, vt = _qkv(h, gm, attn_w_qkv[j].astype(BF16), qscale)
            lamv = jnp.stack([attn_lambda_q1[j], attn_lambda_k1[j],
                              attn_lambda_q2[j], attn_lambda_k2[j]]).astype(F32)
            o = _attention(q, k, vt, bias, lamv, attn_subln_g[j].reshape(hd, 1).astype(F32), lam_init)
            h = _attn_out_ffn(h, o, attn_w_o[j].astype(BF16), gf, wg, wu, wd)
        else:
            h = _conv_ffn(h, gm, conv_w_in[j].astype(BF16), row(conv_b_in[j]),
                          conv_dw_w[j].astype(F32), row(conv_dw_b[j]), row(conv_ln_g[j]),
                          row(conv_ln_b[j]), conv_w_out[j].astype(BF16), row(conv_b_out[j]),
                          gf, wg, wu, wd)
    return h
```

```python
import functools
import math

import jax
import jax.numpy as jnp
from jax import lax
from jax.experimental import pallas as pl
from jax.experimental.pallas import tpu as pltpu

F32 = jnp.float32
BF16 = jnp.bfloat16

N_META = 16
CHUNK = 64
POOL_WINDOWS = (2, 4, 8, 16)
DA_HEADS = 8
REL_BUCKETS = 32
REL_MAX_DIST = 128
CONV_WIDTH = 31
RMS_EPS = 1e-6
LN_EPS = 1e-5
SUBLN_EPS = 1e-5
N_MIXERS = 3

LANES = 128
SUBLANES = 8
MXU_DIM = 256
VMEM_LIMIT_BYTES = 56 * 1024 * 1024

T_SEQ = 512
META_BLK = 128
POOL_HALO = 16
CONV_HALO = 32
CONV_ROWS = 32
NEG = -1e30
LOG2E = math.log2(math.e)
FAR_BUCKET = REL_BUCKETS // 2 - 1


def _ffn_chunks(f):
    step = 3 * MXU_DIM
    return [(s, min(s + step, f)) for s in range(0, f, step)]


def _rms(x, g, eps):
    ms = jnp.mean(x * x, axis=-1, keepdims=True)
    return (x * lax.rsqrt(ms + eps)) * g


def _ffn(hmid, g_ref, wg_ref, wu_ref, wd_ref):
    x = _rms(hmid, g_ref[...], RMS_EPS).astype(BF16)
    acc = None
    for f0, f1 in _ffn_chunks(wg_ref.shape[1]):
        gate = jnp.dot(x, wg_ref[:, f0:f1], preferred_element_type=F32)
        up = jnp.dot(x, wu_ref[:, f0:f1], preferred_element_type=F32)
        act = ((gate * jax.nn.sigmoid(gate)) * up).astype(BF16)
        part = jnp.dot(act, wd_ref[f0:f1, :], preferred_element_type=F32)
        acc = part if acc is None else acc + part
    return hmid + acc


def _positions(tile_idx, rows):
    r = lax.broadcasted_iota(jnp.int32, (rows, 1), 0)
    return tile_idx * rows + r - (T_SEQ - N_META)


def _pool_ffn_kernel(h_ref, gm_ref, pw_ref, pb_ref, ps_ref, gf_ref, wg_ref, wu_ref, wd_ref, *rest,
                     final):
    if final:
        fg_ref, o_ref, halo_ref = rest
    else:
        o_ref, halo_ref = rest
    i = pl.program_id(1)
    h = h_ref[0]
    rows, d = h.shape
    pos = _positions(i, rows)
    y = jnp.where(pos >= 0, _rms(h, gm_ref[...], RMS_EPS), 0.0)

    @pl.when(i == 0)
    def _():
        halo_ref[...] = jnp.zeros_like(halo_ref)

    ext = jnp.concatenate([halo_ref[...], y], axis=0)
    halo_ref[...] = y[rows - POOL_HALO:, :]

    group = d // len(POOL_WINDOWS)
    mixed = []
    for gi, win in enumerate(POOL_WINDOWS):
        c0 = gi * group
        s = ext[:, c0:c0 + group]
        shift = 1
        while shift < win:
            s = s + pltpu.roll(s, shift, axis=0)
            shift *= 2
        cnt = jnp.clip(pos + 1, 1, win).astype(F32)
        pooled = s[POOL_HALO:, :] / cnt - y[:, c0:c0 + group]
        mixed.append(jnp.dot(pooled.astype(BF16), pw_ref[gi], preferred_element_type=F32))
    mixed = jnp.concatenate(mixed, axis=1)
    hmid = h + (mixed + pb_ref[...]) * ps_ref[...]
    out = _ffn(hmid, gf_ref, wg_ref, wu_ref, wd_ref)
    if final:
        out = _rms(out, fg_ref[...], RMS_EPS)
    o_ref[0] = out


def _const_spec(shape):
    nd = len(shape)
    return pl.BlockSpec(shape, lambda *_: (0,) * nd, pipeline_mode=pl.Buffered(1))


def _pool_ffn(h, gm, pw, pb, ps, gf, wg, wu, wd, final_g=None):
    b, lh, d = h.shape
    nt = lh // T_SEQ
    final = final_g is not None
    tile = pl.BlockSpec((1, T_SEQ, d), lambda bi, i: (bi, i, 0))
    in_specs = [tile, _const_spec(gm.shape), _const_spec(pw.shape), _const_spec(pb.shape),
                _const_spec(ps.shape), _const_spec(gf.shape), _const_spec(wg.shape),
                _const_spec(wu.shape), _const_spec(wd.shape)]
    args = [h, gm, pw, pb, ps, gf, wg, wu, wd]
    if final:
        in_specs.append(_const_spec(final_g.shape))
        args.append(final_g)
        out_spec = pl.BlockSpec((1, T_SEQ, d), lambda bi, i: (bi, jnp.maximum(i - 1, 0), 0))
        out_shape = jax.ShapeDtypeStruct((b, lh - T_SEQ, d), F32)
    else:
        out_spec = tile
        out_shape = jax.ShapeDtypeStruct((b, lh, d), F32)
    return pl.pallas_call(
        functools.partial(_pool_ffn_kernel, final=final),
        grid=(b, nt),
        in_specs=in_specs,
        out_specs=out_spec,
        out_shape=out_shape,
        scratch_shapes=[pltpu.VMEM((POOL_HALO, d), F32)],
        compiler_params=pltpu.CompilerParams(
            dimension_semantics=("arbitrary", "arbitrary"), vmem_limit_bytes=VMEM_LIMIT_BYTES),
        name="pool_ffn_final" if final else "pool_ffn",
    )(*args)


def _qkv_kernel(h_ref, g_ref, w_ref, q_ref, k_ref, vt_ref, *, qscale):
    h = h_ref[0]
    d = h.shape[1]
    y = _rms(h, g_ref[...], RMS_EPS).astype(BF16)
    q = jnp.dot(y, w_ref[:, 0:d], preferred_element_type=F32) * qscale
    k = jnp.dot(y, w_ref[:, d:2 * d], preferred_element_type=F32)
    v = jnp.dot(y, w_ref[:, 2 * d:3 * d], preferred_element_type=F32)
    q_ref[0] = q.astype(BF16)
    k_ref[0] = k.astype(BF16)
    vt_ref[0, 0] = v.T.astype(BF16)


def _qkv(h, g, w, qscale):
    b, lh, d = h.shape
    nt = lh // T_SEQ
    tile = pl.BlockSpec((1, T_SEQ, d), lambda bi, i: (bi, i, 0))
    return pl.pallas_call(
        functools.partial(_qkv_kernel, qscale=qscale),
        grid=(b, nt),
        in_specs=[tile, _const_spec(g.shape), _const_spec(w.shape)],
        out_specs=[tile, tile, pl.BlockSpec((1, 1, d, T_SEQ), lambda bi, i: (bi, i, 0, 0))],
        out_shape=[jax.ShapeDtypeStruct((b, lh, d), BF16), jax.ShapeDtypeStruct((b, lh, d), BF16),
                   jax.ShapeDtypeStruct((b, nt, d, T_SEQ), BF16)],
        compiler_params=pltpu.CompilerParams(
            dimension_semantics=("arbitrary", "arbitrary"), vmem_limit_bytes=VMEM_LIMIT_BYTES),
        name="qkv",
    )(h, g, w)


def _t5_bucket(rel):
    half = REL_BUCKETS // 2
    max_exact = half // 2
    offset = jnp.where(rel > 0, half, 0)
    n = jnp.abs(rel)
    log_ratio = jnp.log(jnp.maximum(n, 1).astype(F32) / max_exact) / math.log(REL_MAX_DIST / max_exact)
    large = jnp.minimum(max_exact + (log_ratio * (half - max_exact)).astype(jnp.int32), half - 1)
    return offset + jnp.where(n < max_exact, n, large)


def _bias_buckets():
    t = T_SEQ
    kl = lax.broadcasted_iota(jnp.int32, (t, t), 0)
    ql = lax.broadcasted_iota(jnp.int32, (t, t), 1)
    prev = _t5_bucket(kl - t - ql)
    diag = jnp.where(kl // CHUNK <= ql // CHUNK, _t5_bucket(kl - ql), -1)
    kr = lax.broadcasted_iota(jnp.int32, (META_BLK, t), 0)
    qc = lax.broadcasted_iota(jnp.int32, (META_BLK, t), 1)
    mk = kr - (META_BLK - N_META)
    mq = qc - (t - N_META)
    lead = jnp.where(mq >= 0, _t5_bucket(mk - mq), FAR_BUCKET)
    first = _t5_bucket(mk - N_META - qc)
    later = jnp.full((META_BLK, t), FAR_BUCKET, jnp.int32)
    meta = [jnp.where(mk >= 0, m, -1) for m in (lead, first, later)]
    nb = t // META_BLK
    return jnp.concatenate([prev.reshape(nb, META_BLK, t), diag.reshape(nb, META_BLK, t),
                            jnp.stack(meta)], axis=0)


def _bias_kernel(tab_ref, bk_ref, o_ref):
    hh = pl.program_id(0)
    far = tab_ref[FAR_BUCKET, hh]
    for blk in range(bk_ref.shape[0]):
        b = bk_ref[blk]
        acc = jnp.zeros(b.shape, F32)
        for t in range(REL_BUCKETS):
            acc = jnp.where(b == t, tab_ref[t, hh] - far, acc)
        o_ref[0, blk] = jnp.where(b < 0, NEG, acc * LOG2E)


def _bias_tiles(table, buckets):
    nb, r, c = buckets.shape
    heads = table.shape[1]
    return pl.pallas_call(
        _bias_kernel,
        grid=(heads,),
        in_specs=[pl.BlockSpec(memory_space=pltpu.SMEM),
                  pl.BlockSpec((nb, r, c), lambda hh: (0, 0, 0))],
        out_specs=pl.BlockSpec((1, nb, r, c), lambda hh: (hh, 0, 0, 0)),
        out_shape=jax.ShapeDtypeStruct((heads, nb, r, c), F32),
        compiler_params=pltpu.CompilerParams(dimension_semantics=("arbitrary",)),
        name="rel_bias",
    )(table, buckets)


def _attn_kernel(q_ref, k_ref, vt_ref, bnear_ref, bmeta_ref, lam_ref, sg_ref, o_ref,
                 qzz_ref, m_ref, l_ref, acc_ref, sa_ref, sb_ref, *, lam_init):
    a = pl.program_id(2)
    t = q_ref.shape[1]
    hd = q_ref.shape[2]

    qt = q_ref[0].astype(F32).T
    row = lax.broadcasted_iota(jnp.int32, (hd, t), 0)
    q1 = jnp.where(row < hd // 2, qt, 0.0)
    q2 = jnp.where(row >= hd // 2, qt, 0.0)
    qzz_ref[...] = jnp.concatenate([q1, q2], axis=1).astype(BF16)

    def produce(j, buf):
        kt = k_ref[0, pl.ds(pl.multiple_of(j * t, t), t), :]
        s = jnp.dot(kt, qzz_ref[...], preferred_element_type=F32)
        buf[...] = s
        return jnp.max(s, axis=0, keepdims=True)

    def consume(s, mt, vt):
        m_old = m_ref[...]
        m_new = jnp.maximum(m_old, mt)
        alpha = jnp.exp2(m_old - m_new)
        p = jnp.exp2(s - m_new)
        l_ref[...] = alpha * l_ref[...] + jnp.sum(p, axis=0, keepdims=True)
        acc_ref[...] = alpha * acc_ref[...] + jnp.dot(vt, p.astype(BF16), preferred_element_type=F32)
        m_ref[...] = m_new

    def consume_near(buf, blk, j):
        bias = bnear_ref[0, blk * nb:(blk + 1) * nb].reshape(t, t)
        s = buf[...] + jnp.concatenate([bias, bias], axis=1)
        consume(s, jnp.max(s, axis=0, keepdims=True), vt_ref[0, j])

    nb = t // META_BLK

    s0 = jnp.dot(k_ref[0, t - META_BLK:t, :], qzz_ref[...], preferred_element_type=F32)
    s0 = s0 + jnp.concatenate([bmeta_ref[0, 0], bmeta_ref[0, 0]], axis=1)
    m0 = jnp.max(s0, axis=0, keepdims=True)
    p0 = jnp.exp2(s0 - m0)
    m_ref[...] = m0
    l_ref[...] = jnp.sum(p0, axis=0, keepdims=True)
    acc_ref[...] = jnp.dot(vt_ref[0, 0, :, t - META_BLK:t], p0.astype(BF16),
                           preferred_element_type=F32)

    @pl.when(a >= 2)
    def _():
        n_far = a - 2
        odd = lax.rem(n_far, 2)

        @pl.when(odd == 1)
        def _():
            mt = produce(1, sa_ref)
            consume(sa_ref[...], mt, vt_ref[0, 1])

        j0 = 1 + odd

        def pair_body(i, mta):
            j = j0 + 2 * i
            mtb = produce(j + 1, sb_ref)
            consume(sa_ref[...], mta, vt_ref[0, j])
            mta_next = produce(j + 2, sa_ref)
            consume(sb_ref[...], mtb, vt_ref[0, j + 1])
            return mta_next

        lax.fori_loop(0, n_far // 2, pair_body, produce(j0, sa_ref))
        produce(a, sb_ref)
        consume_near(sa_ref, 0, a - 1)
        consume_near(sb_ref, 1, a)

    @pl.when(a == 1)
    def _():
        produce(1, sb_ref)
        consume_near(sb_ref, 1, 1)

    lv = lam_ref[...]
    lam = (jnp.exp(jnp.sum(lv[0:1] * lv[1:2], axis=1, keepdims=True))
           - jnp.exp(jnp.sum(lv[2:3] * lv[3:4], axis=1, keepdims=True)) + lam_init)
    on = acc_ref[...] / l_ref[...]
    o = on[:, :t] - lam * on[:, t:]
    o = (o * lax.rsqrt(jnp.mean(o * o, axis=0, keepdims=True) + SUBLN_EPS)) * sg_ref[...]
    o = o * (1.0 - lam_init)
    o_ref[0] = o.T.astype(BF16)


def _attention(q, k, vt, bias, lamv, sg, lam_init):
    b, lh, d = q.shape
    nt = lh // T_SEQ
    hd = d // DA_HEADS
    nb = T_SEQ // META_BLK
    qtile = pl.BlockSpec((1, T_SEQ, hd), lambda bi, hh, a: (bi, a, hh))
    return pl.pallas_call(
        functools.partial(_attn_kernel, lam_init=lam_init),
        grid=(b, DA_HEADS, nt),
        in_specs=[
            qtile,
            pl.BlockSpec((1, lh, hd), lambda bi, hh, a: (bi, 0, hh)),
            pl.BlockSpec((1, nt, hd, T_SEQ), lambda bi, hh, a: (bi, 0, hh, 0)),
            pl.BlockSpec((1, 2 * nb, META_BLK, T_SEQ), lambda bi, hh, a: (hh, 0, 0, 0)),
            pl.BlockSpec((1, 1, META_BLK, T_SEQ),
                         lambda bi, hh, a: (hh, 2 * nb + jnp.minimum(a, 2), 0, 0)),
            pl.BlockSpec(lamv.shape, lambda bi, hh, a: (0, 0)),
            pl.BlockSpec(sg.shape, lambda bi, hh, a: (0, 0)),
        ],
        out_specs=qtile,
        out_shape=jax.ShapeDtypeStruct((b, lh, d), BF16),
        scratch_shapes=[pltpu.VMEM((hd, 2 * T_SEQ), BF16), pltpu.VMEM((1, 2 * T_SEQ), F32),
                        pltpu.VMEM((1, 2 * T_SEQ), F32), pltpu.VMEM((hd, 2 * T_SEQ), F32),
                        pltpu.VMEM((T_SEQ, 2 * T_SEQ), F32), pltpu.VMEM((T_SEQ, 2 * T_SEQ), F32)],
        compiler_params=pltpu.CompilerParams(
            dimension_semantics=("arbitrary", "arbitrary", "arbitrary"),
            vmem_limit_bytes=VMEM_LIMIT_BYTES),
        name="diff_attn",
    )(q, k, vt, bias, bias, lamv, sg)


def _attn_out_ffn_kernel(h_ref, o_ref, wo_ref, gf_ref, wg_ref, wu_ref, wd_ref, out_ref):
    hmid = h_ref[...] + jnp.dot(o_ref[...], wo_ref[...], preferred_element_type=F32)
    out_ref[...] = _ffn(hmid, gf_ref, wg_ref, wu_ref, wd_ref)


def _attn_out_ffn(h, o, wo, gf, wg, wu, wd):
    b, lh, d = h.shape
    rows = b * lh
    tile_f32 = pl.BlockSpec((T_SEQ, d), lambda i: (i, 0))
    out = pl.pallas_call(
        _attn_out_ffn_kernel,
        grid=(rows // T_SEQ,),
        in_specs=[tile_f32, pl.BlockSpec((T_SEQ, d), lambda i: (i, 0)), _const_spec(wo.shape),
                  _const_spec(gf.shape), _const_spec(wg.shape), _const_spec(wu.shape),
                  _const_spec(wd.shape)],
        out_specs=tile_f32,
        out_shape=jax.ShapeDtypeStruct((rows, d), F32),
        compiler_params=pltpu.CompilerParams(
            dimension_semantics=("arbitrary",), vmem_limit_bytes=VMEM_LIMIT_BYTES),
        name="attn_out_ffn",
    )(h.reshape(rows, d), o.reshape(rows, d), wo, gf, wg, wu, wd)
    return out.reshape(b, lh, d)


def _conv_ffn_kernel(h_ref, gm_ref, win_ref, bin_ref, dww_ref, dwb_ref, lng_ref, lnb_ref,
                     wout_ref, bout_ref, gf_ref, wg_ref, wu_ref, wd_ref, o_ref, uext_ref, conv_ref):
    i = pl.program_id(1)
    h = h_ref[0]
    rows, d = h.shape
    pos = _positions(i, rows)
    y = _rms(h, gm_ref[...], RMS_EPS).astype(BF16)
    ag = jnp.dot(y, win_ref[...], preferred_element_type=F32) + bin_ref[...]
    u = ag[:, :d] * jax.nn.sigmoid(ag[:, d:])
    u = jnp.where(pos >= 0, u, 0.0)

    @pl.when(i == 0)
    def _():
        uext_ref[0:CONV_HALO, :] = jnp.zeros((CONV_HALO, d), F32)
        uext_ref[CONV_HALO + rows:, :] = jnp.zeros((SUBLANES, d), F32)

    uext_ref[CONV_HALO:CONV_HALO + rows, :] = u

    base = CONV_HALO - (CONV_WIDTH - 1)
    span = CONV_ROWS + SUBLANES
    n_oct = (base + CONV_WIDTH - 1) // SUBLANES + 1

    def conv_body(c, carry):
        r0 = pl.multiple_of(c * CONV_ROWS, CONV_ROWS)
        for lb in range(d // LANES):
            cs = slice(lb * LANES, (lb + 1) * LANES)
            um = [uext_ref[pl.ds(r0 + SUBLANES * m, span), cs] for m in range(n_oct)]
            out = None
            for sh in range(SUBLANES):
                part = None
                for m in range(n_oct):
                    j = SUBLANES * m + sh - base
                    if 0 <= j < CONV_WIDTH:
                        term = dww_ref[j:j + 1, cs] * um[m]
                        part = term if part is None else part + term
                if sh:
                    part = pltpu.roll(part, span - sh, axis=0)
                out = part if out is None else out + part
            conv_ref[pl.ds(r0, CONV_ROWS), cs] = out[:CONV_ROWS] + dwb_ref[:, cs]
        return carry

    lax.fori_loop(0, rows // CONV_ROWS, conv_body, 0)
    uext_ref[0:CONV_HALO, :] = uext_ref[rows:rows + CONV_HALO, :]

    c = conv_ref[...]
    mu = jnp.mean(c, axis=-1, keepdims=True)
    cc = c - mu
    var = jnp.mean(cc * cc, axis=-1, keepdims=True)
    z = (cc * lax.rsqrt(var + LN_EPS)) * lng_ref[...] + lnb_ref[...]
    z = (z * jax.nn.sigmoid(z)).astype(BF16)
    hmid = h + jnp.dot(z, wout_ref[...], preferred_element_type=F32) + bout_ref[...]
    o_ref[0] = _ffn(hmid, gf_ref, wg_ref, wu_ref, wd_ref)


def _conv_ffn(h, gm, win, bin_, dww, dwb, lng, lnb, wout, bout, gf, wg, wu, wd):
    b, lh, d = h.shape
    nt = lh // T_SEQ
    tile = pl.BlockSpec((1, T_SEQ, d), lambda bi, i: (bi, i, 0))
    consts = [gm, win, bin_, dww, dwb, lng, lnb, wout, bout, gf, wg, wu, wd]
    return pl.pallas_call(
        _conv_ffn_kernel,
        grid=(b, nt),
        in_specs=[tile] + [_const_spec(c.shape) for c in consts],
        out_specs=tile,
        out_shape=jax.ShapeDtypeStruct((b, lh, d), F32),
        scratch_shapes=[pltpu.VMEM((T_SEQ + CONV_HALO + SUBLANES, d), F32), pltpu.VMEM((T_SEQ, d), F32)],
        compiler_params=pltpu.CompilerParams(
            dimension_semantics=("arbitrary", "arbitrary"), vmem_limit_bytes=VMEM_LIMIT_BYTES),
        name="conv_ffn",
    )(h, *consts)


def kernel(x, meta_tokens, norm_mix_g, norm_ffn_g, final_norm_g, rel_bias_table, pool_w, pool_b, pool_scale, attn_w_qkv, attn_w_o, attn_lambda_q1, attn_lambda_k1, attn_lambda_q2, attn_lambda_k2, attn_subln_g, conv_w_in, conv_b_in, conv_dw_w, conv_dw_b, conv_ln_g, conv_ln_b, conv_w_out, conv_b_out, ffn_w_gate, ffn_w_up, ffn_w_down):
    b, seq, d = x.shape
    depth = norm_mix_g.shape[0]
    assert seq % T_SEQ == 0 and d % (LANES * len(POOL_WINDOWS)) == 0
    assert depth % N_MIXERS == 1, "the last layer must be a pooling layer (it applies the final norm)"

    row = lambda v: v.reshape(1, -1).astype(F32)
    meta = jnp.broadcast_to(meta_tokens[None].astype(F32), (b, N_META, d))
    h = jnp.concatenate([jnp.zeros((b, T_SEQ - N_META, d), F32), meta, x], axis=1)

    bias = None
    for i in range(depth):
        mixer, j = i % N_MIXERS, i // N_MIXERS
        gm, gf = row(norm_mix_g[i]), row(norm_ffn_g[i])
        wg, wu, wd = (w[i].astype(BF16) for w in (ffn_w_gate, ffn_w_up, ffn_w_down))
        if mixer == 0:
            final_g = row(final_norm_g) if i == depth - 1 else None
            h = _pool_ffn(h, gm, pool_w[j].astype(BF16), row(pool_b[j]), row(pool_scale[j]),
                          gf, wg, wu, wd, final_g)
        elif mixer == 1:
            lam_init = 0.8 - 0.6 * math.exp(-0.3 * i)
            hd = d // DA_HEADS
            if bias is None:
                bias = _bias_tiles(rel_bias_table.astype(F32), _bias_buckets())
            qscale = (hd // 2) ** -0.5 * LOG2E
            q, k, vt = _qkv(h, gm, attn_w_qkv[j].astype(BF16), qscale)
            lamv = jnp.stack([attn_lambda_q1[j], attn_lambda_k1[j],
                              attn_lambda_q2[j], attn_lambda_k2[j]]).astype(F32)
            o = _attention(q, k, vt, bias, lamv, attn_subln_g[j].reshape(hd, 1).astype(F32), lam_init)
            h = _attn_out_ffn(h, o, attn_w_o[j].astype(BF16), gf, wg, wu, wd)
        else:
            h = _conv_ffn(h, gm, conv_w_in[j].astype(BF16), row(conv_b_in[j]),
                          conv_dw_w[j].astype(F32), row(conv_dw_b[j]), row(conv_ln_g[j]),
                          row(conv_ln_b[j]), conv_w_out[j].astype(BF16), row(conv_b_out[j]),
                          gf, wg, wu, wd)
    return h
```

```python
import functools
import math

import jax
import jax.numpy as jnp
from jax import lax
from jax.experimental import pallas as pl
from jax.experimental.pallas import tpu as pltpu

F32 = jnp.float32
BF16 = jnp.bfloat16

N_META = 16
CHUNK = 64
POOL_WINDOWS = (2, 4, 8, 16)
DA_HEADS = 8
REL_BUCKETS = 32
REL_MAX_DIST = 128
CONV_WIDTH = 31
RMS_EPS = 1e-6
LN_EPS = 1e-5
SUBLN_EPS = 1e-5
N_MIXERS = 3

LANES = 128
SUBLANES = 8
MXU_DIM = 256
VMEM_LIMIT_BYTES = 56 * 1024 * 1024

T_SEQ = 512
META_BLK = 128
ONES_ROWS = 16
COL_CHUNK = 2 * MXU_DIM
POOL_HALO = 16
CONV_HALO = 32
CONV_ROWS = 32
NEG = -1e30
LOG2E = math.log2(math.e)
FAR_BUCKET = REL_BUCKETS // 2 - 1


def _ffn_chunks(f):
    step = 3 * MXU_DIM
    return [(s, min(s + step, f)) for s in range(0, f, step)]


def _rms(x, g, eps):
    ms = jnp.mean(x * x, axis=-1, keepdims=True)
    return (x * lax.rsqrt(ms + eps)) * g


def _ffn(hmid, g_ref, wg_ref, wu_ref, wd_ref):
    x = _rms(hmid, g_ref[...], RMS_EPS).astype(BF16)
    acc = None
    for f0, f1 in _ffn_chunks(wg_ref.shape[1]):
        gate = jnp.dot(x, wg_ref[:, f0:f1], preferred_element_type=F32)
        up = jnp.dot(x, wu_ref[:, f0:f1], preferred_element_type=F32)
        act = ((gate * jax.nn.sigmoid(gate)) * up).astype(BF16)
        part = jnp.dot(act, wd_ref[f0:f1, :], preferred_element_type=F32)
        acc = part if acc is None else acc + part
    return hmid + acc


def _positions(tile_idx, rows):
    r = lax.broadcasted_iota(jnp.int32, (rows, 1), 0)
    return tile_idx * rows + r - (T_SEQ - N_META)


def _pool_ffn_kernel(h_ref, gm_ref, pw_ref, pb_ref, ps_ref, gf_ref, wg_ref, wu_ref, wd_ref, *rest,
                     first, final):
    rest = list(rest)
    meta_ref = rest.pop(0) if first else None
    fg_ref = rest.pop(0) if final else None
    o_ref, halo_ref = rest
    i = pl.program_id(1)
    h = h_ref[0]
    rows, d = h.shape
    if first:
        lead = jnp.concatenate([jnp.zeros((rows - N_META, d), F32), meta_ref[...]], axis=0)
        h = jnp.where(i == 0, lead, h)
    pos = _positions(i, rows)
    y = jnp.where(pos >= 0, _rms(h, gm_ref[...], RMS_EPS), 0.0)

    @pl.when(i == 0)
    def _():
        halo_ref[...] = jnp.zeros_like(halo_ref)

    ext = jnp.concatenate([halo_ref[...], y], axis=0)
    halo_ref[...] = y[rows - POOL_HALO:, :]

    group = d // len(POOL_WINDOWS)
    mixed = []
    for gi, win in enumerate(POOL_WINDOWS):
        c0 = gi * group
        s = ext[:, c0:c0 + group]
        shift = 1
        while shift < win:
            s = s + pltpu.roll(s, shift, axis=0)
            shift *= 2
        cnt = jnp.clip(pos + 1, 1, win).astype(F32)
        pooled = s[POOL_HALO:, :] / cnt - y[:, c0:c0 + group]
        mixed.append(jnp.dot(pooled.astype(BF16), pw_ref[gi], preferred_element_type=F32))
    mixed = jnp.concatenate(mixed, axis=1)
    hmid = h + (mixed + pb_ref[...]) * ps_ref[...]
    out = _ffn(hmid, gf_ref, wg_ref, wu_ref, wd_ref)
    if final:
        out = _rms(out, fg_ref[...], RMS_EPS)
    o_ref[0] = out


def _const_spec(shape):
    nd = len(shape)
    return pl.BlockSpec(shape, lambda *_: (0,) * nd, pipeline_mode=pl.Buffered(1))


def _pool_ffn(h, gm, pw, pb, ps, gf, wg, wu, wd, meta=None, final_g=None):
    first, final = meta is not None, final_g is not None
    b, _, d = h.shape
    lh = h.shape[1] + T_SEQ if first else h.shape[1]
    nt = lh // T_SEQ
    tile = pl.BlockSpec((1, T_SEQ, d), lambda bi, i: (bi, i, 0))
    frames = pl.BlockSpec((1, T_SEQ, d), lambda bi, i: (bi, jnp.maximum(i - 1, 0), 0))
    in_specs = [frames if first else tile, _const_spec(gm.shape), _const_spec(pw.shape),
                _const_spec(pb.shape), _const_spec(ps.shape), _const_spec(gf.shape),
                _const_spec(wg.shape), _const_spec(wu.shape), _const_spec(wd.shape)]
    args = [h, gm, pw, pb, ps, gf, wg, wu, wd]
    if first:
        in_specs.append(_const_spec(meta.shape))
        args.append(meta)
    if final:
        in_specs.append(_const_spec(final_g.shape))
        args.append(final_g)
        out_spec = frames
        out_shape = jax.ShapeDtypeStruct((b, lh - T_SEQ, d), F32)
    else:
        out_spec = tile
        out_shape = jax.ShapeDtypeStruct((b, lh, d), F32)
    return pl.pallas_call(
        functools.partial(_pool_ffn_kernel, first=first, final=final),
        grid=(b, nt),
        in_specs=in_specs,
        out_specs=out_spec,
        out_shape=out_shape,
        scratch_shapes=[pltpu.VMEM((POOL_HALO, d), F32)],
        compiler_params=pltpu.CompilerParams(
            dimension_semantics=("arbitrary", "arbitrary"), vmem_limit_bytes=VMEM_LIMIT_BYTES),
        name="pool_ffn_final" if final else "pool_ffn",
    )(*args)


def _qkv_kernel(h_ref, g_ref, w_ref, q_ref, k_ref, vt_ref, *, qscale):
    h = h_ref[0]
    d = h.shape[1]
    y = _rms(h, g_ref[...], RMS_EPS).astype(BF16)
    q = jnp.dot(y, w_ref[:, 0:d], preferred_element_type=F32) * qscale
    k = jnp.dot(y, w_ref[:, d:2 * d], preferred_element_type=F32)
    v = jnp.dot(y, w_ref[:, 2 * d:3 * d], preferred_element_type=F32)
    q_ref[0] = q.astype(BF16)
    k_ref[0] = k.astype(BF16)
    vt = v.T.astype(BF16)
    heads, hv, rows = vt_ref.shape[2:]
    hd = d // heads
    for hh in range(heads):
        vt_ref[0, 0, hh, 0:hd, :] = vt[hh * hd:(hh + 1) * hd, :]
        vt_ref[0, 0, hh, hd:hv, :] = jnp.ones((hv - hd, rows), BF16)


def _qkv(h, g, w, qscale):
    b, lh, d = h.shape
    nt = lh // T_SEQ
    hv = d // DA_HEADS + ONES_ROWS
    tile = pl.BlockSpec((1, T_SEQ, d), lambda bi, i: (bi, i, 0))
    return pl.pallas_call(
        functools.partial(_qkv_kernel, qscale=qscale),
        grid=(b, nt),
        in_specs=[tile, _const_spec(g.shape), _const_spec(w.shape)],
        out_specs=[tile, tile,
                   pl.BlockSpec((1, 1, DA_HEADS, hv, T_SEQ), lambda bi, i: (bi, i, 0, 0, 0))],
        out_shape=[jax.ShapeDtypeStruct((b, lh, d), BF16), jax.ShapeDtypeStruct((b, lh, d), BF16),
                   jax.ShapeDtypeStruct((b, nt, DA_HEADS, hv, T_SEQ), BF16)],
        compiler_params=pltpu.CompilerParams(
            dimension_semantics=("arbitrary", "arbitrary"), vmem_limit_bytes=VMEM_LIMIT_BYTES),
        name="qkv",
    )(h, g, w)


def _t5_bucket(rel):
    half = REL_BUCKETS // 2
    max_exact = half // 2
    offset = jnp.where(rel > 0, half, 0)
    n = jnp.abs(rel)
    log_ratio = jnp.log(jnp.maximum(n, 1).astype(F32) / max_exact) / math.log(REL_MAX_DIST / max_exact)
    large = jnp.minimum(max_exact + (log_ratio * (half - max_exact)).astype(jnp.int32), half - 1)
    return offset + jnp.where(n < max_exact, n, large)


def _bias_buckets():
    t = T_SEQ
    kl = lax.broadcasted_iota(jnp.int32, (t, t), 0)
    ql = lax.broadcasted_iota(jnp.int32, (t, t), 1)
    prev = _t5_bucket(kl - t - ql)[t - META_BLK:]
    diag = jnp.where(kl // CHUNK <= ql // CHUNK, _t5_bucket(kl - ql), -1)
    kr = lax.broadcasted_iota(jnp.int32, (META_BLK, t), 0)
    qc = lax.broadcasted_iota(jnp.int32, (META_BLK, t), 1)
    mk = kr - (META_BLK - N_META)
    mq = qc - (t - N_META)
    lead = jnp.where(mq >= 0, _t5_bucket(mk - mq), FAR_BUCKET)
    first = _t5_bucket(mk - N_META - qc)
    later = jnp.full((META_BLK, t), FAR_BUCKET, jnp.int32)
    meta = [jnp.where(mk >= 0, m, -1) for m in (lead, first, later)]
    nb = t // META_BLK
    return jnp.concatenate([prev[None], diag.reshape(nb, META_BLK, t), jnp.stack(meta)], axis=0)


def _bias_kernel(tab_ref, bk_ref, o_ref):
    hh = pl.program_id(0)
    far = tab_ref[FAR_BUCKET, hh]
    for blk in range(bk_ref.shape[0]):
        b = bk_ref[blk]
        acc = jnp.zeros(b.shape, F32)
        for t in range(REL_BUCKETS):
            acc = jnp.where(b == t, tab_ref[t, hh] - far, acc)
        o_ref[0, blk] = jnp.where(b < 0, NEG, acc * LOG2E)


def _bias_tiles(table, buckets):
    nb, r, c = buckets.shape
    heads = table.shape[1]
    return pl.pallas_call(
        _bias_kernel,
        grid=(heads,),
        in_specs=[pl.BlockSpec(memory_space=pltpu.SMEM),
                  pl.BlockSpec((nb, r, c), lambda hh: (0, 0, 0))],
        out_specs=pl.BlockSpec((1, nb, r, c), lambda hh: (hh, 0, 0, 0)),
        out_shape=jax.ShapeDtypeStruct((heads, nb, r, c), F32),
        compiler_params=pltpu.CompilerParams(dimension_semantics=("arbitrary",)),
        name="rel_bias",
    )(table, buckets)


def _attn_kernel(q_ref, k_ref, vt_ref, bnear_ref, bmeta_ref, lam_ref, sg_ref, o_ref,
                 qzz_ref, m_ref, mt_ref, acc_ref, sa_ref, sb_ref, *, lam_init):
    a = pl.program_id(2)
    t = q_ref.shape[1]
    hd = q_ref.shape[2]

    qt = q_ref[0].astype(F32).T
    row = lax.broadcasted_iota(jnp.int32, (hd, t), 0)
    q1 = jnp.where(row < hd // 2, qt, 0.0)
    q2 = jnp.where(row >= hd // 2, qt, 0.0)
    qzz_ref[...] = jnp.concatenate([q1, q2], axis=1).astype(BF16)

    def produce(j, buf):
        kt = k_ref[0, pl.ds(pl.multiple_of(j * t, t), t), :]
        mts = []
        for cs in col_chunks:
            s = jnp.dot(kt, qzz_ref[:, cs], preferred_element_type=F32)
            buf[:, cs] = s
            mts.append(jnp.max(s, axis=0, keepdims=True))
        return jnp.concatenate(mts, axis=1)

    def probs(s, m):
        return jnp.exp2(s - m).astype(BF16)

    def consume(buf, mt, vt):
        for cs in col_chunks:
            m_old = m_ref[:, cs]
            m_new = jnp.maximum(m_old, mt[:, cs])
            alpha = jnp.exp2(m_old - m_new)
            acc_ref[:, cs] = alpha * acc_ref[:, cs] + jnp.dot(
                vt, probs(buf[:, cs], m_new), preferred_element_type=F32)
            m_ref[:, cs] = m_new

    def consume_diag(buf, j):
        bias = bnear_ref[0, 1:nb + 1].reshape(t, t)
        mts = []
        for cs in col_chunks:
            bc = slice(cs.start % t, cs.start % t + COL_CHUNK)
            s = buf[:, cs] + bias[:, bc]
            buf[:, cs] = s
            mts.append(jnp.max(s, axis=0, keepdims=True))
        consume(buf, jnp.concatenate(mts, axis=1), vt_ref[0, j, 0])

    def fix_prev(buf, mt):
        corner = bnear_ref[0, 0, :, 0:LANES]
        pieces, start = [], 0
        for base in (0, t):
            cs = slice(base, base + LANES)
            buf[t - META_BLK:, cs] = buf[t - META_BLK:, cs] + corner
            if base > start:
                pieces.append(mt[:, start:base])
            pieces.append(jnp.max(buf[:, cs], axis=0, keepdims=True))
            start = base + LANES
        return jnp.concatenate(pieces + [mt[:, start:]], axis=1)

    def init():
        s0 = jnp.dot(k_ref[0, t - META_BLK:t, :], qzz_ref[...], preferred_element_type=F32)
        s0 = s0 + jnp.concatenate([bmeta_ref[0, 0], bmeta_ref[0, 0]], axis=1)
        m0 = jnp.max(s0, axis=0, keepdims=True)
        m_ref[...] = m0
        acc_ref[...] = jnp.dot(vt_ref[0, 0, 0, :, t - META_BLK:t], probs(s0, m0),
                               preferred_element_type=F32)

    col_chunks = [slice(c, c + COL_CHUNK) for c in range(0, 2 * t, COL_CHUNK)]
    nb = t // META_BLK
    odd = lax.rem(a, 2)

    @pl.when(a == 0)
    def _():
        init()

    @pl.when(a == 1)
    def _():
        init()
        produce(1, sb_ref)
        consume_diag(sb_ref, 1)

    @pl.when((a >= 2) & (odd == 1))
    def _():
        init()
        mtb = produce(1, sb_ref)
        mt_ref[...] = produce(2, sa_ref)
        consume(sb_ref, mtb, vt_ref[0, 1, 0])

    @pl.when((a >= 2) & (odd == 0))
    def _():
        init()
        mt_ref[...] = produce(1, sa_ref)

    @pl.when(a >= 2)
    def _():
        j0 = 1 + odd

        def pair_body(i, mta):
            j = j0 + 2 * i
            mtb = produce(j + 1, sb_ref)
            consume(sa_ref, mta, vt_ref[0, j, 0])
            mta_next = produce(j + 2, sa_ref)
            consume(sb_ref, mtb, vt_ref[0, j + 1, 0])
            return mta_next

        mta = lax.fori_loop(0, (a - 2) // 2, pair_body, mt_ref[...])
        produce(a, sb_ref)
        consume(sa_ref, fix_prev(sa_ref, mta), vt_ref[0, a - 1, 0])
        consume_diag(sb_ref, a)

    lv = lam_ref[...]
    lam = (jnp.exp(jnp.sum(lv[0:1] * lv[1:2], axis=1, keepdims=True))
           - jnp.exp(jnp.sum(lv[2:3] * lv[3:4], axis=1, keepdims=True)) + lam_init)
    on = acc_ref[0:hd, :] / acc_ref[hd:hd + 1, :]
    o = on[:, :t] - lam * on[:, t:]
    o = (o * lax.rsqrt(jnp.mean(o * o, axis=0, keepdims=True) + SUBLN_EPS)) * sg_ref[...]
    o = o * (1.0 - lam_init)
    o_ref[0] = o.T.astype(BF16)


def _attention(q, k, vt, bias, lamv, sg, lam_init):
    b, lh, d = q.shape
    nt = lh // T_SEQ
    hd = d // DA_HEADS
    nb = T_SEQ // META_BLK
    hv = vt.shape[3]
    qtile = pl.BlockSpec((1, T_SEQ, hd), lambda bi, hh, a: (bi, a, hh))
    return pl.pallas_call(
        functools.partial(_attn_kernel, lam_init=lam_init),
        grid=(b, DA_HEADS, nt),
        in_specs=[
            qtile,
            pl.BlockSpec((1, lh, hd), lambda bi, hh, a: (bi, 0, hh)),
            pl.BlockSpec((1, nt, 1, hv, T_SEQ), lambda bi, hh, a: (bi, 0, hh, 0, 0)),
            pl.BlockSpec((1, nb + 1, META_BLK, T_SEQ), lambda bi, hh, a: (hh, 0, 0, 0)),
            pl.BlockSpec((1, 1, META_BLK, T_SEQ),
                         lambda bi, hh, a: (hh, nb + 1 + jnp.minimum(a, 2), 0, 0)),
            pl.BlockSpec(lamv.shape, lambda bi, hh, a: (0, 0)),
            pl.BlockSpec(sg.shape, lambda bi, hh, a: (0, 0)),
        ],
        out_specs=qtile,
        out_shape=jax.ShapeDtypeStruct((b, lh, d), BF16),
        scratch_shapes=[pltpu.VMEM((hd, 2 * T_SEQ), BF16), pltpu.VMEM((1, 2 * T_SEQ), F32),
                        pltpu.VMEM((1, 2 * T_SEQ), F32), pltpu.VMEM((hv, 2 * T_SEQ), F32),
                        pltpu.VMEM((T_SEQ, 2 * T_SEQ), F32), pltpu.VMEM((T_SEQ, 2 * T_SEQ), F32)],
        compiler_params=pltpu.CompilerParams(
            dimension_semantics=("arbitrary", "arbitrary", "arbitrary"),
            vmem_limit_bytes=VMEM_LIMIT_BYTES),
        name="diff_attn",
    )(q, k, vt, bias, bias, lamv, sg)


def _attn_out_ffn_kernel(h_ref, o_ref, wo_ref, gf_ref, wg_ref, wu_ref, wd_ref, out_ref):
    hmid = h_ref[...] + jnp.dot(o_ref[...], wo_ref[...], preferred_element_type=F32)
    out_ref[...] = _ffn(hmid, gf_ref, wg_ref, wu_ref, wd_ref)


def _attn_out_ffn(h, o, wo, gf, wg, wu, wd):
    b, lh, d = h.shape
    rows = b * lh
    tile_f32 = pl.BlockSpec((T_SEQ, d), lambda i: (i, 0))
    out = pl.pallas_call(
        _attn_out_ffn_kernel,
        grid=(rows // T_SEQ,),
        in_specs=[tile_f32, pl.BlockSpec((T_SEQ, d), lambda i: (i, 0)), _const_spec(wo.shape),
                  _const_spec(gf.shape), _const_spec(wg.shape), _const_spec(wu.shape),
                  _const_spec(wd.shape)],
        out_specs=tile_f32,
        out_shape=jax.ShapeDtypeStruct((rows, d), F32),
        compiler_params=pltpu.CompilerParams(
            dimension_semantics=("arbitrary",), vmem_limit_bytes=VMEM_LIMIT_BYTES),
        name="attn_out_ffn",
    )(h.reshape(rows, d), o.reshape(rows, d), wo, gf, wg, wu, wd)
    return out.reshape(b, lh, d)


def _conv_ffn_kernel(h_ref, gm_ref, win_ref, bin_ref, dww_ref, dwb_ref, lng_ref, lnb_ref,
                     wout_ref, bout_ref, gf_ref, wg_ref, wu_ref, wd_ref, o_ref, uext_ref, conv_ref):
    i = pl.program_id(1)
    h = h_ref[0]
    rows, d = h.shape
    pos = _positions(i, rows)
    y = _rms(h, gm_ref[...], RMS_EPS).astype(BF16)
    ag = jnp.dot(y, win_ref[...], preferred_element_type=F32) + bin_ref[...]
    u = ag[:, :d] * jax.nn.sigmoid(ag[:, d:])
    u = jnp.where(pos >= 0, u, 0.0)

    @pl.when(i == 0)
    def _():
        uext_ref[0:CONV_HALO, :] = jnp.zeros((CONV_HALO, d), F32)
        uext_ref[CONV_HALO + rows:, :] = jnp.zeros((SUBLANES, d), F32)

    uext_ref[CONV_HALO:CONV_HALO + rows, :] = u

    base = CONV_HALO - (CONV_WIDTH - 1)
    span = CONV_ROWS + SUBLANES
    n_oct = (base + CONV_WIDTH - 1) // SUBLANES + 1

    def conv_body(c, carry):
        r0 = pl.multiple_of(c * CONV_ROWS, CONV_ROWS)
        for lb in range(d // LANES):
            cs = slice(lb * LANES, (lb + 1) * LANES)
            um = [uext_ref[pl.ds(r0 + SUBLANES * m, span), cs] for m in range(n_oct)]
            out = None
            for sh in range(SUBLANES):
                part = None
                for m in range(n_oct):
                    j = SUBLANES * m + sh - base
                    if 0 <= j < CONV_WIDTH:
                        term = dww_ref[j:j + 1, cs] * um[m]
                        part = term if part is None else part + term
                if sh:
                    part = pltpu.roll(part, span - sh, axis=0)
                out = part if out is None else out + part
            conv_ref[pl.ds(r0, CONV_ROWS), cs] = out[:CONV_ROWS] + dwb_ref[:, cs]
        return carry

    lax.fori_loop(0, rows // CONV_ROWS, conv_body, 0)
    uext_ref[0:CONV_HALO, :] = uext_ref[rows:rows + CONV_HALO, :]

    c = conv_ref[...]
    mu = jnp.mean(c, axis=-1, keepdims=True)
    cc = c - mu
    var = jnp.mean(cc * cc, axis=-1, keepdims=True)
    z = (cc * lax.rsqrt(var + LN_EPS)) * lng_ref[...] + lnb_ref[...]
    z = (z * jax.nn.sigmoid(z)).astype(BF16)
    hmid = h + jnp.dot(z, wout_ref[...], preferred_element_type=F32) + bout_ref[...]
    o_ref[0] = _ffn(hmid, gf_ref, wg_ref, wu_ref, wd_ref)


def _conv_ffn(h, gm, win, bin_, dww, dwb, lng, lnb, wout, bout, gf, wg, wu, wd):
    b, lh, d = h.shape
    nt = lh // T_SEQ
    tile = pl.BlockSpec((1, T_SEQ, d), lambda bi, i: (bi, i, 0))
    consts = [gm, win, bin_, dww, dwb, lng, lnb, wout, bout, gf, wg, wu, wd]
    return pl.pallas_call(
        _conv_ffn_kernel,
        grid=(b, nt),
        in_specs=[tile] + [_const_spec(c.shape) for c in consts],
        out_specs=tile,
        out_shape=jax.ShapeDtypeStruct((b, lh, d), F32),
        scratch_shapes=[pltpu.VMEM((T_SEQ + CONV_HALO + SUBLANES, d), F32), pltpu.VMEM((T_SEQ, d), F32)],
        compiler_params=pltpu.CompilerParams(
            dimension_semantics=("arbitrary", "arbitrary"), vmem_limit_bytes=VMEM_LIMIT_BYTES),
        name="conv_ffn",
    )(h, *consts)


def kernel(x, meta_tokens, norm_mix_g, norm_ffn_g, final_norm_g, rel_bias_table, pool_w, pool_b, pool_scale, attn_w_qkv, attn_w_o, attn_lambda_q1, attn_lambda_k1, attn_lambda_q2, attn_lambda_k2, attn_subln_g, conv_w_in, conv_b_in, conv_dw_w, conv_dw_b, conv_ln_g, conv_ln_b, conv_w_out, conv_b_out, ffn_w_gate, ffn_w_up, ffn_w_down):
    b, seq, d = x.shape
    depth = norm_mix_g.shape[0]
    assert seq % T_SEQ == 0 and d % (LANES * len(POOL_WINDOWS)) == 0
    assert depth % N_MIXERS == 1, "the last layer must be a pooling layer (it applies the final norm)"

    row = lambda v: v.reshape(1, -1).astype(F32)
    h = x.astype(F32)
    bias = None
    for i in range(depth):
        mixer, j = i % N_MIXERS, i // N_MIXERS
        gm, gf = row(norm_mix_g[i]), row(norm_ffn_g[i])
        wg, wu, wd = (w[i].astype(BF16) for w in (ffn_w_gate, ffn_w_up, ffn_w_down))
        if mixer == 0:
            h = _pool_ffn(h, gm, pool_w[j].astype(BF16), row(pool_b[j]), row(pool_scale[j]),
                          gf, wg, wu, wd,
                          meta=meta_tokens.astype(F32) if i == 0 else None,
                          final_g=row(final_norm_g) if i == depth - 1 else None)
        elif mixer == 1:
            lam_init = 0.8 - 0.6 * math.exp(-0.3 * i)
            hd = d // DA_HEADS
            if bias is None:
                bias = _bias_tiles(rel_bias_table.astype(F32), _bias_buckets())
            qscale = (hd // 2) ** -0.5 * LOG2E
            q, k, vt = _qkv(h, gm, attn_w_qkv[j].astype(BF16), qscale)
            lamv = jnp.stack([attn_lambda_q1[j], attn_lambda_k1[j],
                              attn_lambda_q2[j], attn_lambda_k2[j]]).astype(F32)
            o = _attention(q, k, vt, bias, lamv, attn_subln_g[j].reshape(hd, 1).astype(F32), lam_init)
            h = _attn_out_ffn(h, o, attn_w_o[j].astype(BF16), gf, wg, wu, wd)
        else:
            h = _conv_ffn(h, gm, conv_w_in[j].astype(BF16), row(conv_b_in[j]),
                          conv_dw_w[j].astype(F32), row(conv_dw_b[j]), row(conv_ln_g[j]),
                          row(conv_ln_b[j]), conv_w_out[j].astype(BF16), row(conv_b_out[j]),
                          gf, wg, wu, wd)
    return h
```

```python
import functools
import math

import jax
import jax.numpy as jnp
from jax import lax
from jax.experimental import pallas as pl
from jax.experimental.pallas import tpu as pltpu

F32 = jnp.float32
BF16 = jnp.bfloat16

N_META = 16
CHUNK = 64
POOL_WINDOWS = (2, 4, 8, 16)
DA_HEADS = 8
REL_BUCKETS = 32
REL_MAX_DIST = 128
CONV_WIDTH = 31
RMS_EPS = 1e-6
LN_EPS = 1e-5
SUBLN_EPS = 1e-5
N_MIXERS = 3

LANES = 128
SUBLANES = 8
MXU_DIM = 256
VMEM_LIMIT_BYTES = 56 * 1024 * 1024

T_SEQ = 512
META_BLK = 128
ONES_ROWS = 16
COL_CHUNK = 2 * MXU_DIM
POOL_HALO = 16
CONV_HALO = 32
CONV_ROWS = 32
NEG = -1e30
LOG2E = math.log2(math.e)
FAR_BUCKET = REL_BUCKETS // 2 - 1


def _ffn_chunks(f):
    step = 3 * MXU_DIM
    return [(s, min(s + step, f)) for s in range(0, f, step)]


def _rms(x, g, eps):
    ms = jnp.mean(x * x, axis=-1, keepdims=True)
    return (x * lax.rsqrt(ms + eps)) * g


def _ffn(hmid, g_ref, wg_ref, wu_ref, wd_ref):
    x = _rms(hmid, g_ref[...], RMS_EPS).astype(BF16)
    acc = None
    for f0, f1 in _ffn_chunks(wg_ref.shape[1]):
        gate = jnp.dot(x, wg_ref[:, f0:f1], preferred_element_type=F32)
        up = jnp.dot(x, wu_ref[:, f0:f1], preferred_element_type=F32)
        act = ((gate * jax.nn.sigmoid(gate)) * up).astype(BF16)
        part = jnp.dot(act, wd_ref[f0:f1, :], preferred_element_type=F32)
        acc = part if acc is None else acc + part
    return hmid + acc


def _positions(tile_idx, rows):
    r = lax.broadcasted_iota(jnp.int32, (rows, 1), 0)
    return tile_idx * rows + r - (T_SEQ - N_META)


def _pool_ffn_kernel(h_ref, gm_ref, pw_ref, pb_ref, ps_ref, gf_ref, wg_ref, wu_ref, wd_ref, *rest,
                     first, final):
    rest = list(rest)
    meta_ref = rest.pop(0) if first else None
    fg_ref = rest.pop(0) if final else None
    o_ref, halo_ref = rest
    i = pl.program_id(1)
    h = h_ref[0]
    rows, d = h.shape
    if first:
        lead = jnp.concatenate([jnp.zeros((rows - N_META, d), F32), meta_ref[...]], axis=0)
        h = jnp.where(i == 0, lead, h)
    pos = _positions(i, rows)
    y = jnp.where(pos >= 0, _rms(h, gm_ref[...], RMS_EPS), 0.0)

    @pl.when(i == 0)
    def _():
        halo_ref[...] = jnp.zeros_like(halo_ref)

    ext = jnp.concatenate([halo_ref[...], y], axis=0)
    halo_ref[...] = y[rows - POOL_HALO:, :]

    group = d // len(POOL_WINDOWS)
    mixed = []
    for gi, win in enumerate(POOL_WINDOWS):
        c0 = gi * group
        s = ext[:, c0:c0 + group]
        shift = 1
        while shift < win:
            s = s + pltpu.roll(s, shift, axis=0)
            shift *= 2
        cnt = jnp.clip(pos + 1, 1, win).astype(F32)
        pooled = s[POOL_HALO:, :] / cnt - y[:, c0:c0 + group]
        mixed.append(jnp.dot(pooled.astype(BF16), pw_ref[gi], preferred_element_type=F32))
    mixed = jnp.concatenate(mixed, axis=1)
    hmid = h + (mixed + pb_ref[...]) * ps_ref[...]
    out = _ffn(hmid, gf_ref, wg_ref, wu_ref, wd_ref)
    if final:
        out = _rms(out, fg_ref[...], RMS_EPS)
    o_ref[0] = out


def _const_spec(shape):
    nd = len(shape)
    return pl.BlockSpec(shape, lambda *_: (0,) * nd, pipeline_mode=pl.Buffered(1))


def _pool_ffn(h, gm, pw, pb, ps, gf, wg, wu, wd, meta=None, final_g=None):
    first, final = meta is not None, final_g is not None
    b, _, d = h.shape
    lh = h.shape[1] + T_SEQ if first else h.shape[1]
    nt = lh // T_SEQ
    tile = pl.BlockSpec((1, T_SEQ, d), lambda bi, i: (bi, i, 0))
    frames = pl.BlockSpec((1, T_SEQ, d), lambda bi, i: (bi, jnp.maximum(i - 1, 0), 0))
    in_specs = [frames if first else tile, _const_spec(gm.shape), _const_spec(pw.shape),
                _const_spec(pb.shape), _const_spec(ps.shape), _const_spec(gf.shape),
                _const_spec(wg.shape), _const_spec(wu.shape), _const_spec(wd.shape)]
    args = [h, gm, pw, pb, ps, gf, wg, wu, wd]
    if first:
        in_specs.append(_const_spec(meta.shape))
        args.append(meta)
    if final:
        in_specs.append(_const_spec(final_g.shape))
        args.append(final_g)
        out_spec = frames
        out_shape = jax.ShapeDtypeStruct((b, lh - T_SEQ, d), F32)
    else:
        out_spec = tile
        out_shape = jax.ShapeDtypeStruct((b, lh, d), F32)
    return pl.pallas_call(
        functools.partial(_pool_ffn_kernel, first=first, final=final),
        grid=(b, nt),
        in_specs=in_specs,
        out_specs=out_spec,
        out_shape=out_shape,
        scratch_shapes=[pltpu.VMEM((POOL_HALO, d), F32)],
        compiler_params=pltpu.CompilerParams(
            dimension_semantics=("arbitrary", "arbitrary"), vmem_limit_bytes=VMEM_LIMIT_BYTES),
        name="pool_ffn_final" if final else "pool_ffn",
    )(*args)


def _qkv_kernel(h_ref, g_ref, w_ref, q_ref, k_ref, vt_ref, *, qscale):
    h = h_ref[0]
    d = h.shape[1]
    y = _rms(h, g_ref[...], RMS_EPS).astype(BF16)
    q = jnp.dot(y, w_ref[:, 0:d], preferred_element_type=F32) * qscale
    k = jnp.dot(y, w_ref[:, d:2 * d], preferred_element_type=F32)
    v = jnp.dot(y, w_ref[:, 2 * d:3 * d], preferred_element_type=F32)
    q_ref[0] = q.astype(BF16)
    k_ref[0] = k.astype(BF16)
    vt = v.T.astype(BF16)
    heads, hv, rows = vt_ref.shape[2:]
    hd = d // heads
    for hh in range(heads):
        vt_ref[0, 0, hh, 0:hd, :] = vt[hh * hd:(hh + 1) * hd, :]
        vt_ref[0, 0, hh, hd:hv, :] = jnp.ones((hv - hd, rows), BF16)


def _qkv(h, g, w, qscale):
    b, lh, d = h.shape
    nt = lh // T_SEQ
    hv = d // DA_HEADS + ONES_ROWS
    tile = pl.BlockSpec((1, T_SEQ, d), lambda bi, i: (bi, i, 0))
    return pl.pallas_call(
        functools.partial(_qkv_kernel, qscale=qscale),
        grid=(b, nt),
        in_specs=[tile, _const_spec(g.shape), _const_spec(w.shape)],
        out_specs=[tile, tile,
                   pl.BlockSpec((1, 1, DA_HEADS, hv, T_SEQ), lambda bi, i: (bi, i, 0, 0, 0))],
        out_shape=[jax.ShapeDtypeStruct((b, lh, d), BF16), jax.ShapeDtypeStruct((b, lh, d), BF16),
                   jax.ShapeDtypeStruct((b, nt, DA_HEADS, hv, T_SEQ), BF16)],
        compiler_params=pltpu.CompilerParams(
            dimension_semantics=("arbitrary", "arbitrary"), vmem_limit_bytes=VMEM_LIMIT_BYTES),
        name="qkv",
    )(h, g, w)


def _t5_bucket(rel):
    half = REL_BUCKETS // 2
    max_exact = half // 2
    offset = jnp.where(rel > 0, half, 0)
    n = jnp.abs(rel)
    log_ratio = jnp.log(jnp.maximum(n, 1).astype(F32) / max_exact) / math.log(REL_MAX_DIST / max_exact)
    large = jnp.minimum(max_exact + (log_ratio * (half - max_exact)).astype(jnp.int32), half - 1)
    return offset + jnp.where(n < max_exact, n, large)


def _bias_buckets():
    t = T_SEQ
    kl = lax.broadcasted_iota(jnp.int32, (t, t), 0)
    ql = lax.broadcasted_iota(jnp.int32, (t, t), 1)
    prev = _t5_bucket(kl - t - ql)[t - META_BLK:]
    diag = jnp.where(kl // CHUNK <= ql // CHUNK, _t5_bucket(kl - ql), -1)
    kr = lax.broadcasted_iota(jnp.int32, (META_BLK, t), 0)
    qc = lax.broadcasted_iota(jnp.int32, (META_BLK, t), 1)
    mk = kr - (META_BLK - N_META)
    mq = qc - (t - N_META)
    lead = jnp.where(mq >= 0, _t5_bucket(mk - mq), FAR_BUCKET)
    first = _t5_bucket(mk - N_META - qc)
    later = jnp.full((META_BLK, t), FAR_BUCKET, jnp.int32)
    meta = [jnp.where(mk >= 0, m, -1) for m in (lead, first, later)]
    nb = t // META_BLK
    return jnp.concatenate([prev[None], diag.reshape(nb, META_BLK, t), jnp.stack(meta)], axis=0)


def _bias_kernel(tab_ref, bk_ref, o_ref):
    hh = pl.program_id(0)
    far = tab_ref[FAR_BUCKET, hh]
    for blk in range(bk_ref.shape[0]):
        b = bk_ref[blk]
        acc = jnp.zeros(b.shape, F32)
        for t in range(REL_BUCKETS):
            acc = jnp.where(b == t, tab_ref[t, hh] - far, acc)
        o_ref[0, blk] = jnp.where(b < 0, NEG, acc * LOG2E)


def _bias_tiles(table, buckets):
    nb, r, c = buckets.shape
    heads = table.shape[1]
    return pl.pallas_call(
        _bias_kernel,
        grid=(heads,),
        in_specs=[pl.BlockSpec(memory_space=pltpu.SMEM),
                  pl.BlockSpec((nb, r, c), lambda hh: (0, 0, 0))],
        out_specs=pl.BlockSpec((1, nb, r, c), lambda hh: (hh, 0, 0, 0)),
        out_shape=jax.ShapeDtypeStruct((heads, nb, r, c), F32),
        compiler_params=pltpu.CompilerParams(dimension_semantics=("arbitrary",)),
        name="rel_bias",
    )(table, buckets)


def _attn_kernel(q_ref, k_ref, vt_ref, bnear_ref, bmeta_ref, lam_ref, sg_ref, o_ref,
                 qzz_ref, m_ref, mt_ref, acc_ref, sa_ref, sb_ref, *, lam_init):
    a = pl.program_id(2)
    t = q_ref.shape[1]
    hd = q_ref.shape[2]

    qt = q_ref[0].astype(F32).T
    row = lax.broadcasted_iota(jnp.int32, (hd, t), 0)
    q1 = jnp.where(row < hd // 2, qt, 0.0)
    q2 = jnp.where(row >= hd // 2, qt, 0.0)
    qzz_ref[...] = jnp.concatenate([q1, q2], axis=1).astype(BF16)

    def produce(j, buf):
        kt = k_ref[0, pl.ds(pl.multiple_of(j * t, t), t), :]
        mts = []
        for cs in col_chunks:
            s = jnp.dot(kt, qzz_ref[:, cs], preferred_element_type=F32)
            buf[:, cs] = s
            mts.append(jnp.max(s, axis=0, keepdims=True))
        return jnp.concatenate(mts, axis=1)

    def probs(s, m):
        return jnp.exp2(s - m).astype(BF16)

    def consume(buf, mt, vt):
        for cs in col_chunks:
            m_old = m_ref[:, cs]
            m_new = jnp.maximum(m_old, mt[:, cs])
            alpha = jnp.exp2(m_old - m_new)
            acc_ref[:, cs] = alpha * acc_ref[:, cs] + jnp.dot(
                vt, probs(buf[:, cs], m_new), preferred_element_type=F32)
            m_ref[:, cs] = m_new

    def consume_diag(buf, j):
        bias = bnear_ref[0, 1:nb + 1].reshape(t, t)
        mts = []
        for cs in col_chunks:
            bc = slice(cs.start % t, cs.start % t + COL_CHUNK)
            s = buf[:, cs] + bias[:, bc]
            buf[:, cs] = s
            mts.append(jnp.max(s, axis=0, keepdims=True))
        consume(buf, jnp.concatenate(mts, axis=1), vt_ref[0, j, 0])

    def fix_prev(buf, mt):
        corner = bnear_ref[0, 0, :, 0:LANES]
        pieces, start = [], 0
        for base in (0, t):
            cs = slice(base, base + LANES)
            buf[t - META_BLK:, cs] = buf[t - META_BLK:, cs] + corner
            if base > start:
                pieces.append(mt[:, start:base])
            pieces.append(jnp.max(buf[:, cs], axis=0, keepdims=True))
            start = base + LANES
        return jnp.concatenate(pieces + [mt[:, start:]], axis=1)

    def init():
        s0 = jnp.dot(k_ref[0, t - META_BLK:t, :], qzz_ref[...], preferred_element_type=F32)
        s0 = s0 + jnp.concatenate([bmeta_ref[0, 0], bmeta_ref[0, 0]], axis=1)
        m0 = jnp.max(s0, axis=0, keepdims=True)
        m_ref[...] = m0
        acc_ref[...] = jnp.dot(vt_ref[0, 0, 0, :, t - META_BLK:t], probs(s0, m0),
                               preferred_element_type=F32)

    col_chunks = [slice(c, c + COL_CHUNK) for c in range(0, 2 * t, COL_CHUNK)]
    nb = t // META_BLK
    odd = lax.rem(a, 2)

    @pl.when(a == 0)
    def _():
        init()

    @pl.when(a == 1)
    def _():
        init()
        produce(1, sb_ref)
        consume_diag(sb_ref, 1)

    @pl.when((a >= 2) & (odd == 1))
    def _():
        init()
        mtb = produce(1, sb_ref)
        mt_ref[...] = produce(2, sa_ref)
        consume(sb_ref, mtb, vt_ref[0, 1, 0])

    @pl.when((a >= 2) & (odd == 0))
    def _():
        init()
        mt_ref[...] = produce(1, sa_ref)

    @pl.when(a >= 2)
    def _():
        def pair(j, mta):
            mtb = produce(j + 1, sb_ref)
            consume(sa_ref, mta, vt_ref[0, j, 0])
            mta_next = produce(j + 2, sa_ref)
            consume(sb_ref, mtb, vt_ref[0, j + 1, 0])
            return mta_next

        n_pairs = (a - 2) // 2
        odd_pairs = lax.rem(n_pairs, 2)

        @pl.when(odd_pairs == 1)
        def _():
            mt_ref[...] = pair(1 + odd, mt_ref[...])

        j0 = 1 + odd + 2 * odd_pairs

        def quad_body(i, mta):
            j = j0 + 4 * i
            return pair(j + 2, pair(j, mta))

        mta = lax.fori_loop(0, n_pairs // 2, quad_body, mt_ref[...])
        produce(a, sb_ref)
        consume(sa_ref, fix_prev(sa_ref, mta), vt_ref[0, a - 1, 0])
        consume_diag(sb_ref, a)

    lv = lam_ref[...]
    lam = (jnp.exp(jnp.sum(lv[0:1] * lv[1:2], axis=1, keepdims=True))
           - jnp.exp(jnp.sum(lv[2:3] * lv[3:4], axis=1, keepdims=True)) + lam_init)
    on = acc_ref[0:hd, :] * (1.0 / acc_ref[hd:hd + 1, :])
    o = on[:, :t] - lam * on[:, t:]
    o = (o * lax.rsqrt(jnp.mean(o * o, axis=0, keepdims=True) + SUBLN_EPS)) * sg_ref[...]
    o = o * (1.0 - lam_init)
    o_ref[0] = o.T.astype(BF16)


def _attention(q, k, vt, bias, lamv, sg, lam_init):
    b, lh, d = q.shape
    nt = lh // T_SEQ
    hd = d // DA_HEADS
    nb = T_SEQ // META_BLK
    hv = vt.shape[3]
    qtile = pl.BlockSpec((1, T_SEQ, hd), lambda bi, hh, a: (bi, a, hh))
    return pl.pallas_call(
        functools.partial(_attn_kernel, lam_init=lam_init),
        grid=(b, DA_HEADS, nt),
        in_specs=[
            qtile,
            pl.BlockSpec((1, lh, hd), lambda bi, hh, a: (bi, 0, hh)),
            pl.BlockSpec((1, nt, 1, hv, T_SEQ), lambda bi, hh, a: (bi, 0, hh, 0, 0)),
            pl.BlockSpec((1, nb + 1, META_BLK, T_SEQ), lambda bi, hh, a: (hh, 0, 0, 0)),
            pl.BlockSpec((1, 1, META_BLK, T_SEQ),
                         lambda bi, hh, a: (hh, nb + 1 + jnp.minimum(a, 2), 0, 0)),
            pl.BlockSpec(lamv.shape, lambda bi, hh, a: (0, 0)),
            pl.BlockSpec(sg.shape, lambda bi, hh, a: (0, 0)),
        ],
        out_specs=qtile,
        out_shape=jax.ShapeDtypeStruct((b, lh, d), BF16),
        scratch_shapes=[pltpu.VMEM((hd, 2 * T_SEQ), BF16), pltpu.VMEM((1, 2 * T_SEQ), F32),
                        pltpu.VMEM((1, 2 * T_SEQ), F32), pltpu.VMEM((hv, 2 * T_SEQ), F32),
                        pltpu.VMEM((T_SEQ, 2 * T_SEQ), F32), pltpu.VMEM((T_SEQ, 2 * T_SEQ), F32)],
        compiler_params=pltpu.CompilerParams(
            dimension_semantics=("arbitrary", "arbitrary", "arbitrary"),
            vmem_limit_bytes=VMEM_LIMIT_BYTES),
        name="diff_attn",
    )(q, k, vt, bias, bias, lamv, sg)


def _attn_out_ffn_kernel(h_ref, o_ref, wo_ref, gf_ref, wg_ref, wu_ref, wd_ref, out_ref):
    hmid = h_ref[...] + jnp.dot(o_ref[...], wo_ref[...], preferred_element_type=F32)
    out_ref[...] = _ffn(hmid, gf_ref, wg_ref, wu_ref, wd_ref)


def _attn_out_ffn(h, o, wo, gf, wg, wu, wd):
    b, lh, d = h.shape
    rows = b * lh
    tile_f32 = pl.BlockSpec((T_SEQ, d), lambda i: (i, 0))
    out = pl.pallas_call(
        _attn_out_ffn_kernel,
        grid=(rows // T_SEQ,),
        in_specs=[tile_f32, pl.BlockSpec((T_SEQ, d), lambda i: (i, 0)), _const_spec(wo.shape),
                  _const_spec(gf.shape), _const_spec(wg.shape), _const_spec(wu.shape),
                  _const_spec(wd.shape)],
        out_specs=tile_f32,
        out_shape=jax.ShapeDtypeStruct((rows, d), F32),
        compiler_params=pltpu.CompilerParams(
            dimension_semantics=("arbitrary",), vmem_limit_bytes=VMEM_LIMIT_BYTES),
        name="attn_out_ffn",
    )(h.reshape(rows, d), o.reshape(rows, d), wo, gf, wg, wu, wd)
    return out.reshape(b, lh, d)


def _conv_ffn_kernel(h_ref, gm_ref, win_ref, bin_ref, dww_ref, dwb_ref, lng_ref, lnb_ref,
                     wout_ref, bout_ref, gf_ref, wg_ref, wu_ref, wd_ref, o_ref, uext_ref, conv_ref):
    i = pl.program_id(1)
    h = h_ref[0]
    rows, d = h.shape
    pos = _positions(i, rows)
    y = _rms(h, gm_ref[...], RMS_EPS).astype(BF16)
    ag = jnp.dot(y, win_ref[...], preferred_element_type=F32) + bin_ref[...]
    u = ag[:, :d] * jax.nn.sigmoid(ag[:, d:])
    u = jnp.where(pos >= 0, u, 0.0)

    @pl.when(i == 0)
    def _():
        uext_ref[0:CONV_HALO, :] = jnp.zeros((CONV_HALO, d), F32)
        uext_ref[CONV_HALO + rows:, :] = jnp.zeros((SUBLANES, d), F32)

    uext_ref[CONV_HALO:CONV_HALO + rows, :] = u

    base = CONV_HALO - (CONV_WIDTH - 1)
    span = CONV_ROWS + SUBLANES
    n_oct = (base + CONV_WIDTH - 1) // SUBLANES + 1

    def conv_body(c, carry):
        r0 = pl.multiple_of(c * CONV_ROWS, CONV_ROWS)
        for lb in range(d // LANES):
            cs = slice(lb * LANES, (lb + 1) * LANES)
            um = [uext_ref[pl.ds(r0 + SUBLANES * m, span), cs] for m in range(n_oct)]
            out = None
            for sh in range(SUBLANES):
                part = None
                for m in range(n_oct):
                    j = SUBLANES * m + sh - base
                    if 0 <= j < CONV_WIDTH:
                        term = dww_ref[j:j + 1, cs] * um[m]
                        part = term if part is None else part + term
                if sh:
                    part = pltpu.roll(part, span - sh, axis=0)
                out = part if out is None else out + part
            conv_ref[pl.ds(r0, CONV_ROWS), cs] = out[:CONV_ROWS] + dwb_ref[:, cs]
        return carry

    lax.fori_loop(0, rows // CONV_ROWS, conv_body, 0)
    uext_ref[0:CONV_HALO, :] = uext_ref[rows:rows + CONV_HALO, :]

    c = conv_ref[...]
    mu = jnp.mean(c, axis=-1, keepdims=True)
    cc = c - mu
    var = jnp.mean(cc * cc, axis=-1, keepdims=True)
    z = (cc * lax.rsqrt(var + LN_EPS)) * lng_ref[...] + lnb_ref[...]
    z = (z * jax.nn.sigmoid(z)).astype(BF16)
    hmid = h + jnp.dot(z, wout_ref[...], preferred_element_type=F32) + bout_ref[...]
    o_ref[0] = _ffn(hmid, gf_ref, wg_ref, wu_ref, wd_ref)


def _conv_ffn(h, gm, win, bin_, dww, dwb, lng, lnb, wout, bout, gf, wg, wu, wd):
    b, lh, d = h.shape
    nt = lh // T_SEQ
    tile = pl.BlockSpec((1, T_SEQ, d), lambda bi, i: (bi, i, 0))
    consts = [gm, win, bin_, dww, dwb, lng, lnb, wout, bout, gf, wg, wu, wd]
    return pl.pallas_call(
        _conv_ffn_kernel,
        grid=(b, nt),
        in_specs=[tile] + [_const_spec(c.shape) for c in consts],
        out_specs=tile,
        out_shape=jax.ShapeDtypeStruct((b, lh, d), F32),
        scratch_shapes=[pltpu.VMEM((T_SEQ + CONV_HALO + SUBLANES, d), F32), pltpu.VMEM((T_SEQ, d), F32)],
        compiler_params=pltpu.CompilerParams(
            dimension_semantics=("arbitrary", "arbitrary"), vmem_limit_bytes=VMEM_LIMIT_BYTES),
        name="conv_ffn",
    )(h, *consts)


def kernel(x, meta_tokens, norm_mix_g, norm_ffn_g, final_norm_g, rel_bias_table, pool_w, pool_b, pool_scale, attn_w_qkv, attn_w_o, attn_lambda_q1, attn_lambda_k1, attn_lambda_q2, attn_lambda_k2, attn_subln_g, conv_w_in, conv_b_in, conv_dw_w, conv_dw_b, conv_ln_g, conv_ln_b, conv_w_out, conv_b_out, ffn_w_gate, ffn_w_up, ffn_w_down):
    b, seq, d = x.shape
    depth = norm_mix_g.shape[0]
    assert seq % T_SEQ == 0 and d % (LANES * len(POOL_WINDOWS)) == 0
    assert depth % N_MIXERS == 1, "the last layer must be a pooling layer (it applies the final norm)"

    row = lambda v: v.reshape(1, -1).astype(F32)
    h = x.astype(F32)
    bias = None
    for i in range(depth):
        mixer, j = i % N_MIXERS, i // N_MIXERS
        gm, gf = row(norm_mix_g[i]), row(norm_ffn_g[i])
        wg, wu, wd = (w[i].astype(BF16) for w in (ffn_w_gate, ffn_w_up, ffn_w_down))
        if mixer == 0:
            h = _pool_ffn(h, gm, pool_w[j].astype(BF16), row(pool_b[j]), row(pool_scale[j]),
                          gf, wg, wu, wd,
                          meta=meta_tokens.astype(F32) if i == 0 else None,
                          final_g=row(final_norm_g) if i == depth - 1 else None)
        elif mixer == 1:
            lam_init = 0.8 - 0.6 * math.exp(-0.3 * i)
            hd = d // DA_HEADS
            if bias is None:
                bias = _bias_tiles(rel_bias_table.astype(F32), _bias_buckets())
            qscale = (hd // 2) ** -0.5 * LOG2E
            q, k, vt = _qkv(h, gm, attn_w_qkv[j].astype(BF16), qscale)
            lamv = jnp.stack([attn_lambda_q1[j], attn_lambda_k1[j],
                              attn_lambda_q2[j], attn_lambda_k2[j]]).astype(F32)
            o = _attention(q, k, vt, bias, lamv, attn_subln_g[j].reshape(hd, 1).astype(F32), lam_init)
            h = _attn_out_ffn(h, o, attn_w_o[j].astype(BF16), gf, wg, wu, wd)
        else:
            h = _conv_ffn(h, gm, conv_w_in[j].astype(BF16), row(conv_b_in[j]),
                          conv_dw_w[j].astype(F32), row(conv_dw_b[j]), row(conv_ln_g[j]),
                          row(conv_ln_b[j]), conv_w_out[j].astype(BF16), row(conv_b_out[j]),
                          gf, wg, wu, wd)
    return h
```

```python
import functools
import math

import jax
import jax.numpy as jnp
from jax import lax
from jax.experimental import pallas as pl
from jax.experimental.pallas import tpu as pltpu

F32 = jnp.float32
BF16 = jnp.bfloat16

N_META = 16
CHUNK = 64
POOL_WINDOWS = (2, 4, 8, 16)
DA_HEADS = 8
REL_BUCKETS = 32
REL_MAX_DIST = 128
CONV_WIDTH = 31
RMS_EPS = 1e-6
LN_EPS = 1e-5
SUBLN_EPS = 1e-5
N_MIXERS = 3

LANES = 128
SUBLANES = 8
MXU_DIM = 256
VMEM_LIMIT_BYTES = 56 * 1024 * 1024

T_SEQ = 512
META_BLK = 128
ONES_ROWS = 16
COL_CHUNK = 2 * MXU_DIM
POOL_HALO = 16
CONV_HALO = 32
CONV_ROWS = 32
NEG = -1e30
LOG2E = math.log2(math.e)
FAR_BUCKET = REL_BUCKETS // 2 - 1


def _ffn_chunks(f):
    step = 3 * MXU_DIM
    return [(s, min(s + step, f)) for s in range(0, f, step)]


def _rms(x, g, eps):
    ms = jnp.mean(x * x, axis=-1, keepdims=True)
    return (x * lax.rsqrt(ms + eps)) * g


def _ffn(hmid, g_ref, wg_ref, wu_ref, wd_ref):
    x = _rms(hmid, g_ref[...], RMS_EPS).astype(BF16)
    acc = None
    for f0, f1 in _ffn_chunks(wg_ref.shape[1]):
        gate = jnp.dot(x, wg_ref[:, f0:f1], preferred_element_type=F32)
        up = jnp.dot(x, wu_ref[:, f0:f1], preferred_element_type=F32)
        act = ((gate * jax.nn.sigmoid(gate)) * up).astype(BF16)
        part = jnp.dot(act, wd_ref[f0:f1, :], preferred_element_type=F32)
        acc = part if acc is None else acc + part
    return hmid + acc


def _positions(tile_idx, rows):
    r = lax.broadcasted_iota(jnp.int32, (rows, 1), 0)
    return tile_idx * rows + r - (T_SEQ - N_META)


def _pool_ffn_kernel(h_ref, gm_ref, pw_ref, pb_ref, ps_ref, gf_ref, wg_ref, wu_ref, wd_ref, *rest,
                     first, final):
    rest = list(rest)
    meta_ref = rest.pop(0) if first else None
    fg_ref = rest.pop(0) if final else None
    o_ref, halo_ref = rest
    i = pl.program_id(1)
    h = h_ref[0]
    rows, d = h.shape
    if first:
        lead = jnp.concatenate([jnp.zeros((rows - N_META, d), F32), meta_ref[...]], axis=0)
        h = jnp.where(i == 0, lead, h)
    pos = _positions(i, rows)
    y = jnp.where(pos >= 0, _rms(h, gm_ref[...], RMS_EPS), 0.0)

    @pl.when(i == 0)
    def _():
        halo_ref[...] = jnp.zeros_like(halo_ref)

    ext = jnp.concatenate([halo_ref[...], y], axis=0)
    halo_ref[...] = y[rows - POOL_HALO:, :]

    group = d // len(POOL_WINDOWS)
    mixed = []
    for gi, win in enumerate(POOL_WINDOWS):
        c0 = gi * group
        s = ext[:, c0:c0 + group]
        shift = 1
        while shift < win:
            s = s + pltpu.roll(s, shift, axis=0)
            shift *= 2
        cnt = jnp.clip(pos + 1, 1, win).astype(F32)
        pooled = s[POOL_HALO:, :] / cnt - y[:, c0:c0 + group]
        mixed.append(jnp.dot(pooled.astype(BF16), pw_ref[gi], preferred_element_type=F32))
    mixed = jnp.concatenate(mixed, axis=1)
    hmid = h + (mixed + pb_ref[...]) * ps_ref[...]
    out = _ffn(hmid, gf_ref, wg_ref, wu_ref, wd_ref)
    if final:
        out = _rms(out, fg_ref[...], RMS_EPS)
    o_ref[0] = out


def _const_spec(shape):
    nd = len(shape)
    return pl.BlockSpec(shape, lambda *_: (0,) * nd, pipeline_mode=pl.Buffered(1))


def _pool_ffn(h, gm, pw, pb, ps, gf, wg, wu, wd, meta=None, final_g=None):
    first, final = meta is not None, final_g is not None
    b, _, d = h.shape
    lh = h.shape[1] + T_SEQ if first else h.shape[1]
    nt = lh // T_SEQ
    tile = pl.BlockSpec((1, T_SEQ, d), lambda bi, i: (bi, i, 0))
    frames = pl.BlockSpec((1, T_SEQ, d), lambda bi, i: (bi, jnp.maximum(i - 1, 0), 0))
    in_specs = [frames if first else tile, _const_spec(gm.shape), _const_spec(pw.shape),
                _const_spec(pb.shape), _const_spec(ps.shape), _const_spec(gf.shape),
                _const_spec(wg.shape), _const_spec(wu.shape), _const_spec(wd.shape)]
    args = [h, gm, pw, pb, ps, gf, wg, wu, wd]
    if first:
        in_specs.append(_const_spec(meta.shape))
        args.append(meta)
    if final:
        in_specs.append(_const_spec(final_g.shape))
        args.append(final_g)
        out_spec = frames
        out_shape = jax.ShapeDtypeStruct((b, lh - T_SEQ, d), F32)
    else:
        out_spec = tile
        out_shape = jax.ShapeDtypeStruct((b, lh, d), F32)
    return pl.pallas_call(
        functools.partial(_pool_ffn_kernel, first=first, final=final),
        grid=(b, nt),
        in_specs=in_specs,
        out_specs=out_spec,
        out_shape=out_shape,
        scratch_shapes=[pltpu.VMEM((POOL_HALO, d), F32)],
        compiler_params=pltpu.CompilerParams(
            dimension_semantics=("arbitrary", "arbitrary"), vmem_limit_bytes=VMEM_LIMIT_BYTES),
        name="pool_ffn_final" if final else "pool_ffn",
    )(*args)


def _qkv_kernel(h_ref, g_ref, w_ref, q_ref, k_ref, vt_ref, *, qscale):
    h = h_ref[0]
    d = h.shape[1]
    y = _rms(h, g_ref[...], RMS_EPS).astype(BF16)
    q = jnp.dot(y, w_ref[:, 0:d], preferred_element_type=F32) * qscale
    k = jnp.dot(y, w_ref[:, d:2 * d], preferred_element_type=F32)
    v = jnp.dot(y, w_ref[:, 2 * d:3 * d], preferred_element_type=F32)
    q_ref[0] = q.astype(BF16)
    k_ref[0] = k.astype(BF16)
    vt = v.T.astype(BF16)
    heads, hv, rows = vt_ref.shape[2:]
    hd = d // heads
    for hh in range(heads):
        vt_ref[0, 0, hh, 0:hd, :] = vt[hh * hd:(hh + 1) * hd, :]
        vt_ref[0, 0, hh, hd:hv, :] = jnp.ones((hv - hd, rows), BF16)


def _qkv(h, g, w, qscale):
    b, lh, d = h.shape
    nt = lh // T_SEQ
    hv = d // DA_HEADS + ONES_ROWS
    tile = pl.BlockSpec((1, T_SEQ, d), lambda bi, i: (bi, i, 0))
    return pl.pallas_call(
        functools.partial(_qkv_kernel, qscale=qscale),
        grid=(b, nt),
        in_specs=[tile, _const_spec(g.shape), _const_spec(w.shape)],
        out_specs=[tile, tile,
                   pl.BlockSpec((1, 1, DA_HEADS, hv, T_SEQ), lambda bi, i: (bi, i, 0, 0, 0))],
        out_shape=[jax.ShapeDtypeStruct((b, lh, d), BF16), jax.ShapeDtypeStruct((b, lh, d), BF16),
                   jax.ShapeDtypeStruct((b, nt, DA_HEADS, hv, T_SEQ), BF16)],
        compiler_params=pltpu.CompilerParams(
            dimension_semantics=("arbitrary", "arbitrary"), vmem_limit_bytes=VMEM_LIMIT_BYTES),
        name="qkv",
    )(h, g, w)


def _t5_bucket(rel):
    half = REL_BUCKETS // 2
    max_exact = half // 2
    offset = jnp.where(rel > 0, half, 0)
    n = jnp.abs(rel)
    log_ratio = jnp.log(jnp.maximum(n, 1).astype(F32) / max_exact) / math.log(REL_MAX_DIST / max_exact)
    large = jnp.minimum(max_exact + (log_ratio * (half - max_exact)).astype(jnp.int32), half - 1)
    return offset + jnp.where(n < max_exact, n, large)


def _bias_buckets():
    t = T_SEQ
    kl = lax.broadcasted_iota(jnp.int32, (t, t), 0)
    ql = lax.broadcasted_iota(jnp.int32, (t, t), 1)
    prev = _t5_bucket(kl - t - ql)[t - META_BLK:]
    diag = jnp.where(kl // CHUNK <= ql // CHUNK, _t5_bucket(kl - ql), -1)
    kr = lax.broadcasted_iota(jnp.int32, (META_BLK, t), 0)
    qc = lax.broadcasted_iota(jnp.int32, (META_BLK, t), 1)
    mk = kr - (META_BLK - N_META)
    mq = qc - (t - N_META)
    lead = jnp.where(mq >= 0, _t5_bucket(mk - mq), FAR_BUCKET)
    first = _t5_bucket(mk - N_META - qc)
    later = jnp.full((META_BLK, t), FAR_BUCKET, jnp.int32)
    meta = [jnp.where(mk >= 0, m, -1) for m in (lead, first, later)]
    nb = t // META_BLK
    return jnp.concatenate([prev[None], diag.reshape(nb, META_BLK, t), jnp.stack(meta)], axis=0)


def _bias_kernel(tab_ref, bk_ref, o_ref):
    hh = pl.program_id(0)
    far = tab_ref[FAR_BUCKET, hh]
    for blk in range(bk_ref.shape[0]):
        b = bk_ref[blk]
        acc = jnp.zeros(b.shape, F32)
        for t in range(REL_BUCKETS):
            acc = jnp.where(b == t, tab_ref[t, hh] - far, acc)
        o_ref[0, blk] = jnp.where(b < 0, NEG, acc * LOG2E)


def _bias_tiles(table, buckets):
    nb, r, c = buckets.shape
    heads = table.shape[1]
    return pl.pallas_call(
        _bias_kernel,
        grid=(heads,),
        in_specs=[pl.BlockSpec(memory_space=pltpu.SMEM),
                  pl.BlockSpec((nb, r, c), lambda hh: (0, 0, 0))],
        out_specs=pl.BlockSpec((1, nb, r, c), lambda hh: (hh, 0, 0, 0)),
        out_shape=jax.ShapeDtypeStruct((heads, nb, r, c), F32),
        compiler_params=pltpu.CompilerParams(dimension_semantics=("arbitrary",)),
        name="rel_bias",
    )(table, buckets)


def _attn_kernel(q_ref, k_ref, vt_ref, bnear_ref, bmeta_ref, lam_ref, sg_ref, o_ref,
                 qzz_ref, m_ref, mt_ref, acc_ref, sa_ref, sb_ref, *, lam_init):
    a = pl.program_id(2)
    t = q_ref.shape[1]
    hd = q_ref.shape[2]

    qt = q_ref[0].astype(F32).T
    row = lax.broadcasted_iota(jnp.int32, (hd, t), 0)
    q1 = jnp.where(row < hd // 2, qt, 0.0)
    q2 = jnp.where(row >= hd // 2, qt, 0.0)
    qzz_ref[...] = jnp.concatenate([q1, q2], axis=1).astype(BF16)

    def produce(j, buf):
        kt = k_ref[0, pl.ds(pl.multiple_of(j * t, t), t), :]
        mts = []
        for cs in col_chunks:
            s = jnp.dot(kt, qzz_ref[:, cs], preferred_element_type=F32)
            buf[:, cs] = s
            mts.append(jnp.max(s, axis=0, keepdims=True))
        return jnp.concatenate(mts, axis=1)

    def probs(s, m):
        return jnp.exp2(s - m).astype(BF16)

    def consume(buf, mt, vt):
        for cs in col_chunks:
            m_old = m_ref[:, cs]
            m_new = jnp.maximum(m_old, mt[:, cs])
            alpha = jnp.exp2(m_old - m_new)
            acc_ref[:, cs] = alpha * acc_ref[:, cs] + jnp.dot(
                vt, probs(buf[:, cs], m_new), preferred_element_type=F32)
            m_ref[:, cs] = m_new

    def consume_diag(buf, j):
        bias = bnear_ref[0, 1:nb + 1].reshape(t, t)
        mts = []
        for cs in col_chunks:
            bc = slice(cs.start % t, cs.start % t + COL_CHUNK)
            s = buf[:, cs] + bias[:, bc]
            buf[:, cs] = s
            mts.append(jnp.max(s, axis=0, keepdims=True))
        consume(buf, jnp.concatenate(mts, axis=1), vt_ref[0, j, 0])

    def fix_prev(buf, mt):
        corner = bnear_ref[0, 0, :, 0:LANES]
        pieces, start = [], 0
        for base in (0, t):
            cs = slice(base, base + LANES)
            buf[t - META_BLK:, cs] = buf[t - META_BLK:, cs] + corner
            if base > start:
                pieces.append(mt[:, start:base])
            pieces.append(jnp.max(buf[:, cs], axis=0, keepdims=True))
            start = base + LANES
        return jnp.concatenate(pieces + [mt[:, start:]], axis=1)

    def init():
        s0 = jnp.dot(k_ref[0, t - META_BLK:t, :], qzz_ref[...], preferred_element_type=F32)
        s0 = s0 + jnp.concatenate([bmeta_ref[0, 0], bmeta_ref[0, 0]], axis=1)
        m0 = jnp.max(s0, axis=0, keepdims=True)
        m_ref[...] = m0
        acc_ref[...] = jnp.dot(vt_ref[0, 0, 0, :, t - META_BLK:t], probs(s0, m0),
                               preferred_element_type=F32)

    col_chunks = [slice(c, c + COL_CHUNK) for c in range(0, 2 * t, COL_CHUNK)]
    nb = t // META_BLK
    odd = lax.rem(a, 2)

    @pl.when(a == 0)
    def _():
        init()

    @pl.when(a == 1)
    def _():
        init()
        produce(1, sb_ref)
        consume_diag(sb_ref, 1)

    @pl.when((a >= 2) & (odd == 1))
    def _():
        init()
        mtb = produce(1, sb_ref)
        mt_ref[...] = produce(2, sa_ref)
        consume(sb_ref, mtb, vt_ref[0, 1, 0])

    @pl.when((a >= 2) & (odd == 0))
    def _():
        init()
        mt_ref[...] = produce(1, sa_ref)

    @pl.when(a >= 2)
    def _():
        def pair(j, mta):
            mtb = produce(j + 1, sb_ref)
            consume(sa_ref, mta, vt_ref[0, j, 0])
            mta_next = produce(j + 2, sa_ref)
            consume(sb_ref, mtb, vt_ref[0, j + 1, 0])
            return mta_next

        n_pairs = (a - 2) // 2
        odd_pairs = lax.rem(n_pairs, 2)

        @pl.when(odd_pairs == 1)
        def _():
            mt_ref[...] = pair(1 + odd, mt_ref[...])

        n_quads = n_pairs // 2
        odd_quads = lax.rem(n_quads, 2)

        @pl.when(odd_quads == 1)
        def _():
            j = 1 + odd + 2 * odd_pairs
            mt_ref[...] = pair(j + 2, pair(j, mt_ref[...]))

        j0 = 1 + odd + 2 * odd_pairs + 4 * odd_quads

        def oct_body(i, mta):
            j = j0 + 8 * i
            for step in range(0, 8, 2):
                mta = pair(j + step, mta)
            return mta

        mta = lax.fori_loop(0, n_quads // 2, oct_body, mt_ref[...])
        produce(a, sb_ref)
        consume(sa_ref, fix_prev(sa_ref, mta), vt_ref[0, a - 1, 0])
        consume_diag(sb_ref, a)

    lv = lam_ref[...]
    lam = (jnp.exp(jnp.sum(lv[0:1] * lv[1:2], axis=1, keepdims=True))
           - jnp.exp(jnp.sum(lv[2:3] * lv[3:4], axis=1, keepdims=True)) + lam_init)
    on = acc_ref[0:hd, :] * (1.0 / acc_ref[hd:hd + 1, :])
    o = on[:, :t] - lam * on[:, t:]
    o = (o * lax.rsqrt(jnp.mean(o * o, axis=0, keepdims=True) + SUBLN_EPS)) * sg_ref[...]
    o = o * (1.0 - lam_init)
    o_ref[0] = o.T.astype(BF16)


def _attention(q, k, vt, bias, lamv, sg, lam_init):
    b, lh, d = q.shape
    nt = lh // T_SEQ
    hd = d // DA_HEADS
    nb = T_SEQ // META_BLK
    hv = vt.shape[3]
    qtile = pl.BlockSpec((1, T_SEQ, hd), lambda bi, hh, a: (bi, a, hh))
    return pl.pallas_call(
        functools.partial(_attn_kernel, lam_init=lam_init),
        grid=(b, DA_HEADS, nt),
        in_specs=[
            qtile,
            pl.BlockSpec((1, lh, hd), lambda bi, hh, a: (bi, 0, hh)),
            pl.BlockSpec((1, nt, 1, hv, T_SEQ), lambda bi, hh, a: (bi, 0, hh, 0, 0)),
            pl.BlockSpec((1, nb + 1, META_BLK, T_SEQ), lambda bi, hh, a: (hh, 0, 0, 0)),
            pl.BlockSpec((1, 1, META_BLK, T_SEQ),
                         lambda bi, hh, a: (hh, nb + 1 + jnp.minimum(a, 2), 0, 0)),
            pl.BlockSpec(lamv.shape, lambda bi, hh, a: (0, 0)),
            pl.BlockSpec(sg.shape, lambda bi, hh, a: (0, 0)),
        ],
        out_specs=qtile,
        out_shape=jax.ShapeDtypeStruct((b, lh, d), BF16),
        scratch_shapes=[pltpu.VMEM((hd, 2 * T_SEQ), BF16), pltpu.VMEM((1, 2 * T_SEQ), F32),
                        pltpu.VMEM((1, 2 * T_SEQ), F32), pltpu.VMEM((hv, 2 * T_SEQ), F32),
                        pltpu.VMEM((T_SEQ, 2 * T_SEQ), F32), pltpu.VMEM((T_SEQ, 2 * T_SEQ), F32)],
        compiler_params=pltpu.CompilerParams(
            dimension_semantics=("arbitrary", "arbitrary", "arbitrary"),
            vmem_limit_bytes=VMEM_LIMIT_BYTES),
        name="diff_attn",
    )(q, k, vt, bias, bias, lamv, sg)


def _attn_out_ffn_kernel(h_ref, o_ref, wo_ref, gf_ref, wg_ref, wu_ref, wd_ref, out_ref):
    hmid = h_ref[...] + jnp.dot(o_ref[...], wo_ref[...], preferred_element_type=F32)
    out_ref[...] = _ffn(hmid, gf_ref, wg_ref, wu_ref, wd_ref)


def _attn_out_ffn(h, o, wo, gf, wg, wu, wd):
    b, lh, d = h.shape
    rows = b * lh
    tile_f32 = pl.BlockSpec((T_SEQ, d), lambda i: (i, 0))
    out = pl.pallas_call(
        _attn_out_ffn_kernel,
        grid=(rows // T_SEQ,),
        in_specs=[tile_f32, pl.BlockSpec((T_SEQ, d), lambda i: (i, 0)), _const_spec(wo.shape),
                  _const_spec(gf.shape), _const_spec(wg.shape), _const_spec(wu.shape),
                  _const_spec(wd.shape)],
        out_specs=tile_f32,
        out_shape=jax.ShapeDtypeStruct((rows, d), F32),
        compiler_params=pltpu.CompilerParams(
            dimension_semantics=("arbitrary",), vmem_limit_bytes=VMEM_LIMIT_BYTES),
        name="attn_out_ffn",
    )(h.reshape(rows, d), o.reshape(rows, d), wo, gf, wg, wu, wd)
    return out.reshape(b, lh, d)


def _conv_ffn_kernel(h_ref, gm_ref, win_ref, bin_ref, dww_ref, dwb_ref, lng_ref, lnb_ref,
                     wout_ref, bout_ref, gf_ref, wg_ref, wu_ref, wd_ref, o_ref, uext_ref, conv_ref):
    i = pl.program_id(1)
    h = h_ref[0]
    rows, d = h.shape
    pos = _positions(i, rows)
    y = _rms(h, gm_ref[...], RMS_EPS).astype(BF16)
    ag = jnp.dot(y, win_ref[...], preferred_element_type=F32) + bin_ref[...]
    u = ag[:, :d] * jax.nn.sigmoid(ag[:, d:])
    u = jnp.where(pos >= 0, u, 0.0)

    @pl.when(i == 0)
    def _():
        uext_ref[0:CONV_HALO, :] = jnp.zeros((CONV_HALO, d), F32)
        uext_ref[CONV_HALO + rows:, :] = jnp.zeros((SUBLANES, d), F32)

    uext_ref[CONV_HALO:CONV_HALO + rows, :] = u

    base = CONV_HALO - (CONV_WIDTH - 1)
    span = CONV_ROWS + SUBLANES
    n_oct = (base + CONV_WIDTH - 1) // SUBLANES + 1

    def conv_body(c, carry):
        r0 = pl.multiple_of(c * CONV_ROWS, CONV_ROWS)
        for lb in range(d // LANES):
            cs = slice(lb * LANES, (lb + 1) * LANES)
            um = [uext_ref[pl.ds(r0 + SUBLANES * m, span), cs] for m in range(n_oct)]
            out = None
            for sh in range(SUBLANES):
                part = None
                for m in range(n_oct):
                    j = SUBLANES * m + sh - base
                    if 0 <= j < CONV_WIDTH:
                        term = dww_ref[j:j + 1, cs] * um[m]
                        part = term if part is None else part + term
                if sh:
                    part = pltpu.roll(part, span - sh, axis=0)
                out = part if out is None else out + part
            conv_ref[pl.ds(r0, CONV_ROWS), cs] = out[:CONV_ROWS] + dwb_ref[:, cs]
        return carry

    lax.fori_loop(0, rows // CONV_ROWS, conv_body, 0)
    uext_ref[0:CONV_HALO, :] = uext_ref[rows:rows + CONV_HALO, :]

    c = conv_ref[...]
    mu = jnp.mean(c, axis=-1, keepdims=True)
    cc = c - mu
    var = jnp.mean(cc * cc, axis=-1, keepdims=True)
    z = (cc * lax.rsqrt(var + LN_EPS)) * lng_ref[...] + lnb_ref[...]
    z = (z * jax.nn.sigmoid(z)).astype(BF16)
    hmid = h + jnp.dot(z, wout_ref[...], preferred_element_type=F32) + bout_ref[...]
    o_ref[0] = _ffn(hmid, gf_ref, wg_ref, wu_ref, wd_ref)


def _conv_ffn(h, gm, win, bin_, dww, dwb, lng, lnb, wout, bout, gf, wg, wu, wd):
    b, lh, d = h.shape
    nt = lh // T_SEQ
    tile = pl.BlockSpec((1, T_SEQ, d), lambda bi, i: (bi, i, 0))
    consts = [gm, win, bin_, dww, dwb, lng, lnb, wout, bout, gf, wg, wu, wd]
    return pl.pallas_call(
        _conv_ffn_kernel,
        grid=(b, nt),
        in_specs=[tile] + [_const_spec(c.shape) for c in consts],
        out_specs=tile,
        out_shape=jax.ShapeDtypeStruct((b, lh, d), F32),
        scratch_shapes=[pltpu.VMEM((T_SEQ + CONV_HALO + SUBLANES, d), F32), pltpu.VMEM((T_SEQ, d), F32)],
        compiler_params=pltpu.CompilerParams(
            dimension_semantics=("arbitrary", "arbitrary"), vmem_limit_bytes=VMEM_LIMIT_BYTES),
        name="conv_ffn",
    )(h, *consts)


def kernel(x, meta_tokens, norm_mix_g, norm_ffn_g, final_norm_g, rel_bias_table, pool_w, pool_b, pool_scale, attn_w_qkv, attn_w_o, attn_lambda_q1, attn_lambda_k1, attn_lambda_q2, attn_lambda_k2, attn_subln_g, conv_w_in, conv_b_in, conv_dw_w, conv_dw_b, conv_ln_g, conv_ln_b, conv_w_out, conv_b_out, ffn_w_gate, ffn_w_up, ffn_w_down):
    b, seq, d = x.shape
    depth = norm_mix_g.shape[0]
    assert seq % T_SEQ == 0 and d % (LANES * len(POOL_WINDOWS)) == 0
    assert depth % N_MIXERS == 1, "the last layer must be a pooling layer (it applies the final norm)"

    row = lambda v: v.reshape(1, -1).astype(F32)
    h = x.astype(F32)
    bias = None
    for i in range(depth):
        mixer, j = i % N_MIXERS, i // N_MIXERS
        gm, gf = row(norm_mix_g[i]), row(norm_ffn_g[i])
        wg, wu, wd = (w[i].astype(BF16) for w in (ffn_w_gate, ffn_w_up, ffn_w_down))
        if mixer == 0:
            h = _pool_ffn(h, gm, pool_w[j].astype(BF16), row(pool_b[j]), row(pool_scale[j]),
                          gf, wg, wu, wd,
                          meta=meta_tokens.astype(F32) if i == 0 else None,
                          final_g=row(final_norm_g) if i == depth - 1 else None)
        elif mixer == 1:
            lam_init = 0.8 - 0.6 * math.exp(-0.3 * i)
            hd = d // DA_HEADS
            if bias is None:
                bias = _bias_tiles(rel_bias_table.astype(F32), _bias_buckets())
            qscale = (hd // 2) ** -0.5 * LOG2E
            q, k, vt = _qkv(h, gm, attn_w_qkv[j].astype(BF16), qscale)
            lamv = jnp.stack([attn_lambda_q1[j], attn_lambda_k1[j],
                              attn_lambda_q2[j], attn_lambda_k2[j]]).astype(F32)
            o = _attention(q, k, vt, bias, lamv, attn_subln_g[j].reshape(hd, 1).astype(F32), lam_init)
            h = _attn_out_ffn(h, o, attn_w_o[j].astype(BF16), gf, wg, wu, wd)
        else:
            h = _conv_ffn(h, gm, conv_w_in[j].astype(BF16), row(conv_b_in[j]),
                          conv_dw_w[j].astype(F32), row(conv_dw_b[j]), row(conv_ln_g[j]),
                          row(conv_ln_b[j]), conv_w_out[j].astype(BF16), row(conv_b_out[j]),
                          gf, wg, wu, wd)
    return h
```

```python
import functools
import math

import jax
import jax.numpy as jnp
from jax import lax
from jax.experimental import pallas as pl
from jax.experimental.pallas import tpu as pltpu

F32 = jnp.float32
BF16 = jnp.bfloat16

N_META = 16
CHUNK = 64
POOL_WINDOWS = (2, 4, 8, 16)
DA_HEADS = 8
REL_BUCKETS = 32
REL_MAX_DIST = 128
CONV_WIDTH = 31
RMS_EPS = 1e-6
LN_EPS = 1e-5
SUBLN_EPS = 1e-5
N_MIXERS = 3

LANES = 128
SUBLANES = 8
MXU_DIM = 256
VMEM_LIMIT_BYTES = 56 * 1024 * 1024

T_SEQ = 512
META_BLK = 128
ONES_ROWS = 16
COL_CHUNK = 2 * MXU_DIM
POOL_HALO = 16
CONV_HALO = 32
CONV_ROWS = 32
NEG = -1e30
LOG2E = math.log2(math.e)
FAR_BUCKET = REL_BUCKETS // 2 - 1


def _ffn_chunks(f):
    step = 3 * MXU_DIM
    return [(s, min(s + step, f)) for s in range(0, f, step)]


def _rms(x, g, eps):
    ms = jnp.mean(x * x, axis=-1, keepdims=True)
    return (x * lax.rsqrt(ms + eps)) * g


def _ffn(hmid, g_ref, wg_ref, wu_ref, wd_ref):
    x = _rms(hmid, g_ref[...], RMS_EPS).astype(BF16)
    acc = None
    for f0, f1 in _ffn_chunks(wg_ref.shape[1]):
        gate = jnp.dot(x, wg_ref[:, f0:f1], preferred_element_type=F32)
        up = jnp.dot(x, wu_ref[:, f0:f1], preferred_element_type=F32)
        act = ((gate * jax.nn.sigmoid(gate)) * up).astype(BF16)
        part = jnp.dot(act, wd_ref[f0:f1, :], preferred_element_type=F32)
        acc = part if acc is None else acc + part
    return hmid + acc


def _positions(tile_idx, rows):
    r = lax.broadcasted_iota(jnp.int32, (rows, 1), 0)
    return tile_idx * rows + r - (T_SEQ - N_META)


def _pool_ffn_kernel(h_ref, gm_ref, pw_ref, pb_ref, ps_ref, gf_ref, wg_ref, wu_ref, wd_ref, *rest,
                     first, final):
    rest = list(rest)
    meta_ref = rest.pop(0) if first else None
    fg_ref = rest.pop(0) if final else None
    o_ref, halo_ref = rest
    i = pl.program_id(1)
    h = h_ref[0]
    rows, d = h.shape
    if first:
        lead = jnp.concatenate([jnp.zeros((rows - N_META, d), F32), meta_ref[...]], axis=0)
        h = jnp.where(i == 0, lead, h)
    pos = _positions(i, rows)
    y = jnp.where(pos >= 0, _rms(h, gm_ref[...], RMS_EPS), 0.0)

    @pl.when(i == 0)
    def _():
        halo_ref[...] = jnp.zeros_like(halo_ref)

    ext = jnp.concatenate([halo_ref[...], y], axis=0)
    halo_ref[...] = y[rows - POOL_HALO:, :]

    group = d // len(POOL_WINDOWS)
    mixed = []
    for gi, win in enumerate(POOL_WINDOWS):
        c0 = gi * group
        s = ext[:, c0:c0 + group]
        shift = 1
        while shift < win:
            s = s + pltpu.roll(s, shift, axis=0)
            shift *= 2
        cnt = jnp.clip(pos + 1, 1, win).astype(F32)
        pooled = s[POOL_HALO:, :] / cnt - y[:, c0:c0 + group]
        mixed.append(jnp.dot(pooled.astype(BF16), pw_ref[gi], preferred_element_type=F32))
    mixed = jnp.concatenate(mixed, axis=1)
    hmid = h + (mixed + pb_ref[...]) * ps_ref[...]
    out = _ffn(hmid, gf_ref, wg_ref, wu_ref, wd_ref)
    if final:
        out = _rms(out, fg_ref[...], RMS_EPS)
    o_ref[0] = out


def _const_spec(shape):
    nd = len(shape)
    return pl.BlockSpec(shape, lambda *_: (0,) * nd, pipeline_mode=pl.Buffered(1))


def _pool_ffn(h, gm, pw, pb, ps, gf, wg, wu, wd, meta=None, final_g=None):
    first, final = meta is not None, final_g is not None
    b, _, d = h.shape
    lh = h.shape[1] + T_SEQ if first else h.shape[1]
    nt = lh // T_SEQ
    tile = pl.BlockSpec((1, T_SEQ, d), lambda bi, i: (bi, i, 0))
    frames = pl.BlockSpec((1, T_SEQ, d), lambda bi, i: (bi, jnp.maximum(i - 1, 0), 0))
    in_specs = [frames if first else tile, _const_spec(gm.shape), _const_spec(pw.shape),
                _const_spec(pb.shape), _const_spec(ps.shape), _const_spec(gf.shape),
                _const_spec(wg.shape), _const_spec(wu.shape), _const_spec(wd.shape)]
    args = [h, gm, pw, pb, ps, gf, wg, wu, wd]
    if first:
        in_specs.append(_const_spec(meta.shape))
        args.append(meta)
    if final:
        in_specs.append(_const_spec(final_g.shape))
        args.append(final_g)
        out_spec = frames
        out_shape = jax.ShapeDtypeStruct((b, lh - T_SEQ, d), F32)
    else:
        out_spec = tile
        out_shape = jax.ShapeDtypeStruct((b, lh, d), F32)
    return pl.pallas_call(
        functools.partial(_pool_ffn_kernel, first=first, final=final),
        grid=(b, nt),
        in_specs=in_specs,
        out_specs=out_spec,
        out_shape=out_shape,
        scratch_shapes=[pltpu.VMEM((POOL_HALO, d), F32)],
        compiler_params=pltpu.CompilerParams(
            dimension_semantics=("arbitrary", "arbitrary"), vmem_limit_bytes=VMEM_LIMIT_BYTES),
        name="pool_ffn_final" if final else "pool_ffn",
    )(*args)


def _qkv_kernel(h_ref, g_ref, w_ref, q_ref, k_ref, vt_ref, *, qscale):
    h = h_ref[0]
    d = h.shape[1]
    y = _rms(h, g_ref[...], RMS_EPS).astype(BF16)
    q = jnp.dot(y, w_ref[:, 0:d], preferred_element_type=F32) * qscale
    k = jnp.dot(y, w_ref[:, d:2 * d], preferred_element_type=F32)
    v = jnp.dot(y, w_ref[:, 2 * d:3 * d], preferred_element_type=F32)
    q_ref[0] = q.astype(BF16)
    k_ref[0] = k.astype(BF16)
    vt = v.T.astype(BF16)
    heads, hv, rows = vt_ref.shape[2:]
    hd = d // heads
    for hh in range(heads):
        vt_ref[0, 0, hh, 0:hd, :] = vt[hh * hd:(hh + 1) * hd, :]
        vt_ref[0, 0, hh, hd:hv, :] = jnp.ones((hv - hd, rows), BF16)


def _qkv(h, g, w, qscale):
    b, lh, d = h.shape
    nt = lh // T_SEQ
    hv = d // DA_HEADS + ONES_ROWS
    tile = pl.BlockSpec((1, T_SEQ, d), lambda bi, i: (bi, i, 0))
    return pl.pallas_call(
        functools.partial(_qkv_kernel, qscale=qscale),
        grid=(b, nt),
        in_specs=[tile, _const_spec(g.shape), _const_spec(w.shape)],
        out_specs=[tile, tile,
                   pl.BlockSpec((1, 1, DA_HEADS, hv, T_SEQ), lambda bi, i: (bi, i, 0, 0, 0))],
        out_shape=[jax.ShapeDtypeStruct((b, lh, d), BF16), jax.ShapeDtypeStruct((b, lh, d), BF16),
                   jax.ShapeDtypeStruct((b, nt, DA_HEADS, hv, T_SEQ), BF16)],
        compiler_params=pltpu.CompilerParams(
            dimension_semantics=("arbitrary", "arbitrary"), vmem_limit_bytes=VMEM_LIMIT_BYTES),
        name="qkv",
    )(h, g, w)


def _t5_bucket(rel):
    half = REL_BUCKETS // 2
    max_exact = half // 2
    offset = jnp.where(rel > 0, half, 0)
    n = jnp.abs(rel)
    log_ratio = jnp.log(jnp.maximum(n, 1).astype(F32) / max_exact) / math.log(REL_MAX_DIST / max_exact)
    large = jnp.minimum(max_exact + (log_ratio * (half - max_exact)).astype(jnp.int32), half - 1)
    return offset + jnp.where(n < max_exact, n, large)


def _bias_buckets():
    t = T_SEQ
    kl = lax.broadcasted_iota(jnp.int32, (t, t), 0)
    ql = lax.broadcasted_iota(jnp.int32, (t, t), 1)
    prev = _t5_bucket(kl - t - ql)[t - META_BLK:]
    diag = jnp.where(kl // CHUNK <= ql // CHUNK, _t5_bucket(kl - ql), -1)
    kr = lax.broadcasted_iota(jnp.int32, (META_BLK, t), 0)
    qc = lax.broadcasted_iota(jnp.int32, (META_BLK, t), 1)
    mk = kr - (META_BLK - N_META)
    mq = qc - (t - N_META)
    lead = jnp.where(mq >= 0, _t5_bucket(mk - mq), FAR_BUCKET)
    first = _t5_bucket(mk - N_META - qc)
    later = jnp.full((META_BLK, t), FAR_BUCKET, jnp.int32)
    meta = [jnp.where(mk >= 0, m, -1) for m in (lead, first, later)]
    nb = t // META_BLK
    return jnp.concatenate([prev[None], diag.reshape(nb, META_BLK, t), jnp.stack(meta)], axis=0)


def _bias_kernel(tab_ref, bk_ref, o_ref):
    hh = pl.program_id(0)
    far = tab_ref[FAR_BUCKET, hh]
    for blk in range(bk_ref.shape[0]):
        b = bk_ref[blk]
        acc = jnp.zeros(b.shape, F32)
        for t in range(REL_BUCKETS):
            acc = jnp.where(b == t, tab_ref[t, hh] - far, acc)
        o_ref[0, blk] = jnp.where(b < 0, NEG, acc * LOG2E)


def _bias_tiles(table, buckets):
    nb, r, c = buckets.shape
    heads = table.shape[1]
    return pl.pallas_call(
        _bias_kernel,
        grid=(heads,),
        in_specs=[pl.BlockSpec(memory_space=pltpu.SMEM),
                  pl.BlockSpec((nb, r, c), lambda hh: (0, 0, 0))],
        out_specs=pl.BlockSpec((1, nb, r, c), lambda hh: (hh, 0, 0, 0)),
        out_shape=jax.ShapeDtypeStruct((heads, nb, r, c), F32),
        compiler_params=pltpu.CompilerParams(dimension_semantics=("arbitrary",)),
        name="rel_bias",
    )(table, buckets)


def _attn_kernel(q_ref, k_ref, vt_ref, bias_ref, lam_ref, sg_ref, o_ref,
                 qzz_ref, m_ref, mt_ref, acc_ref, sa_ref, sb_ref, *, lam_init):
    lv = lam_ref[...]
    lam = (jnp.exp(jnp.sum(lv[0:1] * lv[1:2], axis=1, keepdims=True))
           - jnp.exp(jnp.sum(lv[2:3] * lv[3:4], axis=1, keepdims=True)) + lam_init)

    def query_tile(a, carry):
        _attn_query_tile(a, lam, q_ref, k_ref, vt_ref, bias_ref, sg_ref, o_ref,
                         qzz_ref, m_ref, mt_ref, acc_ref, sa_ref, sb_ref, lam_init=lam_init)
        return carry

    lax.fori_loop(0, q_ref.shape[1] // T_SEQ, query_tile, 0)


def _attn_query_tile(a, lam, q_ref, k_ref, vt_ref, bias_ref, sg_ref, o_ref,
                     qzz_ref, m_ref, mt_ref, acc_ref, sa_ref, sb_ref, *, lam_init):
    t = T_SEQ
    hd = q_ref.shape[2]
    nb = t // META_BLK
    rows = pl.ds(pl.multiple_of(a * t, t), t)
    bnear_ref = bias_ref

    qt = q_ref[0, rows, :].astype(F32).T
    row = lax.broadcasted_iota(jnp.int32, (hd, t), 0)
    q1 = jnp.where(row < hd // 2, qt, 0.0)
    q2 = jnp.where(row >= hd // 2, qt, 0.0)
    qzz_ref[...] = jnp.concatenate([q1, q2], axis=1).astype(BF16)

    def produce(j, buf):
        kt = k_ref[0, pl.ds(pl.multiple_of(j * t, t), t), :]
        mts = []
        for cs in col_chunks:
            s = jnp.dot(kt, qzz_ref[:, cs], preferred_element_type=F32)
            buf[:, cs] = s
            mts.append(jnp.max(s, axis=0, keepdims=True))
        return jnp.concatenate(mts, axis=1)

    def probs(s, m):
        return jnp.exp2(s - m).astype(BF16)

    def consume(buf, mt, vt):
        for cs in col_chunks:
            m_old = m_ref[:, cs]
            m_new = jnp.maximum(m_old, mt[:, cs])
            alpha = jnp.exp2(m_old - m_new)
            acc_ref[:, cs] = alpha * acc_ref[:, cs] + jnp.dot(
                vt, probs(buf[:, cs], m_new), preferred_element_type=F32)
            m_ref[:, cs] = m_new

    def consume_diag(buf, j):
        bias = bnear_ref[0, 1:nb + 1].reshape(t, t)
        mts = []
        for cs in col_chunks:
            bc = slice(cs.start % t, cs.start % t + COL_CHUNK)
            s = buf[:, cs] + bias[:, bc]
            buf[:, cs] = s
            mts.append(jnp.max(s, axis=0, keepdims=True))
        consume(buf, jnp.concatenate(mts, axis=1), vt_ref[0, j, 0])

    def fix_prev(buf, mt):
        corner = bnear_ref[0, 0, :, 0:LANES]
        pieces, start = [], 0
        for base in (0, t):
            cs = slice(base, base + LANES)
            buf[t - META_BLK:, cs] = buf[t - META_BLK:, cs] + corner
            if base > start:
                pieces.append(mt[:, start:base])
            pieces.append(jnp.max(buf[:, cs], axis=0, keepdims=True))
            start = base + LANES
        return jnp.concatenate(pieces + [mt[:, start:]], axis=1)

    def init():
        s0 = jnp.dot(k_ref[0, t - META_BLK:t, :], qzz_ref[...], preferred_element_type=F32)
        bmeta = bias_ref[0, nb + 1 + jnp.minimum(a, 2)]
        s0 = s0 + jnp.concatenate([bmeta, bmeta], axis=1)
        m0 = jnp.max(s0, axis=0, keepdims=True)
        m_ref[...] = m0
        acc_ref[...] = jnp.dot(vt_ref[0, 0, 0, :, t - META_BLK:t], probs(s0, m0),
                               preferred_element_type=F32)

    col_chunks = [slice(c, c + COL_CHUNK) for c in range(0, 2 * t, COL_CHUNK)]
    odd = lax.rem(a, 2)

    @pl.when(a == 0)
    def _():
        init()

    @pl.when(a == 1)
    def _():
        init()
        produce(1, sb_ref)
        consume_diag(sb_ref, 1)

    @pl.when((a >= 2) & (odd == 1))
    def _():
        init()
        mtb = produce(1, sb_ref)
        mt_ref[...] = produce(2, sa_ref)
        consume(sb_ref, mtb, vt_ref[0, 1, 0])

    @pl.when((a >= 2) & (odd == 0))
    def _():
        init()
        mt_ref[...] = produce(1, sa_ref)

    @pl.when(a >= 2)
    def _():
        def pair(j, mta):
            mtb = produce(j + 1, sb_ref)
            consume(sa_ref, mta, vt_ref[0, j, 0])
            mta_next = produce(j + 2, sa_ref)
            consume(sb_ref, mtb, vt_ref[0, j + 1, 0])
            return mta_next

        n_pairs = (a - 2) // 2
        odd_pairs = lax.rem(n_pairs, 2)

        @pl.when(odd_pairs == 1)
        def _():
            mt_ref[...] = pair(1 + odd, mt_ref[...])

        n_quads = n_pairs // 2
        odd_quads = lax.rem(n_quads, 2)

        @pl.when(odd_quads == 1)
        def _():
            j = 1 + odd + 2 * odd_pairs
            mt_ref[...] = pair(j + 2, pair(j, mt_ref[...]))

        j0 = 1 + odd + 2 * odd_pairs + 4 * odd_quads

        def oct_body(i, mta):
            j = j0 + 8 * i
            for step in range(0, 8, 2):
                mta = pair(j + step, mta)
            return mta

        mta = lax.fori_loop(0, n_quads // 2, oct_body, mt_ref[...])
        produce(a, sb_ref)
        consume(sa_ref, fix_prev(sa_ref, mta), vt_ref[0, a - 1, 0])
        consume_diag(sb_ref, a)

    on = acc_ref[0:hd, :] * (1.0 / acc_ref[hd:hd + 1, :])
    o = on[:, :t] - lam * on[:, t:]
    o = (o * lax.rsqrt(jnp.mean(o * o, axis=0, keepdims=True) + SUBLN_EPS)) * sg_ref[...]
    o = o * (1.0 - lam_init)
    o_ref[0, rows, :] = o.T.astype(BF16)


def _attention(q, k, vt, bias, lamv, sg, lam_init):
    b, lh, d = q.shape
    nt = lh // T_SEQ
    hd = d // DA_HEADS
    hv = vt.shape[3]
    seq = pl.BlockSpec((1, lh, hd), lambda bi, hh: (bi, 0, hh))
    return pl.pallas_call(
        functools.partial(_attn_kernel, lam_init=lam_init),
        grid=(b, DA_HEADS),
        in_specs=[
            seq,
            seq,
            pl.BlockSpec((1, nt, 1, hv, T_SEQ), lambda bi, hh: (bi, 0, hh, 0, 0)),
            pl.BlockSpec((1,) + bias.shape[1:], lambda bi, hh: (hh, 0, 0, 0)),
            pl.BlockSpec(lamv.shape, lambda bi, hh: (0, 0)),
            pl.BlockSpec(sg.shape, lambda bi, hh: (0, 0)),
        ],
        out_specs=seq,
        out_shape=jax.ShapeDtypeStruct((b, lh, d), BF16),
        scratch_shapes=[pltpu.VMEM((hd, 2 * T_SEQ), BF16), pltpu.VMEM((1, 2 * T_SEQ), F32),
                        pltpu.VMEM((1, 2 * T_SEQ), F32), pltpu.VMEM((hv, 2 * T_SEQ), F32),
                        pltpu.VMEM((T_SEQ, 2 * T_SEQ), F32), pltpu.VMEM((T_SEQ, 2 * T_SEQ), F32)],
        compiler_params=pltpu.CompilerParams(
            dimension_semantics=("arbitrary", "arbitrary"), vmem_limit_bytes=VMEM_LIMIT_BYTES),
        name="diff_attn",
    )(q, k, vt, bias, lamv, sg)


def _attn_out_ffn_kernel(h_ref, o_ref, wo_ref, gf_ref, wg_ref, wu_ref, wd_ref, out_ref):
    hmid = h_ref[...] + jnp.dot(o_ref[...], wo_ref[...], preferred_element_type=F32)
    out_ref[...] = _ffn(hmid, gf_ref, wg_ref, wu_ref, wd_ref)


def _attn_out_ffn(h, o, wo, gf, wg, wu, wd):
    b, lh, d = h.shape
    rows = b * lh
    tile_f32 = pl.BlockSpec((T_SEQ, d), lambda i: (i, 0))
    out = pl.pallas_call(
        _attn_out_ffn_kernel,
        grid=(rows // T_SEQ,),
        in_specs=[tile_f32, pl.BlockSpec((T_SEQ, d), lambda i: (i, 0)), _const_spec(wo.shape),
                  _const_spec(gf.shape), _const_spec(wg.shape), _const_spec(wu.shape),
                  _const_spec(wd.shape)],
        out_specs=tile_f32,
        out_shape=jax.ShapeDtypeStruct((rows, d), F32),
        compiler_params=pltpu.CompilerParams(
            dimension_semantics=("arbitrary",), vmem_limit_bytes=VMEM_LIMIT_BYTES),
        name="attn_out_ffn",
    )(h.reshape(rows, d), o.reshape(rows, d), wo, gf, wg, wu, wd)
    return out.reshape(b, lh, d)


def _conv_ffn_kernel(h_ref, gm_ref, win_ref, bin_ref, dww_ref, dwb_ref, lng_ref, lnb_ref,
                     wout_ref, bout_ref, gf_ref, wg_ref, wu_ref, wd_ref, o_ref, uext_ref, conv_ref):
    i = pl.program_id(1)
    h = h_ref[0]
    rows, d = h.shape
    pos = _positions(i, rows)
    y = _rms(h, gm_ref[...], RMS_EPS).astype(BF16)
    ag = jnp.dot(y, win_ref[...], preferred_element_type=F32) + bin_ref[...]
    u = ag[:, :d] * jax.nn.sigmoid(ag[:, d:])
    u = jnp.where(pos >= 0, u, 0.0)

    @pl.when(i == 0)
    def _():
        uext_ref[0:CONV_HALO, :] = jnp.zeros((CONV_HALO, d), F32)
        uext_ref[CONV_HALO + rows:, :] = jnp.zeros((SUBLANES, d), F32)

    uext_ref[CONV_HALO:CONV_HALO + rows, :] = u

    base = CONV_HALO - (CONV_WIDTH - 1)
    span = CONV_ROWS + SUBLANES
    n_oct = (base + CONV_WIDTH - 1) // SUBLANES + 1

    def conv_body(c, carry):
        r0 = pl.multiple_of(c * CONV_ROWS, CONV_ROWS)
        for lb in range(d // LANES):
            cs = slice(lb * LANES, (lb + 1) * LANES)
            um = [uext_ref[pl.ds(r0 + SUBLANES * m, span), cs] for m in range(n_oct)]
            out = None
            for sh in range(SUBLANES):
                part = None
                for m in range(n_oct):
                    j = SUBLANES * m + sh - base
                    if 0 <= j < CONV_WIDTH:
                        term = dww_ref[j:j + 1, cs] * um[m]
                        part = term if part is None else part + term
                if sh:
                    part = pltpu.roll(part, span - sh, axis=0)
                out = part if out is None else out + part
            conv_ref[pl.ds(r0, CONV_ROWS), cs] = out[:CONV_ROWS] + dwb_ref[:, cs]
        return carry

    lax.fori_loop(0, rows // CONV_ROWS, conv_body, 0)
    uext_ref[0:CONV_HALO, :] = uext_ref[rows:rows + CONV_HALO, :]

    c = conv_ref[...]
    mu = jnp.mean(c, axis=-1, keepdims=True)
    cc = c - mu
    var = jnp.mean(cc * cc, axis=-1, keepdims=True)
    z = (cc * lax.rsqrt(var + LN_EPS)) * lng_ref[...] + lnb_ref[...]
    z = (z * jax.nn.sigmoid(z)).astype(BF16)
    hmid = h + jnp.dot(z, wout_ref[...], preferred_element_type=F32) + bout_ref[...]
    o_ref[0] = _ffn(hmid, gf_ref, wg_ref, wu_ref, wd_ref)


def _conv_ffn(h, gm, win, bin_, dww, dwb, lng, lnb, wout, bout, gf, wg, wu, wd):
    b, lh, d = h.shape
    nt = lh // T_SEQ
    tile = pl.BlockSpec((1, T_SEQ, d), lambda bi, i: (bi, i, 0))
    consts = [gm, win, bin_, dww, dwb, lng, lnb, wout, bout, gf, wg, wu, wd]
    return pl.pallas_call(
        _conv_ffn_kernel,
        grid=(b, nt),
        in_specs=[tile] + [_const_spec(c.shape) for c in consts],
        out_specs=tile,
        out_shape=jax.ShapeDtypeStruct((b, lh, d), F32),
        scratch_shapes=[pltpu.VMEM((T_SEQ + CONV_HALO + SUBLANES, d), F32), pltpu.VMEM((T_SEQ, d), F32)],
        compiler_params=pltpu.CompilerParams(
            dimension_semantics=("arbitrary", "arbitrary"), vmem_limit_bytes=VMEM_LIMIT_BYTES),
        name="conv_ffn",
    )(h, *consts)


def kernel(x, meta_tokens, norm_mix_g, norm_ffn_g, final_norm_g, rel_bias_table, pool_w, pool_b, pool_scale, attn_w_qkv, attn_w_o, attn_lambda_q1, attn_lambda_k1, attn_lambda_q2, attn_lambda_k2, attn_subln_g, conv_w_in, conv_b_in, conv_dw_w, conv_dw_b, conv_ln_g, conv_ln_b, conv_w_out, conv_b_out, ffn_w_gate, ffn_w_up, ffn_w_down):
    b, seq, d = x.shape
    depth = norm_mix_g.shape[0]
    assert seq % T_SEQ == 0 and d % (LANES * len(POOL_WINDOWS)) == 0
    assert depth % N_MIXERS == 1, "the last layer must be a pooling layer (it applies the final norm)"

    row = lambda v: v.reshape(1, -1).astype(F32)
    h = x.astype(F32)
    bias = None
    for i in range(depth):
        mixer, j = i % N_MIXERS, i // N_MIXERS
        gm, gf = row(norm_mix_g[i]), row(norm_ffn_g[i])
        wg, wu, wd = (w[i].astype(BF16) for w in (ffn_w_gate, ffn_w_up, ffn_w_down))
        if mixer == 0:
            h = _pool_ffn(h, gm, pool_w[j].astype(BF16), row(pool_b[j]), row(pool_scale[j]),
                          gf, wg, wu, wd,
                          meta=meta_tokens.astype(F32) if i == 0 else None,
                          final_g=row(final_norm_g) if i == depth - 1 else None)
        elif mixer == 1:
            lam_init = 0.8 - 0.6 * math.exp(-0.3 * i)
            hd = d // DA_HEADS
            if bias is None:
                bias = _bias_tiles(rel_bias_table.astype(F32), _bias_buckets())
            qscale = (hd // 2) ** -0.5 * LOG2E
            q, k, vt = _qkv(h, gm, attn_w_qkv[j].astype(BF16), qscale)
            lamv = jnp.stack([attn_lambda_q1[j], attn_lambda_k1[j],
                              attn_lambda_q2[j], attn_lambda_k2[j]]).astype(F32)
            o = _attention(q, k, vt, bias, lamv, attn_subln_g[j].reshape(hd, 1).astype(F32), lam_init)
            h = _attn_out_ffn(h, o, attn_w_o[j].astype(BF16), gf, wg, wu, wd)
        else:
            h = _conv_ffn(h, gm, conv_w_in[j].astype(BF16), row(conv_b_in[j]),
                          conv_dw_w[j].astype(F32), row(conv_dw_b[j]), row(conv_ln_g[j]),
                          row(conv_ln_b[j]), conv_w_out[j].astype(BF16), row(conv_b_out[j]),
                          gf, wg, wu, wd)
    return h
```

```python
import functools
import math

import jax
import jax.numpy as jnp
from jax import lax
from jax.experimental import pallas as pl
from jax.experimental.pallas import tpu as pltpu

F32 = jnp.float32
BF16 = jnp.bfloat16

N_META = 16
CHUNK = 64
POOL_WINDOWS = (2, 4, 8, 16)
DA_HEADS = 8
REL_BUCKETS = 32
REL_MAX_DIST = 128
CONV_WIDTH = 31
RMS_EPS = 1e-6
LN_EPS = 1e-5
SUBLN_EPS = 1e-5
N_MIXERS = 3

LANES = 128
SUBLANES = 8
MXU_DIM = 256
VMEM_LIMIT_BYTES = 56 * 1024 * 1024

T_SEQ = 512
META_BLK = 128
ONES_ROWS = 16
COL_CHUNK = 2 * MXU_DIM
POOL_HALO = 16
CONV_HALO = 32
CONV_ROWS = 32
NEG = -1e30
LOG2E = math.log2(math.e)
FAR_BUCKET = REL_BUCKETS // 2 - 1


def _ffn_chunks(f):
    step = 3 * MXU_DIM
    return [(s, min(s + step, f)) for s in range(0, f, step)]


def _rms(x, g, eps):
    ms = jnp.mean(x * x, axis=-1, keepdims=True)
    return (x * lax.rsqrt(ms + eps)) * g


def _ffn(hmid, g_ref, wg_ref, wu_ref, wd_ref):
    x = _rms(hmid, g_ref[...], RMS_EPS).astype(BF16)
    acc = None
    for f0, f1 in _ffn_chunks(wg_ref.shape[1]):
        gate = jnp.dot(x, wg_ref[:, f0:f1], preferred_element_type=F32)
        up = jnp.dot(x, wu_ref[:, f0:f1], preferred_element_type=F32)
        act = ((gate * jax.nn.sigmoid(gate)) * up).astype(BF16)
        part = jnp.dot(act, wd_ref[f0:f1, :], preferred_element_type=F32)
        acc = part if acc is None else acc + part
    return hmid + acc


def _positions(tile_idx, rows):
    r = lax.broadcasted_iota(jnp.int32, (rows, 1), 0)
    return tile_idx * rows + r - (T_SEQ - N_META)


def _pool_ffn_kernel(h_ref, gm_ref, pw_ref, pb_ref, ps_ref, gf_ref, wg_ref, wu_ref, wd_ref, *rest,
                     first, final):
    rest = list(rest)
    meta_ref = rest.pop(0) if first else None
    fg_ref = rest.pop(0) if final else None
    o_ref, halo_ref = rest
    i = pl.program_id(1)
    h = h_ref[0]
    rows, d = h.shape
    if first:
        lead = jnp.concatenate([jnp.zeros((rows - N_META, d), F32), meta_ref[...]], axis=0)
        h = jnp.where(i == 0, lead, h)
    pos = _positions(i, rows)
    y = jnp.where(pos >= 0, _rms(h, gm_ref[...], RMS_EPS), 0.0)

    @pl.when(i == 0)
    def _():
        halo_ref[...] = jnp.zeros_like(halo_ref)

    ext = jnp.concatenate([halo_ref[...], y], axis=0)
    halo_ref[...] = y[rows - POOL_HALO:, :]

    group = d // len(POOL_WINDOWS)
    mixed = []
    for gi, win in enumerate(POOL_WINDOWS):
        c0 = gi * group
        s = ext[:, c0:c0 + group]
        shift = 1
        while shift < win:
            s = s + pltpu.roll(s, shift, axis=0)
            shift *= 2
        cnt = jnp.clip(pos + 1, 1, win).astype(F32)
        pooled = s[POOL_HALO:, :] / cnt - y[:, c0:c0 + group]
        mixed.append(jnp.dot(pooled.astype(BF16), pw_ref[gi], preferred_element_type=F32))
    mixed = jnp.concatenate(mixed, axis=1)
    hmid = h + (mixed + pb_ref[...]) * ps_ref[...]
    out = _ffn(hmid, gf_ref, wg_ref, wu_ref, wd_ref)
    if final:
        out = _rms(out, fg_ref[...], RMS_EPS)
    o_ref[0] = out


def _const_spec(shape):
    nd = len(shape)
    return pl.BlockSpec(shape, lambda *_: (0,) * nd, pipeline_mode=pl.Buffered(1))


def _pool_ffn(h, gm, pw, pb, ps, gf, wg, wu, wd, meta=None, final_g=None):
    first, final = meta is not None, final_g is not None
    b, _, d = h.shape
    lh = h.shape[1] + T_SEQ if first else h.shape[1]
    nt = lh // T_SEQ
    tile = pl.BlockSpec((1, T_SEQ, d), lambda bi, i: (bi, i, 0))
    frames = pl.BlockSpec((1, T_SEQ, d), lambda bi, i: (bi, jnp.maximum(i - 1, 0), 0))
    in_specs = [frames if first else tile, _const_spec(gm.shape), _const_spec(pw.shape),
                _const_spec(pb.shape), _const_spec(ps.shape), _const_spec(gf.shape),
                _const_spec(wg.shape), _const_spec(wu.shape), _const_spec(wd.shape)]
    args = [h, gm, pw, pb, ps, gf, wg, wu, wd]
    if first:
        in_specs.append(_const_spec(meta.shape))
        args.append(meta)
    if final:
        in_specs.append(_const_spec(final_g.shape))
        args.append(final_g)
        out_spec = frames
        out_shape = jax.ShapeDtypeStruct((b, lh - T_SEQ, d), F32)
    else:
        out_spec = tile
        out_shape = jax.ShapeDtypeStruct((b, lh, d), F32)
    return pl.pallas_call(
        functools.partial(_pool_ffn_kernel, first=first, final=final),
        grid=(b, nt),
        in_specs=in_specs,
        out_specs=out_spec,
        out_shape=out_shape,
        scratch_shapes=[pltpu.VMEM((POOL_HALO, d), F32)],
        compiler_params=pltpu.CompilerParams(
            dimension_semantics=("arbitrary", "arbitrary"), vmem_limit_bytes=VMEM_LIMIT_BYTES),
        name="pool_ffn_final" if final else "pool_ffn",
    )(*args)


def _qkv_kernel(h_ref, g_ref, w_ref, q_ref, k_ref, vt_ref, *, qscale):
    h = h_ref[0]
    d = h.shape[1]
    y = _rms(h, g_ref[...], RMS_EPS).astype(BF16)
    q = jnp.dot(y, w_ref[:, 0:d], preferred_element_type=F32) * qscale
    k = jnp.dot(y, w_ref[:, d:2 * d], preferred_element_type=F32)
    v = jnp.dot(y, w_ref[:, 2 * d:3 * d], preferred_element_type=F32)
    q_ref[0] = q.astype(BF16)
    k_ref[0] = k.astype(BF16)
    vt = v.T.astype(BF16)
    heads, hv, rows = vt_ref.shape[2:]
    hd = d // heads
    for hh in range(heads):
        vt_ref[0, 0, hh, 0:hd, :] = vt[hh * hd:(hh + 1) * hd, :]
        vt_ref[0, 0, hh, hd:hv, :] = jnp.ones((hv - hd, rows), BF16)


def _qkv(h, g, w, qscale):
    b, lh, d = h.shape
    nt = lh // T_SEQ
    hv = d // DA_HEADS + ONES_ROWS
    tile = pl.BlockSpec((1, T_SEQ, d), lambda bi, i: (bi, i, 0))
    return pl.pallas_call(
        functools.partial(_qkv_kernel, qscale=qscale),
        grid=(b, nt),
        in_specs=[tile, _const_spec(g.shape), _const_spec(w.shape)],
        out_specs=[tile, tile,
                   pl.BlockSpec((1, 1, DA_HEADS, hv, T_SEQ), lambda bi, i: (bi, i, 0, 0, 0))],
        out_shape=[jax.ShapeDtypeStruct((b, lh, d), BF16), jax.ShapeDtypeStruct((b, lh, d), BF16),
                   jax.ShapeDtypeStruct((b, nt, DA_HEADS, hv, T_SEQ), BF16)],
        compiler_params=pltpu.CompilerParams(
            dimension_semantics=("arbitrary", "arbitrary"), vmem_limit_bytes=VMEM_LIMIT_BYTES),
        name="qkv",
    )(h, g, w)


def _t5_bucket(rel):
    half = REL_BUCKETS // 2
    max_exact = half // 2
    offset = jnp.where(rel > 0, half, 0)
    n = jnp.abs(rel)
    log_ratio = jnp.log(jnp.maximum(n, 1).astype(F32) / max_exact) / math.log(REL_MAX_DIST / max_exact)
    large = jnp.minimum(max_exact + (log_ratio * (half - max_exact)).astype(jnp.int32), half - 1)
    return offset + jnp.where(n < max_exact, n, large)


def _bias_buckets():
    t = T_SEQ
    kl = lax.broadcasted_iota(jnp.int32, (t, t), 0)
    ql = lax.broadcasted_iota(jnp.int32, (t, t), 1)
    prev = _t5_bucket(kl - t - ql)[t - META_BLK:]
    diag = jnp.where(kl // CHUNK <= ql // CHUNK, _t5_bucket(kl - ql), -1)
    kr = lax.broadcasted_iota(jnp.int32, (META_BLK, t), 0)
    qc = lax.broadcasted_iota(jnp.int32, (META_BLK, t), 1)
    mk = kr - (META_BLK - N_META)
    mq = qc - (t - N_META)
    lead = jnp.where(mq >= 0, _t5_bucket(mk - mq), FAR_BUCKET)
    first = _t5_bucket(mk - N_META - qc)
    later = jnp.full((META_BLK, t), FAR_BUCKET, jnp.int32)
    meta = [jnp.where(mk >= 0, m, -1) for m in (lead, first, later)]
    nb = t // META_BLK
    return jnp.concatenate([prev[None], diag.reshape(nb, META_BLK, t), jnp.stack(meta)], axis=0)


def _bias_kernel(tab_ref, bk_ref, o_ref):
    hh = pl.program_id(0)
    far = tab_ref[FAR_BUCKET, hh]
    for blk in range(bk_ref.shape[0]):
        b = bk_ref[blk]
        acc = jnp.zeros(b.shape, F32)
        for t in range(REL_BUCKETS):
            acc = jnp.where(b == t, tab_ref[t, hh] - far, acc)
        o_ref[0, blk] = jnp.where(b < 0, NEG, acc * LOG2E)


def _bias_tiles(table, buckets):
    nb, r, c = buckets.shape
    heads = table.shape[1]
    return pl.pallas_call(
        _bias_kernel,
        grid=(heads,),
        in_specs=[pl.BlockSpec(memory_space=pltpu.SMEM),
                  pl.BlockSpec((nb, r, c), lambda hh: (0, 0, 0))],
        out_specs=pl.BlockSpec((1, nb, r, c), lambda hh: (hh, 0, 0, 0)),
        out_shape=jax.ShapeDtypeStruct((heads, nb, r, c), F32),
        compiler_params=pltpu.CompilerParams(dimension_semantics=("arbitrary",)),
        name="rel_bias",
    )(table, buckets)


def _attn_kernel(q_ref, k_ref, vt_ref, bias_ref, lam_ref, sg_ref, o_ref,
                 qzz_ref, m_ref, mt_ref, acc_ref, sa_ref, sb_ref, *, lam_init):
    lv = lam_ref[...]
    lam = (jnp.exp(jnp.sum(lv[0:1] * lv[1:2], axis=1, keepdims=True))
           - jnp.exp(jnp.sum(lv[2:3] * lv[3:4], axis=1, keepdims=True)) + lam_init)

    def query_tile(a, carry):
        _attn_query_tile(a, lam, q_ref, k_ref, vt_ref, bias_ref, sg_ref, o_ref,
                         qzz_ref, m_ref, mt_ref, acc_ref, sa_ref, sb_ref, lam_init=lam_init)
        return carry

    n_tiles = q_ref.shape[1] // T_SEQ
    lax.fori_loop(0, n_tiles, query_tile, 0)
    _attn_finalize(n_tiles - 1, lam, acc_ref, sg_ref, o_ref, lam_init)


def _attn_query_tile(a, lam, q_ref, k_ref, vt_ref, bias_ref, sg_ref, o_ref,
                     qzz_ref, m_ref, mt_ref, acc_ref, sa_ref, sb_ref, *, lam_init):
    t = T_SEQ
    hd = q_ref.shape[2]
    nb = t // META_BLK
    rows = pl.ds(pl.multiple_of(a * t, t), t)
    bnear_ref = bias_ref

    qt = q_ref[0, rows, :].astype(F32).T
    row = lax.broadcasted_iota(jnp.int32, (hd, t), 0)
    q1 = jnp.where(row < hd // 2, qt, 0.0)
    q2 = jnp.where(row >= hd // 2, qt, 0.0)
    qzz_ref[...] = jnp.concatenate([q1, q2], axis=1).astype(BF16)

    def produce(j, buf):
        kt = k_ref[0, pl.ds(pl.multiple_of(j * t, t), t), :]
        mts = []
        for cs in col_chunks:
            s = jnp.dot(kt, qzz_ref[:, cs], preferred_element_type=F32)
            buf[:, cs] = s
            mts.append(jnp.max(s, axis=0, keepdims=True))
        return jnp.concatenate(mts, axis=1)

    def probs(s, m):
        return jnp.exp2(s - m).astype(BF16)

    def consume(buf, mt, vt):
        for cs in col_chunks:
            m_old = m_ref[:, cs]
            m_new = jnp.maximum(m_old, mt[:, cs])
            alpha = jnp.exp2(m_old - m_new)
            acc_ref[:, cs] = alpha * acc_ref[:, cs] + jnp.dot(
                vt, probs(buf[:, cs], m_new), preferred_element_type=F32)
            m_ref[:, cs] = m_new

    def consume_diag(buf, j):
        bias = bnear_ref[0, 1:nb + 1].reshape(t, t)
        mts = []
        for cs in col_chunks:
            bc = slice(cs.start % t, cs.start % t + COL_CHUNK)
            s = buf[:, cs] + bias[:, bc]
            buf[:, cs] = s
            mts.append(jnp.max(s, axis=0, keepdims=True))
        consume(buf, jnp.concatenate(mts, axis=1), vt_ref[0, j, 0])

    def fix_prev(buf, mt):
        corner = bnear_ref[0, 0, :, 0:LANES]
        pieces, start = [], 0
        for base in (0, t):
            cs = slice(base, base + LANES)
            buf[t - META_BLK:, cs] = buf[t - META_BLK:, cs] + corner
            if base > start:
                pieces.append(mt[:, start:base])
            pieces.append(jnp.max(buf[:, cs], axis=0, keepdims=True))
            start = base + LANES
        return jnp.concatenate(pieces + [mt[:, start:]], axis=1)

    def init():
        s0 = jnp.dot(k_ref[0, t - META_BLK:t, :], qzz_ref[...], preferred_element_type=F32)
        bmeta = bias_ref[0, nb + 1 + jnp.minimum(a, 2)]
        s0 = s0 + jnp.concatenate([bmeta, bmeta], axis=1)
        m0 = jnp.max(s0, axis=0, keepdims=True)
        m_ref[...] = m0
        acc_ref[...] = jnp.dot(vt_ref[0, 0, 0, :, t - META_BLK:t], probs(s0, m0),
                               preferred_element_type=F32)

    col_chunks = [slice(c, c + COL_CHUNK) for c in range(0, 2 * t, COL_CHUNK)]
    odd = lax.rem(a, 2)

    def finish_previous():
        _attn_finalize(a - 1, lam, acc_ref, sg_ref, o_ref, lam_init)

    @pl.when(a == 0)
    def _():
        init()

    @pl.when(a == 1)
    def _():
        finish_previous()
        init()
        produce(1, sb_ref)
        consume_diag(sb_ref, 1)

    @pl.when((a >= 2) & (odd == 1))
    def _():
        finish_previous()
        init()
        mtb = produce(1, sb_ref)
        mt_ref[...] = produce(2, sa_ref)
        consume(sb_ref, mtb, vt_ref[0, 1, 0])

    @pl.when((a >= 2) & (odd == 0))
    def _():
        finish_previous()
        init()
        mt_ref[...] = produce(1, sa_ref)

    @pl.when(a >= 2)
    def _():
        def pair(j, mta):
            mtb = produce(j + 1, sb_ref)
            consume(sa_ref, mta, vt_ref[0, j, 0])
            mta_next = produce(j + 2, sa_ref)
            consume(sb_ref, mtb, vt_ref[0, j + 1, 0])
            return mta_next

        n_pairs = (a - 2) // 2
        odd_pairs = lax.rem(n_pairs, 2)

        @pl.when(odd_pairs == 1)
        def _():
            mt_ref[...] = pair(1 + odd, mt_ref[...])

        n_quads = n_pairs // 2
        odd_quads = lax.rem(n_quads, 2)

        @pl.when(odd_quads == 1)
        def _():
            j = 1 + odd + 2 * odd_pairs
            mt_ref[...] = pair(j + 2, pair(j, mt_ref[...]))

        j0 = 1 + odd + 2 * odd_pairs + 4 * odd_quads

        def oct_body(i, mta):
            j = j0 + 8 * i
            for step in range(0, 8, 2):
                mta = pair(j + step, mta)
            return mta

        mta = lax.fori_loop(0, n_quads // 2, oct_body, mt_ref[...])
        produce(a, sb_ref)
        consume(sa_ref, fix_prev(sa_ref, mta), vt_ref[0, a - 1, 0])
        consume_diag(sb_ref, a)


def _attn_finalize(a, lam, acc_ref, sg_ref, o_ref, lam_init):
    t = T_SEQ
    hd = o_ref.shape[2]
    on = acc_ref[0:hd, :] * (1.0 / acc_ref[hd:hd + 1, :])
    o = on[:, :t] - lam * on[:, t:]
    o = (o * lax.rsqrt(jnp.mean(o * o, axis=0, keepdims=True) + SUBLN_EPS)) * sg_ref[...]
    o = o * (1.0 - lam_init)
    o_ref[0, pl.ds(pl.multiple_of(a * t, t), t), :] = o.T.astype(BF16)


def _attention(q, k, vt, bias, lamv, sg, lam_init):
    b, lh, d = q.shape
    nt = lh // T_SEQ
    hd = d // DA_HEADS
    hv = vt.shape[3]
    seq = pl.BlockSpec((1, lh, hd), lambda bi, hh: (bi, 0, hh))
    return pl.pallas_call(
        functools.partial(_attn_kernel, lam_init=lam_init),
        grid=(b, DA_HEADS),
        in_specs=[
            seq,
            seq,
            pl.BlockSpec((1, nt, 1, hv, T_SEQ), lambda bi, hh: (bi, 0, hh, 0, 0)),
            pl.BlockSpec((1,) + bias.shape[1:], lambda bi, hh: (hh, 0, 0, 0)),
            pl.BlockSpec(lamv.shape, lambda bi, hh: (0, 0)),
            pl.BlockSpec(sg.shape, lambda bi, hh: (0, 0)),
        ],
        out_specs=seq,
        out_shape=jax.ShapeDtypeStruct((b, lh, d), BF16),
        scratch_shapes=[pltpu.VMEM((hd, 2 * T_SEQ), BF16), pltpu.VMEM((1, 2 * T_SEQ), F32),
                        pltpu.VMEM((1, 2 * T_SEQ), F32), pltpu.VMEM((hv, 2 * T_SEQ), F32),
                        pltpu.VMEM((T_SEQ, 2 * T_SEQ), F32), pltpu.VMEM((T_SEQ, 2 * T_SEQ), F32)],
        compiler_params=pltpu.CompilerParams(
            dimension_semantics=("arbitrary", "arbitrary"), vmem_limit_bytes=VMEM_LIMIT_BYTES),
        name="diff_attn",
    )(q, k, vt, bias, lamv, sg)


def _attn_out_ffn_kernel(h_ref, o_ref, wo_ref, gf_ref, wg_ref, wu_ref, wd_ref, out_ref):
    hmid = h_ref[...] + jnp.dot(o_ref[...], wo_ref[...], preferred_element_type=F32)
    out_ref[...] = _ffn(hmid, gf_ref, wg_ref, wu_ref, wd_ref)


def _attn_out_ffn(h, o, wo, gf, wg, wu, wd):
    b, lh, d = h.shape
    rows = b * lh
    tile_f32 = pl.BlockSpec((T_SEQ, d), lambda i: (i, 0))
    out = pl.pallas_call(
        _attn_out_ffn_kernel,
        grid=(rows // T_SEQ,),
        in_specs=[tile_f32, pl.BlockSpec((T_SEQ, d), lambda i: (i, 0)), _const_spec(wo.shape),
                  _const_spec(gf.shape), _const_spec(wg.shape), _const_spec(wu.shape),
                  _const_spec(wd.shape)],
        out_specs=tile_f32,
        out_shape=jax.ShapeDtypeStruct((rows, d), F32),
        compiler_params=pltpu.CompilerParams(
            dimension_semantics=("arbitrary",), vmem_limit_bytes=VMEM_LIMIT_BYTES),
        name="attn_out_ffn",
    )(h.reshape(rows, d), o.reshape(rows, d), wo, gf, wg, wu, wd)
    return out.reshape(b, lh, d)


def _conv_ffn_kernel(h_ref, gm_ref, win_ref, bin_ref, dww_ref, dwb_ref, lng_ref, lnb_ref,
                     wout_ref, bout_ref, gf_ref, wg_ref, wu_ref, wd_ref, o_ref, uext_ref, conv_ref):
    i = pl.program_id(1)
    h = h_ref[0]
    rows, d = h.shape
    pos = _positions(i, rows)
    y = _rms(h, gm_ref[...], RMS_EPS).astype(BF16)
    ag = jnp.dot(y, win_ref[...], preferred_element_type=F32) + bin_ref[...]
    u = ag[:, :d] * jax.nn.sigmoid(ag[:, d:])
    u = jnp.where(pos >= 0, u, 0.0)

    @pl.when(i == 0)
    def _():
        uext_ref[0:CONV_HALO, :] = jnp.zeros((CONV_HALO, d), F32)
        uext_ref[CONV_HALO + rows:, :] = jnp.zeros((SUBLANES, d), F32)

    uext_ref[CONV_HALO:CONV_HALO + rows, :] = u

    base = CONV_HALO - (CONV_WIDTH - 1)
    span = CONV_ROWS + SUBLANES
    n_oct = (base + CONV_WIDTH - 1) // SUBLANES + 1

    def conv_body(c, carry):
        r0 = pl.multiple_of(c * CONV_ROWS, CONV_ROWS)
        for lb in range(d // LANES):
            cs = slice(lb * LANES, (lb + 1) * LANES)
            um = [uext_ref[pl.ds(r0 + SUBLANES * m, span), cs] for m in range(n_oct)]
            out = None
            for sh in range(SUBLANES):
                part = None
                for m in range(n_oct):
                    j = SUBLANES * m + sh - base
                    if 0 <= j < CONV_WIDTH:
                        term = dww_ref[j:j + 1, cs] * um[m]
                        part = term if part is None else part + term
                if sh:
                    part = pltpu.roll(part, span - sh, axis=0)
                out = part if out is None else out + part
            conv_ref[pl.ds(r0, CONV_ROWS), cs] = out[:CONV_ROWS] + dwb_ref[:, cs]
        return carry

    lax.fori_loop(0, rows // CONV_ROWS, conv_body, 0)
    uext_ref[0:CONV_HALO, :] = uext_ref[rows:rows + CONV_HALO, :]

    c = conv_ref[...]
    mu = jnp.mean(c, axis=-1, keepdims=True)
    cc = c - mu
    var = jnp.mean(cc * cc, axis=-1, keepdims=True)
    z = (cc * lax.rsqrt(var + LN_EPS)) * lng_ref[...] + lnb_ref[...]
    z = (z * jax.nn.sigmoid(z)).astype(BF16)
    hmid = h + jnp.dot(z, wout_ref[...], preferred_element_type=F32) + bout_ref[...]
    o_ref[0] = _ffn(hmid, gf_ref, wg_ref, wu_ref, wd_ref)


def _conv_ffn(h, gm, win, bin_, dww, dwb, lng, lnb, wout, bout, gf, wg, wu, wd):
    b, lh, d = h.shape
    nt = lh // T_SEQ
    tile = pl.BlockSpec((1, T_SEQ, d), lambda bi, i: (bi, i, 0))
    consts = [gm, win, bin_, dww, dwb, lng, lnb, wout, bout, gf, wg, wu, wd]
    return pl.pallas_call(
        _conv_ffn_kernel,
        grid=(b, nt),
        in_specs=[tile] + [_const_spec(c.shape) for c in consts],
        out_specs=tile,
        out_shape=jax.ShapeDtypeStruct((b, lh, d), F32),
        scratch_shapes=[pltpu.VMEM((T_SEQ + CONV_HALO + SUBLANES, d), F32), pltpu.VMEM((T_SEQ, d), F32)],
        compiler_params=pltpu.CompilerParams(
            dimension_semantics=("arbitrary", "arbitrary"), vmem_limit_bytes=VMEM_LIMIT_BYTES),
        name="conv_ffn",
    )(h, *consts)


def kernel(x, meta_tokens, norm_mix_g, norm_ffn_g, final_norm_g, rel_bias_table, pool_w, pool_b, pool_scale, attn_w_qkv, attn_w_o, attn_lambda_q1, attn_lambda_k1, attn_lambda_q2, attn_lambda_k2, attn_subln_g, conv_w_in, conv_b_in, conv_dw_w, conv_dw_b, conv_ln_g, conv_ln_b, conv_w_out, conv_b_out, ffn_w_gate, ffn_w_up, ffn_w_down):
    b, seq, d = x.shape
    depth = norm_mix_g.shape[0]
    assert seq % T_SEQ == 0 and d % (LANES * len(POOL_WINDOWS)) == 0
    assert depth % N_MIXERS == 1, "the last layer must be a pooling layer (it applies the final norm)"

    row = lambda v: v.reshape(1, -1).astype(F32)
    h = x.astype(F32)
    bias = None
    for i in range(depth):
        mixer, j = i % N_MIXERS, i // N_MIXERS
        gm, gf = row(norm_mix_g[i]), row(norm_ffn_g[i])
        wg, wu, wd = (w[i].astype(BF16) for w in (ffn_w_gate, ffn_w_up, ffn_w_down))
        if mixer == 0:
            h = _pool_ffn(h, gm, pool_w[j].astype(BF16), row(pool_b[j]), row(pool_scale[j]),
                          gf, wg, wu, wd,
                          meta=meta_tokens.astype(F32) if i == 0 else None,
                          final_g=row(final_norm_g) if i == depth - 1 else None)
        elif mixer == 1:
            lam_init = 0.8 - 0.6 * math.exp(-0.3 * i)
            hd = d // DA_HEADS
            if bias is None:
                bias = _bias_tiles(rel_bias_table.astype(F32), _bias_buckets())
            qscale = (hd // 2) ** -0.5 * LOG2E
            q, k, vt = _qkv(h, gm, attn_w_qkv[j].astype(BF16), qscale)
            lamv = jnp.stack([attn_lambda_q1[j], attn_lambda_k1[j],
                              attn_lambda_q2[j], attn_lambda_k2[j]]).astype(F32)
            o = _attention(q, k, vt, bias, lamv, attn_subln_g[j].reshape(hd, 1).astype(F32), lam_init)
            h = _attn_out_ffn(h, o, attn_w_o[j].astype(BF16), gf, wg, wu, wd)
        else:
            h = _conv_ffn(h, gm, conv_w_in[j].astype(BF16), row(conv_b_in[j]),
                          conv_dw_w[j].astype(F32), row(conv_dw_b[j]), row(conv_ln_g[j]),
                          row(conv_ln_b[j]), conv_w_out[j].astype(BF16), row(conv_b_out[j]),
                          gf, wg, wu, wd)
    return h
```

```python
import functools
import math

import jax
import jax.numpy as jnp
from jax import lax
from jax.experimental import pallas as pl
from jax.experimental.pallas import tpu as pltpu

F32 = jnp.float32
BF16 = jnp.bfloat16

N_META = 16
CHUNK = 64
POOL_WINDOWS = (2, 4, 8, 16)
DA_HEADS = 8
REL_BUCKETS = 32
REL_MAX_DIST = 128
CONV_WIDTH = 31
RMS_EPS = 1e-6
LN_EPS = 1e-5
SUBLN_EPS = 1e-5
N_MIXERS = 3

LANES = 128
SUBLANES = 8
MXU_DIM = 256
VMEM_LIMIT_BYTES = 56 * 1024 * 1024

T_SEQ = 512
META_BLK = 128
ONES_ROWS = 16
COL_CHUNK = 2 * MXU_DIM
POOL_HALO = 16
CONV_HALO = 32
CONV_ROWS = 32
NEG = -1e30
LOG2E = math.log2(math.e)
FAR_BUCKET = REL_BUCKETS // 2 - 1


def _ffn_chunks(f):
    step = 3 * MXU_DIM
    return [(s, min(s + step, f)) for s in range(0, f, step)]


def _rms(x, g, eps):
    ms = jnp.mean(x * x, axis=-1, keepdims=True)
    return (x * lax.rsqrt(ms + eps)) * g


def _ffn(hmid, g_ref, wg_ref, wu_ref, wd_ref):
    x = _rms(hmid, g_ref[...], RMS_EPS).astype(BF16)
    acc = None
    for f0, f1 in _ffn_chunks(wg_ref.shape[1]):
        gate = jnp.dot(x, wg_ref[:, f0:f1], preferred_element_type=F32)
        up = jnp.dot(x, wu_ref[:, f0:f1], preferred_element_type=F32)
        act = ((gate * jax.nn.sigmoid(gate)) * up).astype(BF16)
        part = jnp.dot(act, wd_ref[f0:f1, :], preferred_element_type=F32)
        acc = part if acc is None else acc + part
    return hmid + acc


def _positions(tile_idx, rows):
    r = lax.broadcasted_iota(jnp.int32, (rows, 1), 0)
    return tile_idx * rows + r - (T_SEQ - N_META)


def _pool_ffn_kernel(h_ref, gm_ref, pw_ref, pb_ref, ps_ref, gf_ref, wg_ref, wu_ref, wd_ref, *rest,
                     first, final):
    rest = list(rest)
    meta_ref = rest.pop(0) if first else None
    fg_ref = rest.pop(0) if final else None
    o_ref, halo_ref = rest
    i = pl.program_id(1)
    h = h_ref[0]
    rows, d = h.shape
    if first:
        lead = jnp.concatenate([jnp.zeros((rows - N_META, d), F32), meta_ref[...]], axis=0)
        h = jnp.where(i == 0, lead, h)
    pos = _positions(i, rows)
    y = jnp.where(pos >= 0, _rms(h, gm_ref[...], RMS_EPS), 0.0)

    @pl.when(i == 0)
    def _():
        halo_ref[...] = jnp.zeros_like(halo_ref)

    ext = jnp.concatenate([halo_ref[...], y], axis=0)
    halo_ref[...] = y[rows - POOL_HALO:, :]

    group = d // len(POOL_WINDOWS)
    mixed = []
    for gi, win in enumerate(POOL_WINDOWS):
        c0 = gi * group
        s = ext[:, c0:c0 + group]
        shift = 1
        while shift < win:
            s = s + pltpu.roll(s, shift, axis=0)
            shift *= 2
        cnt = jnp.clip(pos + 1, 1, win).astype(F32)
        pooled = s[POOL_HALO:, :] / cnt - y[:, c0:c0 + group]
        mixed.append(jnp.dot(pooled.astype(BF16), pw_ref[gi], preferred_element_type=F32))
    mixed = jnp.concatenate(mixed, axis=1)
    hmid = h + (mixed + pb_ref[...]) * ps_ref[...]
    out = _ffn(hmid, gf_ref, wg_ref, wu_ref, wd_ref)
    if final:
        out = _rms(out, fg_ref[...], RMS_EPS)
    o_ref[0] = out


def _const_spec(shape):
    nd = len(shape)
    return pl.BlockSpec(shape, lambda *_: (0,) * nd, pipeline_mode=pl.Buffered(1))


def _pool_ffn(h, gm, pw, pb, ps, gf, wg, wu, wd, meta=None, final_g=None):
    first, final = meta is not None, final_g is not None
    b, _, d = h.shape
    lh = h.shape[1] + T_SEQ if first else h.shape[1]
    nt = lh // T_SEQ
    tile = pl.BlockSpec((1, T_SEQ, d), lambda bi, i: (bi, i, 0))
    frames = pl.BlockSpec((1, T_SEQ, d), lambda bi, i: (bi, jnp.maximum(i - 1, 0), 0))
    in_specs = [frames if first else tile, _const_spec(gm.shape), _const_spec(pw.shape),
                _const_spec(pb.shape), _const_spec(ps.shape), _const_spec(gf.shape),
                _const_spec(wg.shape), _const_spec(wu.shape), _const_spec(wd.shape)]
    args = [h, gm, pw, pb, ps, gf, wg, wu, wd]
    if first:
        in_specs.append(_const_spec(meta.shape))
        args.append(meta)
    if final:
        in_specs.append(_const_spec(final_g.shape))
        args.append(final_g)
        out_spec = frames
        out_shape = jax.ShapeDtypeStruct((b, lh - T_SEQ, d), F32)
    else:
        out_spec = tile
        out_shape = jax.ShapeDtypeStruct((b, lh, d), F32)
    return pl.pallas_call(
        functools.partial(_pool_ffn_kernel, first=first, final=final),
        grid=(b, nt),
        in_specs=in_specs,
        out_specs=out_spec,
        out_shape=out_shape,
        scratch_shapes=[pltpu.VMEM((POOL_HALO, d), F32)],
        compiler_params=pltpu.CompilerParams(
            dimension_semantics=("arbitrary", "arbitrary"), vmem_limit_bytes=VMEM_LIMIT_BYTES),
        name="pool_ffn_final" if final else "pool_ffn",
    )(*args)


def _qkv_kernel(h_ref, g_ref, w_ref, q_ref, k_ref, vt_ref, *, qscale):
    h = h_ref[0]
    d = h.shape[1]
    y = _rms(h, g_ref[...], RMS_EPS).astype(BF16)
    q = jnp.dot(y, w_ref[:, 0:d], preferred_element_type=F32) * qscale
    k = jnp.dot(y, w_ref[:, d:2 * d], preferred_element_type=F32)
    v = jnp.dot(y, w_ref[:, 2 * d:3 * d], preferred_element_type=F32)
    q_ref[0] = q.astype(BF16)
    k_ref[0] = k.astype(BF16)
    vt = v.T.astype(BF16)
    heads, hv, rows = vt_ref.shape[2:]
    hd = d // heads
    for hh in range(heads):
        vt_ref[0, 0, hh, 0:hd, :] = vt[hh * hd:(hh + 1) * hd, :]
        vt_ref[0, 0, hh, hd:hv, :] = jnp.ones((hv - hd, rows), BF16)


def _qkv(h, g, w, qscale):
    b, lh, d = h.shape
    nt = lh // T_SEQ
    hv = d // DA_HEADS + ONES_ROWS
    tile = pl.BlockSpec((1, T_SEQ, d), lambda bi, i: (bi, i, 0))
    return pl.pallas_call(
        functools.partial(_qkv_kernel, qscale=qscale),
        grid=(b, nt),
        in_specs=[tile, _const_spec(g.shape), _const_spec(w.shape)],
        out_specs=[tile, tile,
                   pl.BlockSpec((1, 1, DA_HEADS, hv, T_SEQ), lambda bi, i: (bi, i, 0, 0, 0))],
        out_shape=[jax.ShapeDtypeStruct((b, lh, d), BF16), jax.ShapeDtypeStruct((b, lh, d), BF16),
                   jax.ShapeDtypeStruct((b, nt, DA_HEADS, hv, T_SEQ), BF16)],
        compiler_params=pltpu.CompilerParams(
            dimension_semantics=("arbitrary", "arbitrary"), vmem_limit_bytes=VMEM_LIMIT_BYTES),
        name="qkv",
    )(h, g, w)


def _t5_bucket(rel):
    half = REL_BUCKETS // 2
    max_exact = half // 2
    offset = jnp.where(rel > 0, half, 0)
    n = jnp.abs(rel)
    log_ratio = jnp.log(jnp.maximum(n, 1).astype(F32) / max_exact) / math.log(REL_MAX_DIST / max_exact)
    large = jnp.minimum(max_exact + (log_ratio * (half - max_exact)).astype(jnp.int32), half - 1)
    return offset + jnp.where(n < max_exact, n, large)


def _bias_buckets():
    t = T_SEQ
    kl = lax.broadcasted_iota(jnp.int32, (t, t), 0)
    ql = lax.broadcasted_iota(jnp.int32, (t, t), 1)
    prev = _t5_bucket(kl - t - ql)[t - META_BLK:]
    diag = jnp.where(kl // CHUNK <= ql // CHUNK, _t5_bucket(kl - ql), -1)
    kr = lax.broadcasted_iota(jnp.int32, (META_BLK, t), 0)
    qc = lax.broadcasted_iota(jnp.int32, (META_BLK, t), 1)
    mk = kr - (META_BLK - N_META)
    mq = qc - (t - N_META)
    lead = jnp.where(mq >= 0, _t5_bucket(mk - mq), FAR_BUCKET)
    first = _t5_bucket(mk - N_META - qc)
    later = jnp.full((META_BLK, t), FAR_BUCKET, jnp.int32)
    meta = [jnp.where(mk >= 0, m, -1) for m in (lead, first, later)]
    nb = t // META_BLK
    return jnp.concatenate([prev[None], diag.reshape(nb, META_BLK, t), jnp.stack(meta)], axis=0)


def _bias_kernel(tab_ref, bk_ref, o_ref):
    hh = pl.program_id(0)
    far = tab_ref[FAR_BUCKET, hh]
    for blk in range(bk_ref.shape[0]):
        b = bk_ref[blk]
        acc = jnp.zeros(b.shape, F32)
        for t in range(REL_BUCKETS):
            acc = jnp.where(b == t, tab_ref[t, hh] - far, acc)
        o_ref[0, blk] = jnp.where(b < 0, NEG, acc * LOG2E)


def _bias_tiles(table, buckets):
    nb, r, c = buckets.shape
    heads = table.shape[1]
    return pl.pallas_call(
        _bias_kernel,
        grid=(heads,),
        in_specs=[pl.BlockSpec(memory_space=pltpu.SMEM),
                  pl.BlockSpec((nb, r, c), lambda hh: (0, 0, 0))],
        out_specs=pl.BlockSpec((1, nb, r, c), lambda hh: (hh, 0, 0, 0)),
        out_shape=jax.ShapeDtypeStruct((heads, nb, r, c), F32),
        compiler_params=pltpu.CompilerParams(dimension_semantics=("arbitrary",)),
        name="rel_bias",
    )(table, buckets)


def _attn_kernel(q_ref, k_ref, vt_ref, bias_ref, lam_ref, sg_ref, o_ref,
                 qzz_ref, m_ref, mt_ref, acc_ref, sa_ref, sb_ref, sc_ref, *, lam_init):
    lv = lam_ref[...]
    lam = (jnp.exp(jnp.sum(lv[0:1] * lv[1:2], axis=1, keepdims=True))
           - jnp.exp(jnp.sum(lv[2:3] * lv[3:4], axis=1, keepdims=True)) + lam_init)

    n_tiles = q_ref.shape[1] // T_SEQ

    def query_tile(a, carry):
        _attn_query_tile(a, n_tiles, lam, q_ref, k_ref, vt_ref, bias_ref, sg_ref, o_ref,
                         qzz_ref, m_ref, mt_ref, acc_ref, sa_ref, sb_ref, sc_ref, lam_init=lam_init)
        return carry

    _attn_query_operand(0, q_ref, qzz_ref)
    lax.fori_loop(0, n_tiles + 1, query_tile, 0)


def _attn_query_operand(a, q_ref, qzz_ref):
    t = T_SEQ
    hd = q_ref.shape[2]
    qt = q_ref[0, pl.ds(pl.multiple_of(a * t, t), t), :].astype(F32).T
    row = lax.broadcasted_iota(jnp.int32, (hd, t), 0)
    q1 = jnp.where(row < hd // 2, qt, 0.0)
    q2 = jnp.where(row >= hd // 2, qt, 0.0)
    qzz_ref[lax.rem(a, 2)] = jnp.concatenate([q1, q2], axis=1).astype(BF16)


def _attn_query_tile(a, n_tiles, lam, q_ref, k_ref, vt_ref, bias_ref, sg_ref, o_ref,
                     qzz_ref, m_ref, mt_ref, acc_ref, sa_ref, sb_ref, sc_ref, *, lam_init):
    t = T_SEQ
    nb = t // META_BLK
    slot = lax.rem(a, 2)
    bnear_ref = bias_ref

    def prepare_next():
        _attn_query_operand(jnp.minimum(a + 1, n_tiles - 1), q_ref, qzz_ref)

    def produce(j, buf):
        kt = k_ref[0, pl.ds(pl.multiple_of(j * t, t), t), :]
        mts = []
        for cs in col_chunks:
            s = jnp.dot(kt, qzz_ref[slot, :, cs], preferred_element_type=F32)
            buf[:, cs] = s
            mts.append(jnp.max(s, axis=0, keepdims=True))
        return jnp.concatenate(mts, axis=1)

    def probs(s, m):
        return jnp.exp2(s - m).astype(BF16)

    def consume(buf, mt, vt):
        for cs in col_chunks:
            m_old = m_ref[:, cs]
            m_new = jnp.maximum(m_old, mt[:, cs])
            alpha = jnp.exp2(m_old - m_new)
            acc_ref[:, cs] = alpha * acc_ref[:, cs] + jnp.dot(
                vt, probs(buf[:, cs], m_new), preferred_element_type=F32)
            m_ref[:, cs] = m_new

    def consume_diag(buf, j):
        bias = bnear_ref[0, 1:nb + 1].reshape(t, t)
        mts = []
        for cs in col_chunks:
            bc = slice(cs.start % t, cs.start % t + COL_CHUNK)
            s = buf[:, cs] + bias[:, bc]
            buf[:, cs] = s
            mts.append(jnp.max(s, axis=0, keepdims=True))
        consume(buf, jnp.concatenate(mts, axis=1), vt_ref[0, j, 0])

    def fix_prev(buf, mt):
        corner = bnear_ref[0, 0, :, 0:LANES]
        pieces, start = [], 0
        for base in (0, t):
            cs = slice(base, base + LANES)
            buf[t - META_BLK:, cs] = buf[t - META_BLK:, cs] + corner
            if base > start:
                pieces.append(mt[:, start:base])
            pieces.append(jnp.max(buf[:, cs], axis=0, keepdims=True))
            start = base + LANES
        return jnp.concatenate(pieces + [mt[:, start:]], axis=1)

    def init():
        s0 = jnp.dot(k_ref[0, t - META_BLK:t, :], qzz_ref[slot], preferred_element_type=F32)
        bmeta = bias_ref[0, nb + 1 + jnp.minimum(a, 2)]
        s0 = s0 + jnp.concatenate([bmeta, bmeta], axis=1)
        m0 = jnp.max(s0, axis=0, keepdims=True)
        m_ref[...] = m0
        acc_ref[...] = jnp.dot(vt_ref[0, 0, 0, :, t - META_BLK:t], probs(s0, m0),
                               preferred_element_type=F32)

    col_chunks = [slice(c, c + COL_CHUNK) for c in range(0, 2 * t, COL_CHUNK)]
    odd = lax.rem(a, 2)

    def finish_previous(diag=True):
        if diag:
            consume_diag(sc_ref, a - 1)
        _attn_finalize(a - 1, lam, acc_ref, sg_ref, o_ref, lam_init)

    @pl.when(a == 0)
    def _():
        init()
        prepare_next()

    @pl.when(a == 1)
    def _():
        finish_previous(diag=False)
        init()
        produce(1, sc_ref)
        prepare_next()

    @pl.when(a == n_tiles)
    def _():
        finish_previous()

    full = (a >= 2) & (a < n_tiles)

    @pl.when(full & (odd == 1))
    def _():
        mtb = produce(1, sb_ref)
        finish_previous()
        init()
        mt_ref[...] = produce(2, sa_ref)
        consume(sb_ref, mtb, vt_ref[0, 1, 0])

    @pl.when(full & (odd == 0))
    def _():
        mt_ref[...] = produce(1, sa_ref)
        finish_previous()
        init()

    @pl.when(full)
    def _():
        def pair(j, mta):
            mtb = produce(j + 1, sb_ref)
            consume(sa_ref, mta, vt_ref[0, j, 0])
            mta_next = produce(j + 2, sa_ref)
            consume(sb_ref, mtb, vt_ref[0, j + 1, 0])
            return mta_next

        n_pairs = (a - 2) // 2
        odd_pairs = lax.rem(n_pairs, 2)

        @pl.when(odd_pairs == 1)
        def _():
            mt_ref[...] = pair(1 + odd, mt_ref[...])

        n_quads = n_pairs // 2
        odd_quads = lax.rem(n_quads, 2)

        @pl.when(odd_quads == 1)
        def _():
            j = 1 + odd + 2 * odd_pairs
            mt_ref[...] = pair(j + 2, pair(j, mt_ref[...]))

        j0 = 1 + odd + 2 * odd_pairs + 4 * odd_quads

        def oct_body(i, mta):
            j = j0 + 8 * i
            for step in range(0, 8, 2):
                mta = pair(j + step, mta)
            return mta

        mta = lax.fori_loop(0, n_quads // 2, oct_body, mt_ref[...])
        produce(a, sc_ref)
        consume(sa_ref, fix_prev(sa_ref, mta), vt_ref[0, a - 1, 0])
        prepare_next()


def _attn_finalize(a, lam, acc_ref, sg_ref, o_ref, lam_init):
    t = T_SEQ
    hd = o_ref.shape[2]
    on = acc_ref[0:hd, :] * (1.0 / acc_ref[hd:hd + 1, :])
    o = on[:, :t] - lam * on[:, t:]
    o = (o * lax.rsqrt(jnp.mean(o * o, axis=0, keepdims=True) + SUBLN_EPS)) * sg_ref[...]
    o = o * (1.0 - lam_init)
    o_ref[0, pl.ds(pl.multiple_of(a * t, t), t), :] = o.T.astype(BF16)


def _attention(q, k, vt, bias, lamv, sg, lam_init):
    b, lh, d = q.shape
    nt = lh // T_SEQ
    hd = d // DA_HEADS
    hv = vt.shape[3]
    seq = pl.BlockSpec((1, lh, hd), lambda bi, hh: (bi, 0, hh))
    return pl.pallas_call(
        functools.partial(_attn_kernel, lam_init=lam_init),
        grid=(b, DA_HEADS),
        in_specs=[
            seq,
            seq,
            pl.BlockSpec((1, nt, 1, hv, T_SEQ), lambda bi, hh: (bi, 0, hh, 0, 0)),
            pl.BlockSpec((1,) + bias.shape[1:], lambda bi, hh: (hh, 0, 0, 0)),
            pl.BlockSpec(lamv.shape, lambda bi, hh: (0, 0)),
            pl.BlockSpec(sg.shape, lambda bi, hh: (0, 0)),
        ],
        out_specs=seq,
        out_shape=jax.ShapeDtypeStruct((b, lh, d), BF16),
        scratch_shapes=[pltpu.VMEM((2, hd, 2 * T_SEQ), BF16), pltpu.VMEM((1, 2 * T_SEQ), F32),
                        pltpu.VMEM((1, 2 * T_SEQ), F32), pltpu.VMEM((hv, 2 * T_SEQ), F32)]
        + [pltpu.VMEM((T_SEQ, 2 * T_SEQ), F32)] * 3,
        compiler_params=pltpu.CompilerParams(
            dimension_semantics=("arbitrary", "arbitrary"), vmem_limit_bytes=VMEM_LIMIT_BYTES),
        name="diff_attn",
    )(q, k, vt, bias, lamv, sg)


def _attn_out_ffn_kernel(h_ref, o_ref, wo_ref, gf_ref, wg_ref, wu_ref, wd_ref, out_ref):
    hmid = h_ref[...] + jnp.dot(o_ref[...], wo_ref[...], preferred_element_type=F32)
    out_ref[...] = _ffn(hmid, gf_ref, wg_ref, wu_ref, wd_ref)


def _attn_out_ffn(h, o, wo, gf, wg, wu, wd):
    b, lh, d = h.shape
    rows = b * lh
    tile_f32 = pl.BlockSpec((T_SEQ, d), lambda i: (i, 0))
    out = pl.pallas_call(
        _attn_out_ffn_kernel,
        grid=(rows // T_SEQ,),
        in_specs=[tile_f32, pl.BlockSpec((T_SEQ, d), lambda i: (i, 0)), _const_spec(wo.shape),
                  _const_spec(gf.shape), _const_spec(wg.shape), _const_spec(wu.shape),
                  _const_spec(wd.shape)],
        out_specs=tile_f32,
        out_shape=jax.ShapeDtypeStruct((rows, d), F32),
        compiler_params=pltpu.CompilerParams(
            dimension_semantics=("arbitrary",), vmem_limit_bytes=VMEM_LIMIT_BYTES),
        name="attn_out_ffn",
    )(h.reshape(rows, d), o.reshape(rows, d), wo, gf, wg, wu, wd)
    return out.reshape(b, lh, d)


def _conv_ffn_kernel(h_ref, gm_ref, win_ref, bin_ref, dww_ref, dwb_ref, lng_ref, lnb_ref,
                     wout_ref, bout_ref, gf_ref, wg_ref, wu_ref, wd_ref, o_ref, uext_ref, conv_ref):
    i = pl.program_id(1)
    h = h_ref[0]
    rows, d = h.shape
    pos = _positions(i, rows)
    y = _rms(h, gm_ref[...], RMS_EPS).astype(BF16)
    ag = jnp.dot(y, win_ref[...], preferred_element_type=F32) + bin_ref[...]
    u = ag[:, :d] * jax.nn.sigmoid(ag[:, d:])
    u = jnp.where(pos >= 0, u, 0.0)

    @pl.when(i == 0)
    def _():
        uext_ref[0:CONV_HALO, :] = jnp.zeros((CONV_HALO, d), F32)
        uext_ref[CONV_HALO + rows:, :] = jnp.zeros((SUBLANES, d), F32)

    uext_ref[CONV_HALO:CONV_HALO + rows, :] = u

    base = CONV_HALO - (CONV_WIDTH - 1)
    span = CONV_ROWS + SUBLANES
    n_oct = (base + CONV_WIDTH - 1) // SUBLANES + 1

    def conv_body(c, carry):
        r0 = pl.multiple_of(c * CONV_ROWS, CONV_ROWS)
        for lb in range(d // LANES):
            cs = slice(lb * LANES, (lb + 1) * LANES)
            um = [uext_ref[pl.ds(r0 + SUBLANES * m, span), cs] for m in range(n_oct)]
            out = None
            for sh in range(SUBLANES):
                part = None
                for m in range(n_oct):
                    j = SUBLANES * m + sh - base
                    if 0 <= j < CONV_WIDTH:
                        term = dww_ref[j:j + 1, cs] * um[m]
                        part = term if part is None else part + term
                if sh:
                    part = pltpu.roll(part, span - sh, axis=0)
                out = part if out is None else out + part
            conv_ref[pl.ds(r0, CONV_ROWS), cs] = out[:CONV_ROWS] + dwb_ref[:, cs]
        return carry

    lax.fori_loop(0, rows // CONV_ROWS, conv_body, 0)
    uext_ref[0:CONV_HALO, :] = uext_ref[rows:rows + CONV_HALO, :]

    c = conv_ref[...]
    mu = jnp.mean(c, axis=-1, keepdims=True)
    cc = c - mu
    var = jnp.mean(cc * cc, axis=-1, keepdims=True)
    z = (cc * lax.rsqrt(var + LN_EPS)) * lng_ref[...] + lnb_ref[...]
    z = (z * jax.nn.sigmoid(z)).astype(BF16)
    hmid = h + jnp.dot(z, wout_ref[...], preferred_element_type=F32) + bout_ref[...]
    o_ref[0] = _ffn(hmid, gf_ref, wg_ref, wu_ref, wd_ref)


def _conv_ffn(h, gm, win, bin_, dww, dwb, lng, lnb, wout, bout, gf, wg, wu, wd):
    b, lh, d = h.shape
    nt = lh // T_SEQ
    tile = pl.BlockSpec((1, T_SEQ, d), lambda bi, i: (bi, i, 0))
    consts = [gm, win, bin_, dww, dwb, lng, lnb, wout, bout, gf, wg, wu, wd]
    return pl.pallas_call(
        _conv_ffn_kernel,
        grid=(b, nt),
        in_specs=[tile] + [_const_spec(c.shape) for c in consts],
        out_specs=tile,
        out_shape=jax.ShapeDtypeStruct((b, lh, d), F32),
        scratch_shapes=[pltpu.VMEM((T_SEQ + CONV_HALO + SUBLANES, d), F32), pltpu.VMEM((T_SEQ, d), F32)],
        compiler_params=pltpu.CompilerParams(
            dimension_semantics=("arbitrary", "arbitrary"), vmem_limit_bytes=VMEM_LIMIT_BYTES),
        name="conv_ffn",
    )(h, *consts)


def kernel(x, meta_tokens, norm_mix_g, norm_ffn_g, final_norm_g, rel_bias_table, pool_w, pool_b, pool_scale, attn_w_qkv, attn_w_o, attn_lambda_q1, attn_lambda_k1, attn_lambda_q2, attn_lambda_k2, attn_subln_g, conv_w_in, conv_b_in, conv_dw_w, conv_dw_b, conv_ln_g, conv_ln_b, conv_w_out, conv_b_out, ffn_w_gate, ffn_w_up, ffn_w_down):
    b, seq, d = x.shape
    depth = norm_mix_g.shape[0]
    assert seq % T_SEQ == 0 and d % (LANES * len(POOL_WINDOWS)) == 0
    assert depth % N_MIXERS == 1, "the last layer must be a pooling layer (it applies the final norm)"

    row = lambda v: v.reshape(1, -1).astype(F32)
    h = x.astype(F32)
    bias = None
    for i in range(depth):
        mixer, j = i % N_MIXERS, i // N_MIXERS
        gm, gf = row(norm_mix_g[i]), row(norm_ffn_g[i])
        wg, wu, wd = (w[i].astype(BF16) for w in (ffn_w_gate, ffn_w_up, ffn_w_down))
        if mixer == 0:
            h = _pool_ffn(h, gm, pool_w[j].astype(BF16), row(pool_b[j]), row(pool_scale[j]),
                          gf, wg, wu, wd,
                          meta=meta_tokens.astype(F32) if i == 0 else None,
                          final_g=row(final_norm_g) if i == depth - 1 else None)
        elif mixer == 1:
            lam_init = 0.8 - 0.6 * math.exp(-0.3 * i)
            hd = d // DA_HEADS
            if bias is None:
                bias = _bias_tiles(rel_bias_table.astype(F32), _bias_buckets())
            qscale = (hd // 2) ** -0.5 * LOG2E
            q, k, vt = _qkv(h, gm, attn_w_qkv[j].astype(BF16), qscale)
            lamv = jnp.stack([attn_lambda_q1[j], attn_lambda_k1[j],
                              attn_lambda_q2[j], attn_lambda_k2[j]]).astype(F32)
            o = _attention(q, k, vt, bias, lamv, attn_subln_g[j].reshape(hd, 1).astype(F32), lam_init)
            h = _attn_out_ffn(h, o, attn_w_o[j].astype(BF16), gf, wg, wu, wd)
        else:
            h = _conv_ffn(h, gm, conv_w_in[j].astype(BF16), row(conv_b_in[j]),
                          conv_dw_w[j].astype(F32), row(conv_dw_b[j]), row(conv_ln_g[j]),
                          row(conv_ln_b[j]), conv_w_out[j].astype(BF16), row(conv_b_out[j]),
                          gf, wg, wu, wd)
    return h
```

```python
import functools
import math

import jax
import jax.numpy as jnp
from jax import lax
from jax.experimental import pallas as pl
from jax.experimental.pallas import tpu as pltpu

F32 = jnp.float32
BF16 = jnp.bfloat16

N_META = 16
CHUNK = 64
POOL_WINDOWS = (2, 4, 8, 16)
DA_HEADS = 8
REL_BUCKETS = 32
REL_MAX_DIST = 128
CONV_WIDTH = 31
RMS_EPS = 1e-6
LN_EPS = 1e-5
SUBLN_EPS = 1e-5
N_MIXERS = 3

LANES = 128
SUBLANES = 8
MXU_DIM = 256
VMEM_LIMIT_BYTES = 56 * 1024 * 1024

T_SEQ = 512
META_BLK = 128
ONES_ROWS = 16
COL_CHUNK = 2 * MXU_DIM
LEAD_ROWS = 32
POOL_HALO = 16
CONV_HALO = 32
CONV_ROWS = 32
NEG = -1e30
LOG2E = math.log2(math.e)
FAR_BUCKET = REL_BUCKETS // 2 - 1


def _ffn_chunks(f):
    step = 3 * MXU_DIM
    return [(s, min(s + step, f)) for s in range(0, f, step)]


def _rms(x, g, eps):
    ms = jnp.mean(x * x, axis=-1, keepdims=True)
    return (x * lax.rsqrt(ms + eps)) * g


def _ffn(hmid, g_ref, wg_ref, wu_ref, wd_ref):
    x = _rms(hmid, g_ref[...], RMS_EPS).astype(BF16)
    acc = None
    for f0, f1 in _ffn_chunks(wg_ref.shape[1]):
        gate = jnp.dot(x, wg_ref[:, f0:f1], preferred_element_type=F32)
        up = jnp.dot(x, wu_ref[:, f0:f1], preferred_element_type=F32)
        act = ((gate * jax.nn.sigmoid(gate)) * up).astype(BF16)
        part = jnp.dot(act, wd_ref[f0:f1, :], preferred_element_type=F32)
        acc = part if acc is None else acc + part
    return hmid + acc


def _positions(tile_idx, rows):
    r = lax.broadcasted_iota(jnp.int32, (rows, 1), 0)
    return tile_idx * rows + r - (T_SEQ - N_META)


def _pool_ffn_kernel(h_ref, gm_ref, pw_ref, pb_ref, ps_ref, gf_ref, wg_ref, wu_ref, wd_ref, *rest,
                     first, final):
    rest = list(rest)
    meta_ref = rest.pop(0) if first else None
    fg_ref = rest.pop(0) if final else None
    o_ref, halo_ref = rest
    i = pl.program_id(1)
    t, d = h_ref.shape[1:]

    def rows_of_tile(h, pos, halo):
        rows = h.shape[0]
        y = jnp.where(pos >= 0, _rms(h, gm_ref[...], RMS_EPS), 0.0)
        ext = jnp.concatenate([halo, y], axis=0)
        halo_ref[...] = y[rows - POOL_HALO:, :]
        group = d // len(POOL_WINDOWS)
        mixed = []
        for gi, win in enumerate(POOL_WINDOWS):
            c0 = gi * group
            s = ext[:, c0:c0 + group]
            shift = 1
            while shift < win:
                s = s + pltpu.roll(s, shift, axis=0)
                shift *= 2
            cnt = jnp.clip(pos + 1, 1, win).astype(F32)
            pooled = s[POOL_HALO:, :] / cnt - y[:, c0:c0 + group]
            mixed.append(jnp.dot(pooled.astype(BF16), pw_ref[gi], preferred_element_type=F32))
        mixed = jnp.concatenate(mixed, axis=1)
        hmid = h + (mixed + pb_ref[...]) * ps_ref[...]
        out = _ffn(hmid, gf_ref, wg_ref, wu_ref, wd_ref)
        if final:
            out = _rms(out, fg_ref[...], RMS_EPS)
        return out

    @pl.when(i == 0)
    def _():
        if first:
            h = jnp.concatenate([jnp.zeros((LEAD_ROWS - N_META, d), F32), meta_ref[...]], axis=0)
        else:
            h = h_ref[0, t - LEAD_ROWS:, :]
        pos = _positions(0, t)[t - LEAD_ROWS:]
        o_ref[0, 0:t - LEAD_ROWS, :] = jnp.zeros((t - LEAD_ROWS, d), F32)
        o_ref[0, t - LEAD_ROWS:, :] = rows_of_tile(h, pos, jnp.zeros((POOL_HALO, d), F32))

    @pl.when(i > 0)
    def _():
        o_ref[0] = rows_of_tile(h_ref[0], _positions(i, t), halo_ref[...])


def _const_spec(shape):
    nd = len(shape)
    return pl.BlockSpec(shape, lambda *_: (0,) * nd, pipeline_mode=pl.Buffered(1))


def _pool_ffn(h, gm, pw, pb, ps, gf, wg, wu, wd, meta=None, final_g=None):
    first, final = meta is not None, final_g is not None
    b, _, d = h.shape
    lh = h.shape[1] + T_SEQ if first else h.shape[1]
    nt = lh // T_SEQ
    tile = pl.BlockSpec((1, T_SEQ, d), lambda bi, i: (bi, i, 0))
    frames = pl.BlockSpec((1, T_SEQ, d), lambda bi, i: (bi, jnp.maximum(i - 1, 0), 0))
    in_specs = [frames if first else tile, _const_spec(gm.shape), _const_spec(pw.shape),
                _const_spec(pb.shape), _const_spec(ps.shape), _const_spec(gf.shape),
                _const_spec(wg.shape), _const_spec(wu.shape), _const_spec(wd.shape)]
    args = [h, gm, pw, pb, ps, gf, wg, wu, wd]
    if first:
        in_specs.append(_const_spec(meta.shape))
        args.append(meta)
    if final:
        in_specs.append(_const_spec(final_g.shape))
        args.append(final_g)
        out_spec = frames
        out_shape = jax.ShapeDtypeStruct((b, lh - T_SEQ, d), F32)
    else:
        out_spec = tile
        out_shape = jax.ShapeDtypeStruct((b, lh, d), F32)
    return pl.pallas_call(
        functools.partial(_pool_ffn_kernel, first=first, final=final),
        grid=(b, nt),
        in_specs=in_specs,
        out_specs=out_spec,
        out_shape=out_shape,
        scratch_shapes=[pltpu.VMEM((POOL_HALO, d), F32)],
        compiler_params=pltpu.CompilerParams(
            dimension_semantics=("arbitrary", "arbitrary"), vmem_limit_bytes=VMEM_LIMIT_BYTES),
        name="pool_ffn_final" if final else "pool_ffn",
    )(*args)


def _qkv_kernel(h_ref, g_ref, w_ref, q_ref, k_ref, vt_ref, *, qscale):
    h = h_ref[0]
    d = h.shape[1]
    y = _rms(h, g_ref[...], RMS_EPS).astype(BF16)
    q = jnp.dot(y, w_ref[:, 0:d], preferred_element_type=F32) * qscale
    k = jnp.dot(y, w_ref[:, d:2 * d], preferred_element_type=F32)
    v = jnp.dot(y, w_ref[:, 2 * d:3 * d], preferred_element_type=F32)
    q_ref[0] = q.astype(BF16)
    k_ref[0] = k.astype(BF16)
    vt = v.T.astype(BF16)
    heads, hv, rows = vt_ref.shape[2:]
    hd = d // heads
    for hh in range(heads):
        vt_ref[0, 0, hh, 0:hd, :] = vt[hh * hd:(hh + 1) * hd, :]
        vt_ref[0, 0, hh, hd:hv, :] = jnp.ones((hv - hd, rows), BF16)


def _qkv(h, g, w, qscale):
    b, lh, d = h.shape
    nt = lh // T_SEQ
    hv = d // DA_HEADS + ONES_ROWS
    tile = pl.BlockSpec((1, T_SEQ, d), lambda bi, i: (bi, i, 0))
    return pl.pallas_call(
        functools.partial(_qkv_kernel, qscale=qscale),
        grid=(b, nt),
        in_specs=[tile, _const_spec(g.shape), _const_spec(w.shape)],
        out_specs=[tile, tile,
                   pl.BlockSpec((1, 1, DA_HEADS, hv, T_SEQ), lambda bi, i: (bi, i, 0, 0, 0))],
        out_shape=[jax.ShapeDtypeStruct((b, lh, d), BF16), jax.ShapeDtypeStruct((b, lh, d), BF16),
                   jax.ShapeDtypeStruct((b, nt, DA_HEADS, hv, T_SEQ), BF16)],
        compiler_params=pltpu.CompilerParams(
            dimension_semantics=("arbitrary", "arbitrary"), vmem_limit_bytes=VMEM_LIMIT_BYTES),
        name="qkv",
    )(h, g, w)


def _t5_bucket(rel):
    half = REL_BUCKETS // 2
    max_exact = half // 2
    offset = jnp.where(rel > 0, half, 0)
    n = jnp.abs(rel)
    log_ratio = jnp.log(jnp.maximum(n, 1).astype(F32) / max_exact) / math.log(REL_MAX_DIST / max_exact)
    large = jnp.minimum(max_exact + (log_ratio * (half - max_exact)).astype(jnp.int32), half - 1)
    return offset + jnp.where(n < max_exact, n, large)


def _bias_buckets():
    t = T_SEQ
    kl = lax.broadcasted_iota(jnp.int32, (t, t), 0)
    ql = lax.broadcasted_iota(jnp.int32, (t, t), 1)
    prev = _t5_bucket(kl - t - ql)[t - META_BLK:]
    diag = jnp.where(kl // CHUNK <= ql // CHUNK, _t5_bucket(kl - ql), -1)
    kr = lax.broadcasted_iota(jnp.int32, (META_BLK, t), 0)
    qc = lax.broadcasted_iota(jnp.int32, (META_BLK, t), 1)
    mk = kr - (META_BLK - N_META)
    mq = qc - (t - N_META)
    lead = jnp.where(mq >= 0, _t5_bucket(mk - mq), FAR_BUCKET)
    first = _t5_bucket(mk - N_META - qc)
    later = jnp.full((META_BLK, t), FAR_BUCKET, jnp.int32)
    meta = [jnp.where(mk >= 0, m, -1) for m in (lead, first, later)]
    nb = t // META_BLK
    return jnp.concatenate([prev[None], diag.reshape(nb, META_BLK, t), jnp.stack(meta)], axis=0)


def _bias_kernel(tab_ref, bk_ref, o_ref):
    hh = pl.program_id(0)
    far = tab_ref[FAR_BUCKET, hh]
    for blk in range(bk_ref.shape[0]):
        b = bk_ref[blk]
        acc = jnp.zeros(b.shape, F32)
        for t in range(REL_BUCKETS):
            acc = jnp.where(b == t, tab_ref[t, hh] - far, acc)
        o_ref[0, blk] = jnp.where(b < 0, NEG, acc * LOG2E)


def _bias_tiles(table, buckets):
    nb, r, c = buckets.shape
    heads = table.shape[1]
    return pl.pallas_call(
        _bias_kernel,
        grid=(heads,),
        in_specs=[pl.BlockSpec(memory_space=pltpu.SMEM),
                  pl.BlockSpec((nb, r, c), lambda hh: (0, 0, 0))],
        out_specs=pl.BlockSpec((1, nb, r, c), lambda hh: (hh, 0, 0, 0)),
        out_shape=jax.ShapeDtypeStruct((heads, nb, r, c), F32),
        compiler_params=pltpu.CompilerParams(dimension_semantics=("arbitrary",)),
        name="rel_bias",
    )(table, buckets)


def _attn_kernel(q_ref, k_ref, vt_ref, bias_ref, lam_ref, sg_ref, o_ref,
                 qzz_ref, m_ref, mt_ref, acc_ref, sa_ref, sb_ref, *, lam_init):
    lv = lam_ref[...]
    lam = (jnp.exp(jnp.sum(lv[0:1] * lv[1:2], axis=1, keepdims=True))
           - jnp.exp(jnp.sum(lv[2:3] * lv[3:4], axis=1, keepdims=True)) + lam_init)

    def query_tile(a, carry):
        _attn_query_tile(a, lam, q_ref, k_ref, vt_ref, bias_ref, sg_ref, o_ref,
                         qzz_ref, m_ref, mt_ref, acc_ref, sa_ref, sb_ref, lam_init=lam_init)
        return carry

    n_tiles = q_ref.shape[1] // T_SEQ
    lax.fori_loop(0, n_tiles, query_tile, 0)
    _attn_finalize(n_tiles - 1, lam, acc_ref, sg_ref, o_ref, lam_init)


def _attn_query_tile(a, lam, q_ref, k_ref, vt_ref, bias_ref, sg_ref, o_ref,
                     qzz_ref, m_ref, mt_ref, acc_ref, sa_ref, sb_ref, *, lam_init):
    t = T_SEQ
    hd = q_ref.shape[2]
    nb = t // META_BLK
    rows = pl.ds(pl.multiple_of(a * t, t), t)
    bnear_ref = bias_ref

    qt = q_ref[0, rows, :].astype(F32).T
    row = lax.broadcasted_iota(jnp.int32, (hd, t), 0)
    q1 = jnp.where(row < hd // 2, qt, 0.0)
    q2 = jnp.where(row >= hd // 2, qt, 0.0)
    qzz_ref[...] = jnp.concatenate([q1, q2], axis=1).astype(BF16)

    def produce(j, buf):
        kt = k_ref[0, pl.ds(pl.multiple_of(j * t, t), t), :]
        mts = []
        for cs in col_chunks:
            s = jnp.dot(kt, qzz_ref[:, cs], preferred_element_type=F32)
            buf[:, cs] = s
            mts.append(jnp.max(s, axis=0, keepdims=True))
        return jnp.concatenate(mts, axis=1)

    def probs(s, m):
        return jnp.exp2(s - m).astype(BF16)

    def consume(buf, mt, vt):
        for cs in col_chunks:
            m_old = m_ref[:, cs]
            m_new = jnp.maximum(m_old, mt[:, cs])
            alpha = jnp.exp2(m_old - m_new)
            acc_ref[:, cs] = alpha * acc_ref[:, cs] + jnp.dot(
                vt, probs(buf[:, cs], m_new), preferred_element_type=F32)
            m_ref[:, cs] = m_new

    def consume_diag(buf, j):
        bias = bnear_ref[0, 1:nb + 1].reshape(t, t)
        mts = []
        for cs in col_chunks:
            bc = slice(cs.start % t, cs.start % t + COL_CHUNK)
            s = buf[:, cs] + bias[:, bc]
            buf[:, cs] = s
            mts.append(jnp.max(s, axis=0, keepdims=True))
        consume(buf, jnp.concatenate(mts, axis=1), vt_ref[0, j, 0])

    def fix_prev(buf, mt):
        corner = bnear_ref[0, 0, :, 0:LANES]
        pieces, start = [], 0
        for base in (0, t):
            cs = slice(base, base + LANES)
            buf[t - META_BLK:, cs] = buf[t - META_BLK:, cs] + corner
            if base > start:
                pieces.append(mt[:, start:base])
            pieces.append(jnp.max(buf[:, cs], axis=0, keepdims=True))
            start = base + LANES
        return jnp.concatenate(pieces + [mt[:, start:]], axis=1)

    def init():
        s0 = jnp.dot(k_ref[0, t - META_BLK:t, :], qzz_ref[...], preferred_element_type=F32)
        bmeta = bias_ref[0, nb + 1 + jnp.minimum(a, 2)]
        s0 = s0 + jnp.concatenate([bmeta, bmeta], axis=1)
        m0 = jnp.max(s0, axis=0, keepdims=True)
        m_ref[...] = m0
        acc_ref[...] = jnp.dot(vt_ref[0, 0, 0, :, t - META_BLK:t], probs(s0, m0),
                               preferred_element_type=F32)

    col_chunks = [slice(c, c + COL_CHUNK) for c in range(0, 2 * t, COL_CHUNK)]
    odd = lax.rem(a, 2)

    def finish_previous():
        _attn_finalize(a - 1, lam, acc_ref, sg_ref, o_ref, lam_init)

    @pl.when(a == 0)
    def _():
        init()

    @pl.when(a == 1)
    def _():
        finish_previous()
        init()
        produce(1, sb_ref)
        consume_diag(sb_ref, 1)

    @pl.when((a >= 2) & (odd == 1))
    def _():
        finish_previous()
        init()
        mtb = produce(1, sb_ref)
        mt_ref[...] = produce(2, sa_ref)
        consume(sb_ref, mtb, vt_ref[0, 1, 0])

    @pl.when((a >= 2) & (odd == 0))
    def _():
        finish_previous()
        init()
        mt_ref[...] = produce(1, sa_ref)

    @pl.when(a >= 2)
    def _():
        def pair(j, mta):
            mtb = produce(j + 1, sb_ref)
            consume(sa_ref, mta, vt_ref[0, j, 0])
            mta_next = produce(j + 2, sa_ref)
            consume(sb_ref, mtb, vt_ref[0, j + 1, 0])
            return mta_next

        n_pairs = (a - 2) // 2
        odd_pairs = lax.rem(n_pairs, 2)

        @pl.when(odd_pairs == 1)
        def _():
            mt_ref[...] = pair(1 + odd, mt_ref[...])

        n_quads = n_pairs // 2
        odd_quads = lax.rem(n_quads, 2)

        @pl.when(odd_quads == 1)
        def _():
            j = 1 + odd + 2 * odd_pairs
            mt_ref[...] = pair(j + 2, pair(j, mt_ref[...]))

        j0 = 1 + odd + 2 * odd_pairs + 4 * odd_quads

        def oct_body(i, mta):
            j = j0 + 8 * i
            for step in range(0, 8, 2):
                mta = pair(j + step, mta)
            return mta

        mta = lax.fori_loop(0, n_quads // 2, oct_body, mt_ref[...])
        produce(a, sb_ref)
        consume(sa_ref, fix_prev(sa_ref, mta), vt_ref[0, a - 1, 0])
        consume_diag(sb_ref, a)


def _attn_finalize(a, lam, acc_ref, sg_ref, o_ref, lam_init):
    t = T_SEQ
    hd = o_ref.shape[2]
    on = acc_ref[0:hd, :] * (1.0 / acc_ref[hd:hd + 1, :])
    o = on[:, :t] - lam * on[:, t:]
    o = (o * lax.rsqrt(jnp.mean(o * o, axis=0, keepdims=True) + SUBLN_EPS)) * sg_ref[...]
    o = o * (1.0 - lam_init)
    o_ref[0, pl.ds(pl.multiple_of(a * t, t), t), :] = o.T.astype(BF16)


def _attention(q, k, vt, bias, lamv, sg, lam_init):
    b, lh, d = q.shape
    nt = lh // T_SEQ
    hd = d // DA_HEADS
    hv = vt.shape[3]
    seq = pl.BlockSpec((1, lh, hd), lambda bi, hh: (bi, 0, hh))
    return pl.pallas_call(
        functools.partial(_attn_kernel, lam_init=lam_init),
        grid=(b, DA_HEADS),
        in_specs=[
            seq,
            seq,
            pl.BlockSpec((1, nt, 1, hv, T_SEQ), lambda bi, hh: (bi, 0, hh, 0, 0)),
            pl.BlockSpec((1,) + bias.shape[1:], lambda bi, hh: (hh, 0, 0, 0)),
            pl.BlockSpec(lamv.shape, lambda bi, hh: (0, 0)),
            pl.BlockSpec(sg.shape, lambda bi, hh: (0, 0)),
        ],
        out_specs=seq,
        out_shape=jax.ShapeDtypeStruct((b, lh, d), BF16),
        scratch_shapes=[pltpu.VMEM((hd, 2 * T_SEQ), BF16), pltpu.VMEM((1, 2 * T_SEQ), F32),
                        pltpu.VMEM((1, 2 * T_SEQ), F32), pltpu.VMEM((hv, 2 * T_SEQ), F32),
                        pltpu.VMEM((T_SEQ, 2 * T_SEQ), F32), pltpu.VMEM((T_SEQ, 2 * T_SEQ), F32)],
        compiler_params=pltpu.CompilerParams(
            dimension_semantics=("arbitrary", "arbitrary"), vmem_limit_bytes=VMEM_LIMIT_BYTES),
        name="diff_attn",
    )(q, k, vt, bias, lamv, sg)


def _attn_out_ffn_kernel(h_ref, o_ref, wo_ref, gf_ref, wg_ref, wu_ref, wd_ref, out_ref, *,
                         tiles_per_seq):
    t, d = h_ref.shape

    def rows_of_tile(h, o):
        hmid = h + jnp.dot(o, wo_ref[...], preferred_element_type=F32)
        return _ffn(hmid, gf_ref, wg_ref, wu_ref, wd_ref)

    lead = lax.rem(pl.program_id(0), tiles_per_seq) == 0

    @pl.when(lead)
    def _():
        out_ref[0:t - LEAD_ROWS, :] = jnp.zeros((t - LEAD_ROWS, d), F32)
        out_ref[t - LEAD_ROWS:, :] = rows_of_tile(h_ref[t - LEAD_ROWS:, :], o_ref[t - LEAD_ROWS:, :])

    @pl.when(jnp.logical_not(lead))
    def _():
        out_ref[...] = rows_of_tile(h_ref[...], o_ref[...])


def _attn_out_ffn(h, o, wo, gf, wg, wu, wd):
    b, lh, d = h.shape
    rows = b * lh
    tile_f32 = pl.BlockSpec((T_SEQ, d), lambda i: (i, 0))
    out = pl.pallas_call(
        functools.partial(_attn_out_ffn_kernel, tiles_per_seq=lh // T_SEQ),
        grid=(rows // T_SEQ,),
        in_specs=[tile_f32, pl.BlockSpec((T_SEQ, d), lambda i: (i, 0)), _const_spec(wo.shape),
                  _const_spec(gf.shape), _const_spec(wg.shape), _const_spec(wu.shape),
                  _const_spec(wd.shape)],
        out_specs=tile_f32,
        out_shape=jax.ShapeDtypeStruct((rows, d), F32),
        compiler_params=pltpu.CompilerParams(
            dimension_semantics=("arbitrary",), vmem_limit_bytes=VMEM_LIMIT_BYTES),
        name="attn_out_ffn",
    )(h.reshape(rows, d), o.reshape(rows, d), wo, gf, wg, wu, wd)
    return out.reshape(b, lh, d)


def _conv_ffn_kernel(h_ref, gm_ref, win_ref, bin_ref, dww_ref, dwb_ref, lng_ref, lnb_ref,
                     wout_ref, bout_ref, gf_ref, wg_ref, wu_ref, wd_ref, o_ref, uext_ref, conv_ref):
    i = pl.program_id(1)
    t, d = h_ref.shape[1:]

    @pl.when(i == 0)
    def _():
        uext_ref[0:CONV_HALO, :] = jnp.zeros((CONV_HALO, d), F32)
        for rows in (LEAD_ROWS, t):
            uext_ref[CONV_HALO + rows:CONV_HALO + rows + SUBLANES, :] = jnp.zeros((SUBLANES, d), F32)
        o_ref[0, 0:t - LEAD_ROWS, :] = jnp.zeros((t - LEAD_ROWS, d), F32)
        o_ref[0, t - LEAD_ROWS:, :] = _conv_ffn_rows(
            h_ref[0, t - LEAD_ROWS:, :], _positions(0, t)[t - LEAD_ROWS:], gm_ref, win_ref, bin_ref,
            dww_ref, dwb_ref, lng_ref, lnb_ref, wout_ref, bout_ref, gf_ref, wg_ref, wu_ref, wd_ref,
            uext_ref, conv_ref)

    @pl.when(i > 0)
    def _():
        o_ref[0] = _conv_ffn_rows(
            h_ref[0], _positions(i, t), gm_ref, win_ref, bin_ref, dww_ref, dwb_ref, lng_ref, lnb_ref,
            wout_ref, bout_ref, gf_ref, wg_ref, wu_ref, wd_ref, uext_ref, conv_ref)


def _conv_ffn_rows(h, pos, gm_ref, win_ref, bin_ref, dww_ref, dwb_ref, lng_ref, lnb_ref,
                   wout_ref, bout_ref, gf_ref, wg_ref, wu_ref, wd_ref, uext_ref, conv_ref):
    rows, d = h.shape
    y = _rms(h, gm_ref[...], RMS_EPS).astype(BF16)
    ag = jnp.dot(y, win_ref[...], preferred_element_type=F32) + bin_ref[...]
    u = ag[:, :d] * jax.nn.sigmoid(ag[:, d:])
    u = jnp.where(pos >= 0, u, 0.0)
    uext_ref[CONV_HALO:CONV_HALO + rows, :] = u

    base = CONV_HALO - (CONV_WIDTH - 1)
    span = CONV_ROWS + SUBLANES
    n_oct = (base + CONV_WIDTH - 1) // SUBLANES + 1

    def conv_body(c, carry):
        r0 = pl.multiple_of(c * CONV_ROWS, CONV_ROWS)
        for lb in range(d // LANES):
            cs = slice(lb * LANES, (lb + 1) * LANES)
            um = [uext_ref[pl.ds(r0 + SUBLANES * m, span), cs] for m in range(n_oct)]
            out = None
            for sh in range(SUBLANES):
                part = None
                for m in range(n_oct):
                    j = SUBLANES * m + sh - base
                    if 0 <= j < CONV_WIDTH:
                        term = dww_ref[j:j + 1, cs] * um[m]
                        part = term if part is None else part + term
                if sh:
                    part = pltpu.roll(part, span - sh, axis=0)
                out = part if out is None else out + part
            conv_ref[pl.ds(r0, CONV_ROWS), cs] = out[:CONV_ROWS] + dwb_ref[:, cs]
        return carry

    lax.fori_loop(0, rows // CONV_ROWS, conv_body, 0)
    uext_ref[0:CONV_HALO, :] = uext_ref[rows:rows + CONV_HALO, :]

    c = conv_ref[0:rows, :]
    mu = jnp.mean(c, axis=-1, keepdims=True)
    cc = c - mu
    var = jnp.mean(cc * cc, axis=-1, keepdims=True)
    z = (cc * lax.rsqrt(var + LN_EPS)) * lng_ref[...] + lnb_ref[...]
    z = (z * jax.nn.sigmoid(z)).astype(BF16)
    hmid = h + jnp.dot(z, wout_ref[...], preferred_element_type=F32) + bout_ref[...]
    return _ffn(hmid, gf_ref, wg_ref, wu_ref, wd_ref)


def _conv_ffn(h, gm, win, bin_, dww, dwb, lng, lnb, wout, bout, gf, wg, wu, wd):
    b, lh, d = h.shape
    nt = lh // T_SEQ
    tile = pl.BlockSpec((1, T_SEQ, d), lambda bi, i: (bi, i, 0))
    consts = [gm, win, bin_, dww, dwb, lng, lnb, wout, bout, gf, wg, wu, wd]
    return pl.pallas_call(
        _conv_ffn_kernel,
        grid=(b, nt),
        in_specs=[tile] + [_const_spec(c.shape) for c in consts],
        out_specs=tile,
        out_shape=jax.ShapeDtypeStruct((b, lh, d), F32),
        scratch_shapes=[pltpu.VMEM((T_SEQ + CONV_HALO + SUBLANES, d), F32), pltpu.VMEM((T_SEQ, d), F32)],
        compiler_params=pltpu.CompilerParams(
            dimension_semantics=("arbitrary", "arbitrary"), vmem_limit_bytes=VMEM_LIMIT_BYTES),
        name="conv_ffn",
    )(h, *consts)


def kernel(x, meta_tokens, norm_mix_g, norm_ffn_g, final_norm_g, rel_bias_table, pool_w, pool_b, pool_scale, attn_w_qkv, attn_w_o, attn_lambda_q1, attn_lambda_k1, attn_lambda_q2, attn_lambda_k2, attn_subln_g, conv_w_in, conv_b_in, conv_dw_w, conv_dw_b, conv_ln_g, conv_ln_b, conv_w_out, conv_b_out, ffn_w_gate, ffn_w_up, ffn_w_down):
    b, seq, d = x.shape
    depth = norm_mix_g.shape[0]
    assert seq % T_SEQ == 0 and d % (LANES * len(POOL_WINDOWS)) == 0
    assert depth % N_MIXERS == 1, "the last layer must be a pooling layer (it applies the final norm)"

    row = lambda v: v.reshape(1, -1).astype(F32)
    h = x.astype(F32)
    bias = None
    for i in range(depth):
        mixer, j = i % N_MIXERS, i // N_MIXERS
        gm, gf = row(norm_mix_g[i]), row(norm_ffn_g[i])
        wg, wu, wd = (w[i].astype(BF16) for w in (ffn_w_gate, ffn_w_up, ffn_w_down))
        if mixer == 0:
            h = _pool_ffn(h, gm, pool_w[j].astype(BF16), row(pool_b[j]), row(pool_scale[j]),
                          gf, wg, wu, wd,
                          meta=meta_tokens.astype(F32) if i == 0 else None,
                          final_g=row(final_norm_g) if i == depth - 1 else None)
        elif mixer == 1:
            lam_init = 0.8 - 0.6 * math.exp(-0.3 * i)
            hd = d // DA_HEADS
            if bias is None:
                bias = _bias_tiles(rel_bias_table.astype(F32), _bias_buckets())
            qscale = (hd // 2) ** -0.5 * LOG2E
            q, k, vt = _qkv(h, gm, attn_w_qkv[j].astype(BF16), qscale)
            lamv = jnp.stack([attn_lambda_q1[j], attn_lambda_k1[j],
                              attn_lambda_q2[j], attn_lambda_k2[j]]).astype(F32)
            o = _attention(q, k, vt, bias, lamv, attn_subln_g[j].reshape(hd, 1).astype(F32), lam_init)
            h = _attn_out_ffn(h, o, attn_w_o[j].astype(BF16), gf, wg, wu, wd)
        else:
            h = _conv_ffn(h, gm, conv_w_in[j].astype(BF16), row(conv_b_in[j]),
                          conv_dw_w[j].astype(F32), row(conv_dw_b[j]), row(conv_ln_g[j]),
                          row(conv_ln_b[j]), conv_w_out[j].astype(BF16), row(conv_b_out[j]),
                          gf, wg, wu, wd)
    return h
```

```python
import functools
import math

import jax
import jax.numpy as jnp
from jax import lax
from jax.experimental import pallas as pl
from jax.experimental.pallas import tpu as pltpu

F32 = jnp.float32
BF16 = jnp.bfloat16

N_META = 16
CHUNK = 64
POOL_WINDOWS = (2, 4, 8, 16)
DA_HEADS = 8
REL_BUCKETS = 32
REL_MAX_DIST = 128
CONV_WIDTH = 31
RMS_EPS = 1e-6
LN_EPS = 1e-5
SUBLN_EPS = 1e-5
N_MIXERS = 3

LANES = 128
SUBLANES = 8
MXU_DIM = 256
VMEM_LIMIT_BYTES = 56 * 1024 * 1024

T_SEQ = 512
META_BLK = 128
ONES_ROWS = 16
COL_CHUNK = 2 * MXU_DIM
LEAD_ROWS = 32
POOL_HALO = 16
CONV_HALO = 32
CONV_ROWS = 32
NEG = -1e30
LOG2E = math.log2(math.e)
FAR_BUCKET = REL_BUCKETS // 2 - 1


def _ffn_chunks(f):
    step = 3 * MXU_DIM
    return [(s, min(s + step, f)) for s in range(0, f, step)]


def _rms(x, g, eps):
    ms = jnp.mean(x * x, axis=-1, keepdims=True)
    return (x * lax.rsqrt(ms + eps)) * g


def _ffn(hmid, g_ref, wg_ref, wu_ref, wd_ref):
    x = _rms(hmid, g_ref[...], RMS_EPS).astype(BF16)
    acc = None
    for f0, f1 in _ffn_chunks(wg_ref.shape[1]):
        gate = jnp.dot(x, wg_ref[:, f0:f1], preferred_element_type=F32)
        up = jnp.dot(x, wu_ref[:, f0:f1], preferred_element_type=F32)
        act = ((gate * jax.nn.sigmoid(gate)) * up).astype(BF16)
        part = jnp.dot(act, wd_ref[f0:f1, :], preferred_element_type=F32)
        acc = part if acc is None else acc + part
    return hmid + acc


def _positions(tile_idx, rows):
    r = lax.broadcasted_iota(jnp.int32, (rows, 1), 0)
    return tile_idx * rows + r - (T_SEQ - N_META)


def _pool_ffn_kernel(h_ref, gm_ref, pw_ref, pb_ref, ps_ref, gf_ref, wg_ref, wu_ref, wd_ref, *rest,
                     first, final):
    rest = list(rest)
    meta_ref = rest.pop(0) if first else None
    fg_ref = rest.pop(0) if final else None
    o_ref, halo_ref = rest
    i = pl.program_id(1)
    t, d = h_ref.shape[1:]

    def rows_of_tile(h, pos, halo):
        rows = h.shape[0]
        y = jnp.where(pos >= 0, _rms(h, gm_ref[...], RMS_EPS), 0.0)
        ext = jnp.concatenate([halo, y], axis=0)
        halo_ref[...] = y[rows - POOL_HALO:, :]
        group = d // len(POOL_WINDOWS)
        mixed = []
        for gi, win in enumerate(POOL_WINDOWS):
            c0 = gi * group
            s = ext[:, c0:c0 + group]
            shift = 1
            while shift < win:
                s = s + pltpu.roll(s, shift, axis=0)
                shift *= 2
            cnt = jnp.clip(pos + 1, 1, win).astype(F32)
            pooled = s[POOL_HALO:, :] / cnt - y[:, c0:c0 + group]
            mixed.append(jnp.dot(pooled.astype(BF16), pw_ref[gi], preferred_element_type=F32))
        mixed = jnp.concatenate(mixed, axis=1)
        hmid = h + (mixed + pb_ref[...]) * ps_ref[...]
        out = _ffn(hmid, gf_ref, wg_ref, wu_ref, wd_ref)
        if final:
            out = _rms(out, fg_ref[...], RMS_EPS)
        return out

    @pl.when(i == 0)
    def _():
        if first:
            h = jnp.concatenate([jnp.zeros((LEAD_ROWS - N_META, d), F32), meta_ref[...]], axis=0)
        else:
            h = h_ref[0, t - LEAD_ROWS:, :]
        pos = _positions(0, t)[t - LEAD_ROWS:]
        o_ref[0, 0:t - LEAD_ROWS, :] = jnp.zeros((t - LEAD_ROWS, d), F32)
        o_ref[0, t - LEAD_ROWS:, :] = rows_of_tile(h, pos, jnp.zeros((POOL_HALO, d), F32))

    @pl.when(i > 0)
    def _():
        o_ref[0] = rows_of_tile(h_ref[0], _positions(i, t), halo_ref[...])


def _const_spec(shape):
    nd = len(shape)
    return pl.BlockSpec(shape, lambda *_: (0,) * nd, pipeline_mode=pl.Buffered(1))


def _layer_spec(stack, layer):
    nd = stack.ndim - 1
    return pl.BlockSpec((None,) + stack.shape[1:], lambda *_: (layer,) + (0,) * nd,
                        pipeline_mode=pl.Buffered(1))


def _pool_ffn(h, gm, pw, pb, ps, gf, wg, wu, wd, meta=None, final_g=None):
    first, final = meta is not None, final_g is not None
    b, _, d = h.shape
    lh = h.shape[1] + T_SEQ if first else h.shape[1]
    nt = lh // T_SEQ
    tile = pl.BlockSpec((1, T_SEQ, d), lambda bi, i: (bi, i, 0))
    frames = pl.BlockSpec((1, T_SEQ, d), lambda bi, i: (bi, jnp.maximum(i - 1, 0), 0))
    in_specs = [frames if first else tile, _const_spec(gm.shape), _const_spec(pw.shape),
                _const_spec(pb.shape), _const_spec(ps.shape), _const_spec(gf.shape),
                _layer_spec(*wg), _layer_spec(*wu), _layer_spec(*wd)]
    args = [h, gm, pw, pb, ps, gf, wg[0], wu[0], wd[0]]
    if first:
        in_specs.append(_const_spec(meta.shape))
        args.append(meta)
    if final:
        in_specs.append(_const_spec(final_g.shape))
        args.append(final_g)
        out_spec = frames
        out_shape = jax.ShapeDtypeStruct((b, lh - T_SEQ, d), F32)
    else:
        out_spec = tile
        out_shape = jax.ShapeDtypeStruct((b, lh, d), F32)
    return pl.pallas_call(
        functools.partial(_pool_ffn_kernel, first=first, final=final),
        grid=(b, nt),
        in_specs=in_specs,
        out_specs=out_spec,
        out_shape=out_shape,
        scratch_shapes=[pltpu.VMEM((POOL_HALO, d), F32)],
        compiler_params=pltpu.CompilerParams(
            dimension_semantics=("arbitrary", "arbitrary"), vmem_limit_bytes=VMEM_LIMIT_BYTES),
        name="pool_ffn_final" if final else "pool_ffn",
    )(*args)


def _qkv_kernel(h_ref, g_ref, w_ref, q_ref, k_ref, vt_ref, *, qscale):
    h = h_ref[0]
    d = h.shape[1]
    y = _rms(h, g_ref[...], RMS_EPS).astype(BF16)
    q = jnp.dot(y, w_ref[:, 0:d], preferred_element_type=F32) * qscale
    k = jnp.dot(y, w_ref[:, d:2 * d], preferred_element_type=F32)
    v = jnp.dot(y, w_ref[:, 2 * d:3 * d], preferred_element_type=F32)
    q_ref[0] = q.astype(BF16)
    k_ref[0] = k.astype(BF16)
    vt = v.T.astype(BF16)
    heads, hv, rows = vt_ref.shape[2:]
    hd = d // heads
    for hh in range(heads):
        vt_ref[0, 0, hh, 0:hd, :] = vt[hh * hd:(hh + 1) * hd, :]
        vt_ref[0, 0, hh, hd:hv, :] = jnp.ones((hv - hd, rows), BF16)


def _qkv(h, g, w, qscale):
    b, lh, d = h.shape
    nt = lh // T_SEQ
    hv = d // DA_HEADS + ONES_ROWS
    tile = pl.BlockSpec((1, T_SEQ, d), lambda bi, i: (bi, i, 0))
    return pl.pallas_call(
        functools.partial(_qkv_kernel, qscale=qscale),
        grid=(b, nt),
        in_specs=[tile, _const_spec(g.shape), _const_spec(w.shape)],
        out_specs=[tile, tile,
                   pl.BlockSpec((1, 1, DA_HEADS, hv, T_SEQ), lambda bi, i: (bi, i, 0, 0, 0))],
        out_shape=[jax.ShapeDtypeStruct((b, lh, d), BF16), jax.ShapeDtypeStruct((b, lh, d), BF16),
                   jax.ShapeDtypeStruct((b, nt, DA_HEADS, hv, T_SEQ), BF16)],
        compiler_params=pltpu.CompilerParams(
            dimension_semantics=("arbitrary", "arbitrary"), vmem_limit_bytes=VMEM_LIMIT_BYTES),
        name="qkv",
    )(h, g, w)


def _t5_bucket(rel):
    half = REL_BUCKETS // 2
    max_exact = half // 2
    offset = jnp.where(rel > 0, half, 0)
    n = jnp.abs(rel)
    log_ratio = jnp.log(jnp.maximum(n, 1).astype(F32) / max_exact) / math.log(REL_MAX_DIST / max_exact)
    large = jnp.minimum(max_exact + (log_ratio * (half - max_exact)).astype(jnp.int32), half - 1)
    return offset + jnp.where(n < max_exact, n, large)


def _bias_buckets():
    t = T_SEQ
    kl = lax.broadcasted_iota(jnp.int32, (t, t), 0)
    ql = lax.broadcasted_iota(jnp.int32, (t, t), 1)
    prev = _t5_bucket(kl - t - ql)[t - META_BLK:]
    diag = jnp.where(kl // CHUNK <= ql // CHUNK, _t5_bucket(kl - ql), -1)
    kr = lax.broadcasted_iota(jnp.int32, (META_BLK, t), 0)
    qc = lax.broadcasted_iota(jnp.int32, (META_BLK, t), 1)
    mk = kr - (META_BLK - N_META)
    mq = qc - (t - N_META)
    lead = jnp.where(mq >= 0, _t5_bucket(mk - mq), FAR_BUCKET)
    first = _t5_bucket(mk - N_META - qc)
    later = jnp.full((META_BLK, t), FAR_BUCKET, jnp.int32)
    meta = [jnp.where(mk >= 0, m, -1) for m in (lead, first, later)]
    nb = t // META_BLK
    return jnp.concatenate([prev[None], diag.reshape(nb, META_BLK, t), jnp.stack(meta)], axis=0)


def _bias_kernel(tab_ref, bk_ref, o_ref):
    hh = pl.program_id(0)
    far = tab_ref[FAR_BUCKET, hh]
    for blk in range(bk_ref.shape[0]):
        b = bk_ref[blk]
        acc = jnp.zeros(b.shape, F32)
        for t in range(REL_BUCKETS):
            acc = jnp.where(b == t, tab_ref[t, hh] - far, acc)
        o_ref[0, blk] = jnp.where(b < 0, NEG, acc * LOG2E)


def _bias_tiles(table, buckets):
    nb, r, c = buckets.shape
    heads = table.shape[1]
    return pl.pallas_call(
        _bias_kernel,
        grid=(heads,),
        in_specs=[pl.BlockSpec(memory_space=pltpu.SMEM),
                  pl.BlockSpec((nb, r, c), lambda hh: (0, 0, 0))],
        out_specs=pl.BlockSpec((1, nb, r, c), lambda hh: (hh, 0, 0, 0)),
        out_shape=jax.ShapeDtypeStruct((heads, nb, r, c), F32),
        compiler_params=pltpu.CompilerParams(dimension_semantics=("arbitrary",)),
        name="rel_bias",
    )(table, buckets)


def _attn_kernel(q_ref, k_ref, vt_ref, bias_ref, lam_ref, sg_ref, o_ref,
                 qzz_ref, m_ref, mt_ref, acc_ref, sa_ref, sb_ref, *, lam_init):
    lv = lam_ref[...]
    lam = (jnp.exp(jnp.sum(lv[0:1] * lv[1:2], axis=1, keepdims=True))
           - jnp.exp(jnp.sum(lv[2:3] * lv[3:4], axis=1, keepdims=True)) + lam_init)

    def query_tile(a, carry):
        _attn_query_tile(a, lam, q_ref, k_ref, vt_ref, bias_ref, sg_ref, o_ref,
                         qzz_ref, m_ref, mt_ref, acc_ref, sa_ref, sb_ref, lam_init=lam_init)
        return carry

    n_tiles = q_ref.shape[1] // T_SEQ
    lax.fori_loop(0, n_tiles, query_tile, 0)
    _attn_finalize(n_tiles - 1, lam, acc_ref, sg_ref, o_ref, lam_init)


def _attn_query_tile(a, lam, q_ref, k_ref, vt_ref, bias_ref, sg_ref, o_ref,
                     qzz_ref, m_ref, mt_ref, acc_ref, sa_ref, sb_ref, *, lam_init):
    t = T_SEQ
    hd = q_ref.shape[2]
    nb = t // META_BLK
    rows = pl.ds(pl.multiple_of(a * t, t), t)
    bnear_ref = bias_ref

    qt = q_ref[0, rows, :].astype(F32).T
    row = lax.broadcasted_iota(jnp.int32, (hd, t), 0)
    q1 = jnp.where(row < hd // 2, qt, 0.0)
    q2 = jnp.where(row >= hd // 2, qt, 0.0)
    qzz_ref[...] = jnp.concatenate([q1, q2], axis=1).astype(BF16)

    def produce(j, buf):
        kt = k_ref[0, pl.ds(pl.multiple_of(j * t, t), t), :]
        mts = []
        for cs in col_chunks:
            s = jnp.dot(kt, qzz_ref[:, cs], preferred_element_type=F32)
            buf[:, cs] = s
            mts.append(jnp.max(s, axis=0, keepdims=True))
        return jnp.concatenate(mts, axis=1)

    def probs(s, m):
        return jnp.exp2(s - m).astype(BF16)

    def consume(buf, mt, vt):
        for cs in col_chunks:
            m_old = m_ref[:, cs]
            m_new = jnp.maximum(m_old, mt[:, cs])
            alpha = jnp.exp2(m_old - m_new)
            acc_ref[:, cs] = alpha * acc_ref[:, cs] + jnp.dot(
                vt, probs(buf[:, cs], m_new), preferred_element_type=F32)
            m_ref[:, cs] = m_new

    def consume_diag(buf, j):
        bias = bnear_ref[0, 1:nb + 1].reshape(t, t)
        mts = []
        for cs in col_chunks:
            bc = slice(cs.start % t, cs.start % t + COL_CHUNK)
            s = buf[:, cs] + bias[:, bc]
            buf[:, cs] = s
            mts.append(jnp.max(s, axis=0, keepdims=True))
        consume(buf, jnp.concatenate(mts, axis=1), vt_ref[0, j, 0])

    def fix_prev(buf, mt):
        corner = bnear_ref[0, 0, :, 0:LANES]
        pieces, start = [], 0
        for base in (0, t):
            cs = slice(base, base + LANES)
            buf[t - META_BLK:, cs] = buf[t - META_BLK:, cs] + corner
            if base > start:
                pieces.append(mt[:, start:base])
            pieces.append(jnp.max(buf[:, cs], axis=0, keepdims=True))
            start = base + LANES
        return jnp.concatenate(pieces + [mt[:, start:]], axis=1)

    def init():
        s0 = jnp.dot(k_ref[0, t - META_BLK:t, :], qzz_ref[...], preferred_element_type=F32)
        bmeta = bias_ref[0, nb + 1 + jnp.minimum(a, 2)]
        s0 = s0 + jnp.concatenate([bmeta, bmeta], axis=1)
        m0 = jnp.max(s0, axis=0, keepdims=True)
        m_ref[...] = m0
        acc_ref[...] = jnp.dot(vt_ref[0, 0, 0, :, t - META_BLK:t], probs(s0, m0),
                               preferred_element_type=F32)

    col_chunks = [slice(c, c + COL_CHUNK) for c in range(0, 2 * t, COL_CHUNK)]
    odd = lax.rem(a, 2)

    def finish_previous():
        _attn_finalize(a - 1, lam, acc_ref, sg_ref, o_ref, lam_init)

    @pl.when(a == 0)
    def _():
        init()

    @pl.when(a == 1)
    def _():
        finish_previous()
        init()
        produce(1, sb_ref)
        consume_diag(sb_ref, 1)

    @pl.when((a >= 2) & (odd == 1))
    def _():
        finish_previous()
        init()
        mtb = produce(1, sb_ref)
        mt_ref[...] = produce(2, sa_ref)
        consume(sb_ref, mtb, vt_ref[0, 1, 0])

    @pl.when((a >= 2) & (odd == 0))
    def _():
        finish_previous()
        init()
        mt_ref[...] = produce(1, sa_ref)

    @pl.when(a >= 2)
    def _():
        def pair(j, mta):
            mtb = produce(j + 1, sb_ref)
            consume(sa_ref, mta, vt_ref[0, j, 0])
            mta_next = produce(j + 2, sa_ref)
            consume(sb_ref, mtb, vt_ref[0, j + 1, 0])
            return mta_next

        n_pairs = (a - 2) // 2
        odd_pairs = lax.rem(n_pairs, 2)

        @pl.when(odd_pairs == 1)
        def _():
            mt_ref[...] = pair(1 + odd, mt_ref[...])

        n_quads = n_pairs // 2
        odd_quads = lax.rem(n_quads, 2)

        @pl.when(odd_quads == 1)
        def _():
            j = 1 + odd + 2 * odd_pairs
            mt_ref[...] = pair(j + 2, pair(j, mt_ref[...]))

        j0 = 1 + odd + 2 * odd_pairs + 4 * odd_quads

        def oct_body(i, mta):
            j = j0 + 8 * i
            for step in range(0, 8, 2):
                mta = pair(j + step, mta)
            return mta

        mta = lax.fori_loop(0, n_quads // 2, oct_body, mt_ref[...])
        produce(a, sb_ref)
        consume(sa_ref, fix_prev(sa_ref, mta), vt_ref[0, a - 1, 0])
        consume_diag(sb_ref, a)


def _attn_finalize(a, lam, acc_ref, sg_ref, o_ref, lam_init):
    t = T_SEQ
    hd = o_ref.shape[2]
    on = acc_ref[0:hd, :] * (1.0 / acc_ref[hd:hd + 1, :])
    o = on[:, :t] - lam * on[:, t:]
    o = (o * lax.rsqrt(jnp.mean(o * o, axis=0, keepdims=True) + SUBLN_EPS)) * sg_ref[...]
    o = o * (1.0 - lam_init)
    o_ref[0, pl.ds(pl.multiple_of(a * t, t), t), :] = o.T.astype(BF16)


def _attention(q, k, vt, bias, lamv, sg, lam_init):
    b, lh, d = q.shape
    nt = lh // T_SEQ
    hd = d // DA_HEADS
    hv = vt.shape[3]
    seq = pl.BlockSpec((1, lh, hd), lambda bi, hh: (bi, 0, hh))
    return pl.pallas_call(
        functools.partial(_attn_kernel, lam_init=lam_init),
        grid=(b, DA_HEADS),
        in_specs=[
            seq,
            seq,
            pl.BlockSpec((1, nt, 1, hv, T_SEQ), lambda bi, hh: (bi, 0, hh, 0, 0)),
            pl.BlockSpec((1,) + bias.shape[1:], lambda bi, hh: (hh, 0, 0, 0)),
            pl.BlockSpec(lamv.shape, lambda bi, hh: (0, 0)),
            pl.BlockSpec(sg.shape, lambda bi, hh: (0, 0)),
        ],
        out_specs=seq,
        out_shape=jax.ShapeDtypeStruct((b, lh, d), BF16),
        scratch_shapes=[pltpu.VMEM((hd, 2 * T_SEQ), BF16), pltpu.VMEM((1, 2 * T_SEQ), F32),
                        pltpu.VMEM((1, 2 * T_SEQ), F32), pltpu.VMEM((hv, 2 * T_SEQ), F32),
                        pltpu.VMEM((T_SEQ, 2 * T_SEQ), F32), pltpu.VMEM((T_SEQ, 2 * T_SEQ), F32)],
        compiler_params=pltpu.CompilerParams(
            dimension_semantics=("arbitrary", "arbitrary"), vmem_limit_bytes=VMEM_LIMIT_BYTES),
        name="diff_attn",
    )(q, k, vt, bias, lamv, sg)


def _attn_out_ffn_kernel(h_ref, o_ref, wo_ref, gf_ref, wg_ref, wu_ref, wd_ref, out_ref, *,
                         tiles_per_seq):
    t, d = h_ref.shape

    def rows_of_tile(h, o):
        hmid = h + jnp.dot(o, wo_ref[...], preferred_element_type=F32)
        return _ffn(hmid, gf_ref, wg_ref, wu_ref, wd_ref)

    lead = lax.rem(pl.program_id(0), tiles_per_seq) == 0

    @pl.when(lead)
    def _():
        out_ref[0:t - LEAD_ROWS, :] = jnp.zeros((t - LEAD_ROWS, d), F32)
        out_ref[t - LEAD_ROWS:, :] = rows_of_tile(h_ref[t - LEAD_ROWS:, :], o_ref[t - LEAD_ROWS:, :])

    @pl.when(jnp.logical_not(lead))
    def _():
        out_ref[...] = rows_of_tile(h_ref[...], o_ref[...])


def _attn_out_ffn(h, o, wo, gf, wg, wu, wd):
    b, lh, d = h.shape
    rows = b * lh
    tile_f32 = pl.BlockSpec((T_SEQ, d), lambda i: (i, 0))
    out = pl.pallas_call(
        functools.partial(_attn_out_ffn_kernel, tiles_per_seq=lh // T_SEQ),
        grid=(rows // T_SEQ,),
        in_specs=[tile_f32, pl.BlockSpec((T_SEQ, d), lambda i: (i, 0)), _const_spec(wo.shape),
                  _const_spec(gf.shape), _layer_spec(*wg), _layer_spec(*wu), _layer_spec(*wd)],
        out_specs=tile_f32,
        out_shape=jax.ShapeDtypeStruct((rows, d), F32),
        compiler_params=pltpu.CompilerParams(
            dimension_semantics=("arbitrary",), vmem_limit_bytes=VMEM_LIMIT_BYTES),
        name="attn_out_ffn",
    )(h.reshape(rows, d), o.reshape(rows, d), wo, gf, wg[0], wu[0], wd[0])
    return out.reshape(b, lh, d)


def _conv_ffn_kernel(h_ref, gm_ref, win_ref, bin_ref, dww_ref, dwb_ref, lng_ref, lnb_ref,
                     wout_ref, bout_ref, gf_ref, wg_ref, wu_ref, wd_ref, o_ref, uext_ref, conv_ref):
    i = pl.program_id(1)
    t, d = h_ref.shape[1:]

    @pl.when(i == 0)
    def _():
        uext_ref[0:CONV_HALO, :] = jnp.zeros((CONV_HALO, d), F32)
        for rows in (LEAD_ROWS, t):
            uext_ref[CONV_HALO + rows:CONV_HALO + rows + SUBLANES, :] = jnp.zeros((SUBLANES, d), F32)
        o_ref[0, 0:t - LEAD_ROWS, :] = jnp.zeros((t - LEAD_ROWS, d), F32)
        o_ref[0, t - LEAD_ROWS:, :] = _conv_ffn_rows(
            h_ref[0, t - LEAD_ROWS:, :], _positions(0, t)[t - LEAD_ROWS:], gm_ref, win_ref, bin_ref,
            dww_ref, dwb_ref, lng_ref, lnb_ref, wout_ref, bout_ref, gf_ref, wg_ref, wu_ref, wd_ref,
            uext_ref, conv_ref)

    @pl.when(i > 0)
    def _():
        o_ref[0] = _conv_ffn_rows(
            h_ref[0], _positions(i, t), gm_ref, win_ref, bin_ref, dww_ref, dwb_ref, lng_ref, lnb_ref,
            wout_ref, bout_ref, gf_ref, wg_ref, wu_ref, wd_ref, uext_ref, conv_ref)


def _conv_ffn_rows(h, pos, gm_ref, win_ref, bin_ref, dww_ref, dwb_ref, lng_ref, lnb_ref,
                   wout_ref, bout_ref, gf_ref, wg_ref, wu_ref, wd_ref, uext_ref, conv_ref):
    rows, d = h.shape
    y = _rms(h, gm_ref[...], RMS_EPS).astype(BF16)
    ag = jnp.dot(y, win_ref[...], preferred_element_type=F32) + bin_ref[...]
    u = ag[:, :d] * jax.nn.sigmoid(ag[:, d:])
    u = jnp.where(pos >= 0, u, 0.0)
    uext_ref[CONV_HALO:CONV_HALO + rows, :] = u

    base = CONV_HALO - (CONV_WIDTH - 1)
    span = CONV_ROWS + SUBLANES
    n_oct = (base + CONV_WIDTH - 1) // SUBLANES + 1

    def conv_body(c, carry):
        r0 = pl.multiple_of(c * CONV_ROWS, CONV_ROWS)
        for lb in range(d // LANES):
            cs = slice(lb * LANES, (lb + 1) * LANES)
            um = [uext_ref[pl.ds(r0 + SUBLANES * m, span), cs] for m in range(n_oct)]
            out = None
            for sh in range(SUBLANES):
                part = None
                for m in range(n_oct):
                    j = SUBLANES * m + sh - base
                    if 0 <= j < CONV_WIDTH:
                        term = dww_ref[j:j + 1, cs] * um[m]
                        part = term if part is None else part + term
                if sh:
                    part = pltpu.roll(part, span - sh, axis=0)
                out = part if out is None else out + part
            conv_ref[pl.ds(r0, CONV_ROWS), cs] = out[:CONV_ROWS] + dwb_ref[:, cs]
        return carry

    lax.fori_loop(0, rows // CONV_ROWS, conv_body, 0)
    uext_ref[0:CONV_HALO, :] = uext_ref[rows:rows + CONV_HALO, :]

    c = conv_ref[0:rows, :]
    mu = jnp.mean(c, axis=-1, keepdims=True)
    cc = c - mu
    var = jnp.mean(cc * cc, axis=-1, keepdims=True)
    z = (cc * lax.rsqrt(var + LN_EPS)) * lng_ref[...] + lnb_ref[...]
    z = (z * jax.nn.sigmoid(z)).astype(BF16)
    hmid = h + jnp.dot(z, wout_ref[...], preferred_element_type=F32) + bout_ref[...]
    return _ffn(hmid, gf_ref, wg_ref, wu_ref, wd_ref)


def _conv_ffn(h, gm, win, bin_, dww, dwb, lng, lnb, wout, bout, gf, wg, wu, wd):
    b, lh, d = h.shape
    nt = lh // T_SEQ
    tile = pl.BlockSpec((1, T_SEQ, d), lambda bi, i: (bi, i, 0))
    consts = [gm, win, bin_, dww, dwb, lng, lnb, wout, bout, gf]
    ffn = [wg, wu, wd]
    return pl.pallas_call(
        _conv_ffn_kernel,
        grid=(b, nt),
        in_specs=[tile] + [_const_spec(c.shape) for c in consts] + [_layer_spec(*w) for w in ffn],
        out_specs=tile,
        out_shape=jax.ShapeDtypeStruct((b, lh, d), F32),
        scratch_shapes=[pltpu.VMEM((T_SEQ + CONV_HALO + SUBLANES, d), F32), pltpu.VMEM((T_SEQ, d), F32)],
        compiler_params=pltpu.CompilerParams(
            dimension_semantics=("arbitrary", "arbitrary"), vmem_limit_bytes=VMEM_LIMIT_BYTES),
        name="conv_ffn",
    )(h, *consts, *(w[0] for w in ffn))


def kernel(x, meta_tokens, norm_mix_g, norm_ffn_g, final_norm_g, rel_bias_table, pool_w, pool_b, pool_scale, attn_w_qkv, attn_w_o, attn_lambda_q1, attn_lambda_k1, attn_lambda_q2, attn_lambda_k2, attn_subln_g, conv_w_in, conv_b_in, conv_dw_w, conv_dw_b, conv_ln_g, conv_ln_b, conv_w_out, conv_b_out, ffn_w_gate, ffn_w_up, ffn_w_down):
    b, seq, d = x.shape
    depth = norm_mix_g.shape[0]
    assert seq % T_SEQ == 0 and d % (LANES * len(POOL_WINDOWS)) == 0
    assert depth % N_MIXERS == 1, "the last layer must be a pooling layer (it applies the final norm)"

    row = lambda v: v.reshape(1, -1).astype(F32)
    h = x.astype(F32)
    bias = None
    ffn_stacks = [w.astype(BF16) for w in (ffn_w_gate, ffn_w_up, ffn_w_down)]
    for i in range(depth):
        mixer, j = i % N_MIXERS, i // N_MIXERS
        gm, gf = row(norm_mix_g[i]), row(norm_ffn_g[i])
        wg, wu, wd = ((w, i) for w in ffn_stacks)
        if mixer == 0:
            h = _pool_ffn(h, gm, pool_w[j].astype(BF16), row(pool_b[j]), row(pool_scale[j]),
                          gf, wg, wu, wd,
                          meta=meta_tokens.astype(F32) if i == 0 else None,
                          final_g=row(final_norm_g) if i == depth - 1 else None)
        elif mixer == 1:
            lam_init = 0.8 - 0.6 * math.exp(-0.3 * i)
            hd = d // DA_HEADS
            if bias is None:
                bias = _bias_tiles(rel_bias_table.astype(F32), _bias_buckets())
            qscale = (hd // 2) ** -0.5 * LOG2E
            q, k, vt = _qkv(h, gm, attn_w_qkv[j].astype(BF16), qscale)
            lamv = jnp.stack([attn_lambda_q1[j], attn_lambda_k1[j],
                              attn_lambda_q2[j], attn_lambda_k2[j]]).astype(F32)
            o = _attention(q, k, vt, bias, lamv, attn_subln_g[j].reshape(hd, 1).astype(F32), lam_init)
            h = _attn_out_ffn(h, o, attn_w_o[j].astype(BF16), gf, wg, wu, wd)
        else:
            h = _conv_ffn(h, gm, conv_w_in[j].astype(BF16), row(conv_b_in[j]),
                          conv_dw_w[j].astype(F32), row(conv_dw_b[j]), row(conv_ln_g[j]),
                          row(conv_ln_b[j]), conv_w_out[j].astype(BF16), row(conv_b_out[j]),
                          gf, wg, wu, wd)
    return h
```

```python
import functools
import math

import jax
import jax.numpy as jnp
from jax import lax
from jax.experimental import pallas as pl
from jax.experimental.pallas import tpu as pltpu

F32 = jnp.float32
BF16 = jnp.bfloat16

N_META = 16
CHUNK = 64
POOL_WINDOWS = (2, 4, 8, 16)
DA_HEADS = 8
REL_BUCKETS = 32
REL_MAX_DIST = 128
CONV_WIDTH = 31
RMS_EPS = 1e-6
LN_EPS = 1e-5
SUBLN_EPS = 1e-5
N_MIXERS = 3

LANES = 128
SUBLANES = 8
MXU_DIM = 256
VMEM_LIMIT_BYTES = 56 * 1024 * 1024

T_SEQ = 512
META_BLK = 128
ONES_ROWS = 16
COL_CHUNK = 2 * MXU_DIM
LEAD_ROWS = 32
POOL_HALO = 16
CONV_HALO = 32
CONV_ROWS = 32
NEG = -1e30
LOG2E = math.log2(math.e)
FAR_BUCKET = REL_BUCKETS // 2 - 1


def _ffn_chunks(f):
    step = 3 * MXU_DIM
    return [(s, min(s + step, f)) for s in range(0, f, step)]


def _rms(x, g, eps):
    ms = jnp.mean(x * x, axis=-1, keepdims=True)
    return (x * lax.rsqrt(ms + eps)) * g


def _ffn(hmid, g_ref, wg_ref, wu_ref, wd_ref):
    x = _rms(hmid, g_ref[...], RMS_EPS).astype(BF16)
    acc = None
    for f0, f1 in _ffn_chunks(wg_ref.shape[1]):
        gate = jnp.dot(x, wg_ref[:, f0:f1], preferred_element_type=F32)
        up = jnp.dot(x, wu_ref[:, f0:f1], preferred_element_type=F32)
        act = ((gate * jax.nn.sigmoid(gate)) * up).astype(BF16)
        part = jnp.dot(act, wd_ref[f0:f1, :], preferred_element_type=F32)
        acc = part if acc is None else acc + part
    return hmid + acc


def _positions(tile_idx, rows):
    r = lax.broadcasted_iota(jnp.int32, (rows, 1), 0)
    return tile_idx * rows + r - (T_SEQ - N_META)


def _pool_ffn_kernel(h_ref, gm_ref, pw_ref, pb_ref, ps_ref, gf_ref, wg_ref, wu_ref, wd_ref, *rest,
                     first, final):
    rest = list(rest)
    meta_ref = rest.pop(0) if first else None
    fg_ref = rest.pop(0) if final else None
    o_ref, halo_ref = rest
    i = pl.program_id(1)
    t, d = h_ref.shape[1:]

    def rows_of_tile(h, pos, halo):
        rows = h.shape[0]
        y = jnp.where(pos >= 0, _rms(h, gm_ref[...], RMS_EPS), 0.0)
        ext = jnp.concatenate([halo, y], axis=0)
        halo_ref[...] = y[rows - POOL_HALO:, :]
        group = d // len(POOL_WINDOWS)
        mixed = []
        for gi, win in enumerate(POOL_WINDOWS):
            c0 = gi * group
            s = ext[:, c0:c0 + group]
            shift = 1
            while shift < win:
                s = s + pltpu.roll(s, shift, axis=0)
                shift *= 2
            cnt = jnp.clip(pos + 1, 1, win).astype(F32)
            pooled = s[POOL_HALO:, :] / cnt - y[:, c0:c0 + group]
            mixed.append(jnp.dot(pooled.astype(BF16), pw_ref[gi], preferred_element_type=F32))
        mixed = jnp.concatenate(mixed, axis=1)
        hmid = h + (mixed + pb_ref[...]) * ps_ref[...]
        out = _ffn(hmid, gf_ref, wg_ref, wu_ref, wd_ref)
        if final:
            out = _rms(out, fg_ref[...], RMS_EPS)
        return out

    @pl.when(i == 0)
    def _():
        if first:
            h = jnp.concatenate([jnp.zeros((LEAD_ROWS - N_META, d), F32), meta_ref[...]], axis=0)
        else:
            h = h_ref[0, t - LEAD_ROWS:, :]
        pos = _positions(0, t)[t - LEAD_ROWS:]
        o_ref[0, 0:t - LEAD_ROWS, :] = jnp.zeros((t - LEAD_ROWS, d), F32)
        o_ref[0, t - LEAD_ROWS:, :] = rows_of_tile(h, pos, jnp.zeros((POOL_HALO, d), F32))

    @pl.when(i > 0)
    def _():
        o_ref[0] = rows_of_tile(h_ref[0], _positions(i, t), halo_ref[...])


def _const_spec(shape):
    nd = len(shape)
    return pl.BlockSpec(shape, lambda *_: (0,) * nd, pipeline_mode=pl.Buffered(1))


def _layer_spec(stack, layer):
    nd = stack.ndim - 1
    return pl.BlockSpec((None,) + stack.shape[1:], lambda *_: (layer,) + (0,) * nd,
                        pipeline_mode=pl.Buffered(1))


def _pool_ffn(h, gm, pw, pb, ps, gf, wg, wu, wd, meta=None, final_g=None):
    first, final = meta is not None, final_g is not None
    b, _, d = h.shape
    lh = h.shape[1] + T_SEQ if first else h.shape[1]
    nt = lh // T_SEQ
    tile = pl.BlockSpec((1, T_SEQ, d), lambda bi, i: (bi, i, 0))
    frames = pl.BlockSpec((1, T_SEQ, d), lambda bi, i: (bi, jnp.maximum(i - 1, 0), 0))
    in_specs = [frames if first else tile, _const_spec(gm.shape), _const_spec(pw.shape),
                _const_spec(pb.shape), _const_spec(ps.shape), _const_spec(gf.shape),
                _layer_spec(*wg), _layer_spec(*wu), _layer_spec(*wd)]
    args = [h, gm, pw, pb, ps, gf, wg[0], wu[0], wd[0]]
    if first:
        in_specs.append(_const_spec(meta.shape))
        args.append(meta)
    if final:
        in_specs.append(_const_spec(final_g.shape))
        args.append(final_g)
        out_spec = frames
        out_shape = jax.ShapeDtypeStruct((b, lh - T_SEQ, d), F32)
    else:
        out_spec = tile
        out_shape = jax.ShapeDtypeStruct((b, lh, d), F32)
    return pl.pallas_call(
        functools.partial(_pool_ffn_kernel, first=first, final=final),
        grid=(b, nt),
        in_specs=in_specs,
        out_specs=out_spec,
        out_shape=out_shape,
        scratch_shapes=[pltpu.VMEM((POOL_HALO, d), F32)],
        compiler_params=pltpu.CompilerParams(
            dimension_semantics=("arbitrary", "arbitrary"), vmem_limit_bytes=VMEM_LIMIT_BYTES),
        name="pool_ffn_final" if final else "pool_ffn",
    )(*args)


def _qkv_kernel(h_ref, g_ref, w_ref, q_ref, k_ref, vt_ref, *, qscale):
    h = h_ref[0]
    d = h.shape[1]
    y = _rms(h, g_ref[...], RMS_EPS).astype(BF16)
    q = jnp.dot(y, w_ref[:, 0:d], preferred_element_type=F32) * qscale
    k = jnp.dot(y, w_ref[:, d:2 * d], preferred_element_type=F32)
    v = jnp.dot(y, w_ref[:, 2 * d:3 * d], preferred_element_type=F32)
    q_ref[0] = q.astype(BF16)
    k_ref[0] = k.astype(BF16)
    vt = v.T.astype(BF16)
    heads, hv, rows = vt_ref.shape[2:]
    hd = d // heads
    for hh in range(heads):
        vt_ref[0, 0, hh, 0:hd, :] = vt[hh * hd:(hh + 1) * hd, :]
        vt_ref[0, 0, hh, hd:hv, :] = jnp.ones((hv - hd, rows), BF16)


def _qkv(h, g, w, qscale):
    b, lh, d = h.shape
    nt = lh // T_SEQ
    hv = d // DA_HEADS + ONES_ROWS
    tile = pl.BlockSpec((1, T_SEQ, d), lambda bi, i: (bi, i, 0))
    return pl.pallas_call(
        functools.partial(_qkv_kernel, qscale=qscale),
        grid=(b, nt),
        in_specs=[tile, _const_spec(g.shape), _const_spec(w.shape)],
        out_specs=[tile, tile,
                   pl.BlockSpec((1, 1, DA_HEADS, hv, T_SEQ), lambda bi, i: (bi, i, 0, 0, 0))],
        out_shape=[jax.ShapeDtypeStruct((b, lh, d), BF16), jax.ShapeDtypeStruct((b, lh, d), BF16),
                   jax.ShapeDtypeStruct((b, nt, DA_HEADS, hv, T_SEQ), BF16)],
        compiler_params=pltpu.CompilerParams(
            dimension_semantics=("arbitrary", "arbitrary"), vmem_limit_bytes=VMEM_LIMIT_BYTES),
        name="qkv",
    )(h, g, w)


def _t5_bucket(rel):
    half = REL_BUCKETS // 2
    max_exact = half // 2
    offset = jnp.where(rel > 0, half, 0)
    n = jnp.abs(rel)
    log_ratio = jnp.log(jnp.maximum(n, 1).astype(F32) / max_exact) / math.log(REL_MAX_DIST / max_exact)
    large = jnp.minimum(max_exact + (log_ratio * (half - max_exact)).astype(jnp.int32), half - 1)
    return offset + jnp.where(n < max_exact, n, large)


def _bias_buckets():
    t = T_SEQ
    kl = lax.broadcasted_iota(jnp.int32, (t, t), 0)
    ql = lax.broadcasted_iota(jnp.int32, (t, t), 1)
    prev = _t5_bucket(kl - t - ql)[t - META_BLK:]
    diag = jnp.where(kl // CHUNK <= ql // CHUNK, _t5_bucket(kl - ql), -1)
    kr = lax.broadcasted_iota(jnp.int32, (META_BLK, t), 0)
    qc = lax.broadcasted_iota(jnp.int32, (META_BLK, t), 1)
    mk = kr - (META_BLK - N_META)
    mq = qc - (t - N_META)
    lead = jnp.where(mq >= 0, _t5_bucket(mk - mq), FAR_BUCKET)
    first = _t5_bucket(mk - N_META - qc)
    later = jnp.full((META_BLK, t), FAR_BUCKET, jnp.int32)
    meta = [jnp.where(mk >= 0, m, -1) for m in (lead, first, later)]
    nb = t // META_BLK
    return jnp.concatenate([prev[None], diag.reshape(nb, META_BLK, t), jnp.stack(meta)], axis=0)


def _bias_kernel(tab_ref, bk_ref, o_ref):
    hh = pl.program_id(0)
    far = tab_ref[FAR_BUCKET, hh]
    for blk in range(bk_ref.shape[0]):
        b = bk_ref[blk]
        acc = jnp.zeros(b.shape, F32)
        for t in range(REL_BUCKETS):
            acc = jnp.where(b == t, tab_ref[t, hh] - far, acc)
        o_ref[0, blk] = jnp.where(b < 0, NEG, acc * LOG2E)


def _bias_tiles(table, buckets):
    nb, r, c = buckets.shape
    heads = table.shape[1]
    return pl.pallas_call(
        _bias_kernel,
        grid=(heads,),
        in_specs=[pl.BlockSpec(memory_space=pltpu.SMEM),
                  pl.BlockSpec((nb, r, c), lambda hh: (0, 0, 0))],
        out_specs=pl.BlockSpec((1, nb, r, c), lambda hh: (hh, 0, 0, 0)),
        out_shape=jax.ShapeDtypeStruct((heads, nb, r, c), F32),
        compiler_params=pltpu.CompilerParams(dimension_semantics=("arbitrary",)),
        name="rel_bias",
    )(table, buckets)


def _attn_kernel(q_ref, k_ref, vt_ref, bias_ref, lam_ref, sg_ref, o_ref,
                 qzz_ref, m_ref, mt_ref, acc_ref, sa_ref, sb_ref, *, lam_init):
    lv = lam_ref[...]
    lam = (jnp.exp(jnp.sum(lv[0:1] * lv[1:2], axis=1, keepdims=True))
           - jnp.exp(jnp.sum(lv[2:3] * lv[3:4], axis=1, keepdims=True)) + lam_init)

    def query_tile(a, carry):
        _attn_query_tile(a, lam, q_ref, k_ref, vt_ref, bias_ref, sg_ref, o_ref,
                         qzz_ref, m_ref, mt_ref, acc_ref, sa_ref, sb_ref, lam_init=lam_init)
        return carry

    n_tiles = q_ref.shape[1] // T_SEQ
    lax.fori_loop(0, n_tiles, query_tile, 0)
    _attn_finalize(n_tiles - 1, lam, acc_ref, sg_ref, o_ref, lam_init)


def _attn_query_tile(a, lam, q_ref, k_ref, vt_ref, bias_ref, sg_ref, o_ref,
                     qzz_ref, m_ref, mt_ref, acc_ref, sa_ref, sb_ref, *, lam_init):
    t = T_SEQ
    hd = q_ref.shape[2]
    nb = t // META_BLK
    rows = pl.ds(pl.multiple_of(a * t, t), t)
    bnear_ref = bias_ref

    qt = q_ref[0, rows, :].astype(F32).T
    row = lax.broadcasted_iota(jnp.int32, (hd, t), 0)
    q1 = jnp.where(row < hd // 2, qt, 0.0)
    q2 = jnp.where(row >= hd // 2, qt, 0.0)
    qzz_ref[...] = jnp.concatenate([q1, q2], axis=1).astype(BF16)

    def produce(j, buf):
        kt = k_ref[0, pl.ds(pl.multiple_of(j * t, t), t), :]
        mts = []
        for cs in col_chunks:
            s = jnp.dot(kt, qzz_ref[:, cs], preferred_element_type=F32)
            buf[:, cs] = s
            mts.append(jnp.max(s, axis=0, keepdims=True))
        return jnp.concatenate(mts, axis=1)

    def probs(s, m):
        return jnp.exp2(s - m).astype(BF16)

    def consume(buf, mt, vt):
        for cs in col_chunks:
            m_old = m_ref[:, cs]
            m_new = jnp.maximum(m_old, mt[:, cs])
            alpha = jnp.exp2(m_old - m_new)
            acc_ref[:, cs] = alpha * acc_ref[:, cs] + jnp.dot(
                vt, probs(buf[:, cs], m_new), preferred_element_type=F32)
            m_ref[:, cs] = m_new

    half = t // 2
    maps = (0, t)

    def produce_diag(j, buf):
        k0 = pl.multiple_of(j * t, t)
        k_top = k_ref[0, pl.ds(k0, half), :]
        k_bot = k_ref[0, pl.ds(k0 + half, half), :]
        for base in maps:
            buf[0:half, base:base + t] = jnp.dot(
                k_top, qzz_ref[:, base:base + t], preferred_element_type=F32)
            buf[half:t, base + half:base + t] = jnp.dot(
                k_bot, qzz_ref[:, base + half:base + t], preferred_element_type=F32)

    def consume_diag(buf, j):
        vt = vt_ref[0, j, 0]
        for base in maps:
            cs, hi = slice(base, base + t), slice(base + half, base + t)
            s_top = buf[0:half, cs] + bnear_ref[0, 1:1 + nb // 2].reshape(half, t)
            s_bot = buf[half:t, hi] + bnear_ref[0, 1 + nb // 2:1 + nb, :, half:t].reshape(half, half)
            mt_top = jnp.max(s_top, axis=0, keepdims=True)
            mt = jnp.concatenate(
                [mt_top[:, 0:half], jnp.maximum(mt_top[:, half:t], jnp.max(s_bot, axis=0, keepdims=True))],
                axis=1)
            m_old = m_ref[:, cs]
            m_new = jnp.maximum(m_old, mt)
            alpha = jnp.exp2(m_old - m_new)
            p_top = probs(s_top, m_new)
            p_bot = probs(s_bot, m_new[:, half:t])
            pv = jnp.dot(vt[:, 0:half], p_top, preferred_element_type=F32)
            pv_hi = pv[:, half:t] + jnp.dot(vt[:, half:t], p_bot, preferred_element_type=F32)
            acc_ref[:, cs] = alpha * acc_ref[:, cs] + jnp.concatenate([pv[:, 0:half], pv_hi], axis=1)
            m_ref[:, cs] = m_new

    def fix_prev(buf, mt):
        corner = bnear_ref[0, 0, :, 0:LANES]
        pieces, start = [], 0
        for base in (0, t):
            cs = slice(base, base + LANES)
            buf[t - META_BLK:, cs] = buf[t - META_BLK:, cs] + corner
            if base > start:
                pieces.append(mt[:, start:base])
            pieces.append(jnp.max(buf[:, cs], axis=0, keepdims=True))
            start = base + LANES
        return jnp.concatenate(pieces + [mt[:, start:]], axis=1)

    def init():
        s0 = jnp.dot(k_ref[0, t - META_BLK:t, :], qzz_ref[...], preferred_element_type=F32)
        bmeta = bias_ref[0, nb + 1 + jnp.minimum(a, 2)]
        s0 = s0 + jnp.concatenate([bmeta, bmeta], axis=1)
        m0 = jnp.max(s0, axis=0, keepdims=True)
        m_ref[...] = m0
        acc_ref[...] = jnp.dot(vt_ref[0, 0, 0, :, t - META_BLK:t], probs(s0, m0),
                               preferred_element_type=F32)

    col_chunks = [slice(c, c + COL_CHUNK) for c in range(0, 2 * t, COL_CHUNK)]
    odd = lax.rem(a, 2)

    def finish_previous():
        _attn_finalize(a - 1, lam, acc_ref, sg_ref, o_ref, lam_init)

    @pl.when(a == 0)
    def _():
        init()

    @pl.when(a == 1)
    def _():
        finish_previous()
        init()
        produce_diag(1, sb_ref)
        consume_diag(sb_ref, 1)

    @pl.when((a >= 2) & (odd == 1))
    def _():
        finish_previous()
        init()
        mtb = produce(1, sb_ref)
        mt_ref[...] = produce(2, sa_ref)
        consume(sb_ref, mtb, vt_ref[0, 1, 0])

    @pl.when((a >= 2) & (odd == 0))
    def _():
        finish_previous()
        init()
        mt_ref[...] = produce(1, sa_ref)

    @pl.when(a >= 2)
    def _():
        def pair(j, mta):
            mtb = produce(j + 1, sb_ref)
            consume(sa_ref, mta, vt_ref[0, j, 0])
            mta_next = produce(j + 2, sa_ref)
            consume(sb_ref, mtb, vt_ref[0, j + 1, 0])
            return mta_next

        n_pairs = (a - 2) // 2
        odd_pairs = lax.rem(n_pairs, 2)

        @pl.when(odd_pairs == 1)
        def _():
            mt_ref[...] = pair(1 + odd, mt_ref[...])

        n_quads = n_pairs // 2
        odd_quads = lax.rem(n_quads, 2)

        @pl.when(odd_quads == 1)
        def _():
            j = 1 + odd + 2 * odd_pairs
            mt_ref[...] = pair(j + 2, pair(j, mt_ref[...]))

        j0 = 1 + odd + 2 * odd_pairs + 4 * odd_quads

        def oct_body(i, mta):
            j = j0 + 8 * i
            for step in range(0, 8, 2):
                mta = pair(j + step, mta)
            return mta

        mta = lax.fori_loop(0, n_quads // 2, oct_body, mt_ref[...])
        produce_diag(a, sb_ref)
        consume(sa_ref, fix_prev(sa_ref, mta), vt_ref[0, a - 1, 0])
        consume_diag(sb_ref, a)


def _attn_finalize(a, lam, acc_ref, sg_ref, o_ref, lam_init):
    t = T_SEQ
    hd = o_ref.shape[2]
    on = acc_ref[0:hd, :] * (1.0 / acc_ref[hd:hd + 1, :])
    o = on[:, :t] - lam * on[:, t:]
    o = (o * lax.rsqrt(jnp.mean(o * o, axis=0, keepdims=True) + SUBLN_EPS)) * sg_ref[...]
    o = o * (1.0 - lam_init)
    o_ref[0, pl.ds(pl.multiple_of(a * t, t), t), :] = o.T.astype(BF16)


def _attention(q, k, vt, bias, lamv, sg, lam_init):
    b, lh, d = q.shape
    nt = lh // T_SEQ
    hd = d // DA_HEADS
    hv = vt.shape[3]
    seq = pl.BlockSpec((1, lh, hd), lambda bi, hh: (bi, 0, hh))
    return pl.pallas_call(
        functools.partial(_attn_kernel, lam_init=lam_init),
        grid=(b, DA_HEADS),
        in_specs=[
            seq,
            seq,
            pl.BlockSpec((1, nt, 1, hv, T_SEQ), lambda bi, hh: (bi, 0, hh, 0, 0)),
            pl.BlockSpec((1,) + bias.shape[1:], lambda bi, hh: (hh, 0, 0, 0)),
            pl.BlockSpec(lamv.shape, lambda bi, hh: (0, 0)),
            pl.BlockSpec(sg.shape, lambda bi, hh: (0, 0)),
        ],
        out_specs=seq,
        out_shape=jax.ShapeDtypeStruct((b, lh, d), BF16),
        scratch_shapes=[pltpu.VMEM((hd, 2 * T_SEQ), BF16), pltpu.VMEM((1, 2 * T_SEQ), F32),
                        pltpu.VMEM((1, 2 * T_SEQ), F32), pltpu.VMEM((hv, 2 * T_SEQ), F32),
                        pltpu.VMEM((T_SEQ, 2 * T_SEQ), F32), pltpu.VMEM((T_SEQ, 2 * T_SEQ), F32)],
        compiler_params=pltpu.CompilerParams(
            dimension_semantics=("arbitrary", "arbitrary"), vmem_limit_bytes=VMEM_LIMIT_BYTES),
        name="diff_attn",
    )(q, k, vt, bias, lamv, sg)


def _attn_out_ffn_kernel(h_ref, o_ref, wo_ref, gf_ref, wg_ref, wu_ref, wd_ref, out_ref, *,
                         tiles_per_seq):
    t, d = h_ref.shape

    def rows_of_tile(h, o):
        hmid = h + jnp.dot(o, wo_ref[...], preferred_element_type=F32)
        return _ffn(hmid, gf_ref, wg_ref, wu_ref, wd_ref)

    lead = lax.rem(pl.program_id(0), tiles_per_seq) == 0

    @pl.when(lead)
    def _():
        out_ref[0:t - LEAD_ROWS, :] = jnp.zeros((t - LEAD_ROWS, d), F32)
        out_ref[t - LEAD_ROWS:, :] = rows_of_tile(h_ref[t - LEAD_ROWS:, :], o_ref[t - LEAD_ROWS:, :])

    @pl.when(jnp.logical_not(lead))
    def _():
        out_ref[...] = rows_of_tile(h_ref[...], o_ref[...])


def _attn_out_ffn(h, o, wo, gf, wg, wu, wd):
    b, lh, d = h.shape
    rows = b * lh
    tile_f32 = pl.BlockSpec((T_SEQ, d), lambda i: (i, 0))
    out = pl.pallas_call(
        functools.partial(_attn_out_ffn_kernel, tiles_per_seq=lh // T_SEQ),
        grid=(rows // T_SEQ,),
        in_specs=[tile_f32, pl.BlockSpec((T_SEQ, d), lambda i: (i, 0)), _const_spec(wo.shape),
                  _const_spec(gf.shape), _layer_spec(*wg), _layer_spec(*wu), _layer_spec(*wd)],
        out_specs=tile_f32,
        out_shape=jax.ShapeDtypeStruct((rows, d), F32),
        compiler_params=pltpu.CompilerParams(
            dimension_semantics=("arbitrary",), vmem_limit_bytes=VMEM_LIMIT_BYTES),
        name="attn_out_ffn",
    )(h.reshape(rows, d), o.reshape(rows, d), wo, gf, wg[0], wu[0], wd[0])
    return out.reshape(b, lh, d)


def _conv_ffn_kernel(h_ref, gm_ref, win_ref, bin_ref, dww_ref, dwb_ref, lng_ref, lnb_ref,
                     wout_ref, bout_ref, gf_ref, wg_ref, wu_ref, wd_ref, o_ref, uext_ref, conv_ref):
    i = pl.program_id(1)
    t, d = h_ref.shape[1:]

    @pl.when(i == 0)
    def _():
        uext_ref[0:CONV_HALO, :] = jnp.zeros((CONV_HALO, d), F32)
        for rows in (LEAD_ROWS, t):
            uext_ref[CONV_HALO + rows:CONV_HALO + rows + SUBLANES, :] = jnp.zeros((SUBLANES, d), F32)
        o_ref[0, 0:t - LEAD_ROWS, :] = jnp.zeros((t - LEAD_ROWS, d), F32)
        o_ref[0, t - LEAD_ROWS:, :] = _conv_ffn_rows(
            h_ref[0, t - LEAD_ROWS:, :], _positions(0, t)[t - LEAD_ROWS:], gm_ref, win_ref, bin_ref,
            dww_ref, dwb_ref, lng_ref, lnb_ref, wout_ref, bout_ref, gf_ref, wg_ref, wu_ref, wd_ref,
            uext_ref, conv_ref)

    @pl.when(i > 0)
    def _():
        o_ref[0] = _conv_ffn_rows(
            h_ref[0], _positions(i, t), gm_ref, win_ref, bin_ref, dww_ref, dwb_ref, lng_ref, lnb_ref,
            wout_ref, bout_ref, gf_ref, wg_ref, wu_ref, wd_ref, uext_ref, conv_ref)


def _conv_ffn_rows(h, pos, gm_ref, win_ref, bin_ref, dww_ref, dwb_ref, lng_ref, lnb_ref,
                   wout_ref, bout_ref, gf_ref, wg_ref, wu_ref, wd_ref, uext_ref, conv_ref):
    rows, d = h.shape
    y = _rms(h, gm_ref[...], RMS_EPS).astype(BF16)
    ag = jnp.dot(y, win_ref[...], preferred_element_type=F32) + bin_ref[...]
    u = ag[:, :d] * jax.nn.sigmoid(ag[:, d:])
    u = jnp.where(pos >= 0, u, 0.0)
    uext_ref[CONV_HALO:CONV_HALO + rows, :] = u

    base = CONV_HALO - (CONV_WIDTH - 1)
    span = CONV_ROWS + SUBLANES
    n_oct = (base + CONV_WIDTH - 1) // SUBLANES + 1

    def conv_body(c, carry):
        r0 = pl.multiple_of(c * CONV_ROWS, CONV_ROWS)
        for lb in range(d // LANES):
            cs = slice(lb * LANES, (lb + 1) * LANES)
            um = [uext_ref[pl.ds(r0 + SUBLANES * m, span), cs] for m in range(n_oct)]
            out = None
            for sh in range(SUBLANES):
                part = None
                for m in range(n_oct):
                    j = SUBLANES * m + sh - base
                    if 0 <= j < CONV_WIDTH:
                        term = dww_ref[j:j + 1, cs] * um[m]
                        part = term if part is None else part + term
                if sh:
                    part = pltpu.roll(part, span - sh, axis=0)
                out = part if out is None else out + part
            conv_ref[pl.ds(r0, CONV_ROWS), cs] = out[:CONV_ROWS] + dwb_ref[:, cs]
        return carry

    lax.fori_loop(0, rows // CONV_ROWS, conv_body, 0)
    uext_ref[0:CONV_HALO, :] = uext_ref[rows:rows + CONV_HALO, :]

    c = conv_ref[0:rows, :]
    mu = jnp.mean(c, axis=-1, keepdims=True)
    cc = c - mu
    var = jnp.mean(cc * cc, axis=-1, keepdims=True)
    z = (cc * lax.rsqrt(var + LN_EPS)) * lng_ref[...] + lnb_ref[...]
    z = (z * jax.nn.sigmoid(z)).astype(BF16)
    hmid = h + jnp.dot(z, wout_ref[...], preferred_element_type=F32) + bout_ref[...]
    return _ffn(hmid, gf_ref, wg_ref, wu_ref, wd_ref)


def _conv_ffn(h, gm, win, bin_, dww, dwb, lng, lnb, wout, bout, gf, wg, wu, wd):
    b, lh, d = h.shape
    nt = lh // T_SEQ
    tile = pl.BlockSpec((1, T_SEQ, d), lambda bi, i: (bi, i, 0))
    consts = [gm, win, bin_, dww, dwb, lng, lnb, wout, bout, gf]
    ffn = [wg, wu, wd]
    return pl.pallas_call(
        _conv_ffn_kernel,
        grid=(b, nt),
        in_specs=[tile] + [_const_spec(c.shape) for c in consts] + [_layer_spec(*w) for w in ffn],
        out_specs=tile,
        out_shape=jax.ShapeDtypeStruct((b, lh, d), F32),
        scratch_shapes=[pltpu.VMEM((T_SEQ + CONV_HALO + SUBLANES, d), F32), pltpu.VMEM((T_SEQ, d), F32)],
        compiler_params=pltpu.CompilerParams(
            dimension_semantics=("arbitrary", "arbitrary"), vmem_limit_bytes=VMEM_LIMIT_BYTES),
        name="conv_ffn",
    )(h, *consts, *(w[0] for w in ffn))


def kernel(x, meta_tokens, norm_mix_g, norm_ffn_g, final_norm_g, rel_bias_table, pool_w, pool_b, pool_scale, attn_w_qkv, attn_w_o, attn_lambda_q1, attn_lambda_k1, attn_lambda_q2, attn_lambda_k2, attn_subln_g, conv_w_in, conv_b_in, conv_dw_w, conv_dw_b, conv_ln_g, conv_ln_b, conv_w_out, conv_b_out, ffn_w_gate, ffn_w_up, ffn_w_down):
    b, seq, d = x.shape
    depth = norm_mix_g.shape[0]
    assert seq % T_SEQ == 0 and d % (LANES * len(POOL_WINDOWS)) == 0
    assert depth % N_MIXERS == 1, "the last layer must be a pooling layer (it applies the final norm)"

    row = lambda v: v.reshape(1, -1).astype(F32)
    h = x.astype(F32)
    bias = None
    ffn_stacks = [w.astype(BF16) for w in (ffn_w_gate, ffn_w_up, ffn_w_down)]
    for i in range(depth):
        mixer, j = i % N_MIXERS, i // N_MIXERS
        gm, gf = row(norm_mix_g[i]), row(norm_ffn_g[i])
        wg, wu, wd = ((w, i) for w in ffn_stacks)
        if mixer == 0:
            h = _pool_ffn(h, gm, pool_w[j].astype(BF16), row(pool_b[j]), row(pool_scale[j]),
                          gf, wg, wu, wd,
                          meta=meta_tokens.astype(F32) if i == 0 else None,
                          final_g=row(final_norm_g) if i == depth - 1 else None)
        elif mixer == 1:
            lam_init = 0.8 - 0.6 * math.exp(-0.3 * i)
            hd = d // DA_HEADS
            if bias is None:
                bias = _bias_tiles(rel_bias_table.astype(F32), _bias_buckets())
            qscale = (hd // 2) ** -0.5 * LOG2E
            q, k, vt = _qkv(h, gm, attn_w_qkv[j].astype(BF16), qscale)
            lamv = jnp.stack([attn_lambda_q1[j], attn_lambda_k1[j],
                              attn_lambda_q2[j], attn_lambda_k2[j]]).astype(F32)
            o = _attention(q, k, vt, bias, lamv, attn_subln_g[j].reshape(hd, 1).astype(F32), lam_init)
            h = _attn_out_ffn(h, o, attn_w_o[j].astype(BF16), gf, wg, wu, wd)
        else:
            h = _conv_ffn(h, gm, conv_w_in[j].astype(BF16), row(conv_b_in[j]),
                          conv_dw_w[j].astype(F32), row(conv_dw_b[j]), row(conv_ln_g[j]),
                          row(conv_ln_b[j]), conv_w_out[j].astype(BF16), row(conv_b_out[j]),
                          gf, wg, wu, wd)
    return h
```

```python
import functools
import math

import jax
import jax.numpy as jnp
from jax import lax
from jax.experimental import pallas as pl
from jax.experimental.pallas import tpu as pltpu

F32 = jnp.float32
BF16 = jnp.bfloat16

N_META = 16
CHUNK = 64
POOL_WINDOWS = (2, 4, 8, 16)
DA_HEADS = 8
REL_BUCKETS = 32
REL_MAX_DIST = 128
CONV_WIDTH = 31
RMS_EPS = 1e-6
LN_EPS = 1e-5
SUBLN_EPS = 1e-5
N_MIXERS = 3

LANES = 128
SUBLANES = 8
MXU_DIM = 256
VMEM_LIMIT_BYTES = 56 * 1024 * 1024

T_SEQ = 512
META_BLK = 128
ONES_ROWS = 16
COL_CHUNK = 2 * MXU_DIM
LEAD_ROWS = 32
POOL_HALO = 16
CONV_HALO = 32
CONV_ROWS = 32
NEG = -1e30
LOG2E = math.log2(math.e)
FAR_BUCKET = REL_BUCKETS // 2 - 1


def _ffn_chunks(f):
    step = 3 * MXU_DIM
    return [(s, min(s + step, f)) for s in range(0, f, step)]


def _rms(x, g, eps):
    ms = jnp.mean(x * x, axis=-1, keepdims=True)
    return (x * lax.rsqrt(ms + eps)) * g


def _ffn(hmid, g_ref, wg_ref, wu_ref, wd_ref):
    x = _rms(hmid, g_ref[...], RMS_EPS).astype(BF16)
    acc = None
    for f0, f1 in _ffn_chunks(wg_ref.shape[1]):
        gate = jnp.dot(x, wg_ref[:, f0:f1], preferred_element_type=F32)
        up = jnp.dot(x, wu_ref[:, f0:f1], preferred_element_type=F32)
        act = ((gate * jax.nn.sigmoid(gate)) * up).astype(BF16)
        part = jnp.dot(act, wd_ref[f0:f1, :], preferred_element_type=F32)
        acc = part if acc is None else acc + part
    return hmid + acc


def _positions(tile_idx, rows):
    r = lax.broadcasted_iota(jnp.int32, (rows, 1), 0)
    return tile_idx * rows + r - (T_SEQ - N_META)


def _pool_ffn_kernel(h_ref, gm_ref, pw_ref, pb_ref, ps_ref, gf_ref, wg_ref, wu_ref, wd_ref, *rest,
                     first, final):
    rest = list(rest)
    meta_ref = rest.pop(0) if first else None
    fg_ref = rest.pop(0) if final else None
    o_ref, halo_ref = rest
    i = pl.program_id(1)
    t, d = h_ref.shape[1:]

    def rows_of_tile(h, pos, halo):
        rows = h.shape[0]
        y = jnp.where(pos >= 0, _rms(h, gm_ref[...], RMS_EPS), 0.0)
        ext = jnp.concatenate([halo, y], axis=0)
        halo_ref[...] = y[rows - POOL_HALO:, :]
        group = d // len(POOL_WINDOWS)
        mixed = []
        for gi, win in enumerate(POOL_WINDOWS):
            c0 = gi * group
            s = ext[:, c0:c0 + group]
            shift = 1
            while shift < win:
                s = s + pltpu.roll(s, shift, axis=0)
                shift *= 2
            cnt = jnp.clip(pos + 1, 1, win).astype(F32)
            pooled = s[POOL_HALO:, :] / cnt - y[:, c0:c0 + group]
            mixed.append(jnp.dot(pooled.astype(BF16), pw_ref[gi], preferred_element_type=F32))
        mixed = jnp.concatenate(mixed, axis=1)
        hmid = h + (mixed + pb_ref[...]) * ps_ref[...]
        out = _ffn(hmid, gf_ref, wg_ref, wu_ref, wd_ref)
        if final:
            out = _rms(out, fg_ref[...], RMS_EPS)
        return out

    @pl.when(i == 0)
    def _():
        if first:
            h = jnp.concatenate([jnp.zeros((LEAD_ROWS - N_META, d), F32), meta_ref[...]], axis=0)
        else:
            h = h_ref[0, t - LEAD_ROWS:, :]
        pos = _positions(0, t)[t - LEAD_ROWS:]
        o_ref[0, 0:t - LEAD_ROWS, :] = jnp.zeros((t - LEAD_ROWS, d), F32)
        o_ref[0, t - LEAD_ROWS:, :] = rows_of_tile(h, pos, jnp.zeros((POOL_HALO, d), F32))

    @pl.when(i > 0)
    def _():
        o_ref[0] = rows_of_tile(h_ref[0], _positions(i, t), halo_ref[...])


def _const_spec(shape):
    nd = len(shape)
    return pl.BlockSpec(shape, lambda *_: (0,) * nd, pipeline_mode=pl.Buffered(1))


def _layer_spec(stack, layer):
    nd = stack.ndim - 1
    return pl.BlockSpec((None,) + stack.shape[1:], lambda *_: (layer,) + (0,) * nd,
                        pipeline_mode=pl.Buffered(1))


def _pool_ffn(h, gm, pw, pb, ps, gf, wg, wu, wd, meta=None, final_g=None):
    first, final = meta is not None, final_g is not None
    b, _, d = h.shape
    lh = h.shape[1] + T_SEQ if first else h.shape[1]
    nt = lh // T_SEQ
    tile = pl.BlockSpec((1, T_SEQ, d), lambda bi, i: (bi, i, 0))
    frames = pl.BlockSpec((1, T_SEQ, d), lambda bi, i: (bi, jnp.maximum(i - 1, 0), 0))
    in_specs = [frames if first else tile, _const_spec(gm.shape), _const_spec(pw.shape),
                _const_spec(pb.shape), _const_spec(ps.shape), _const_spec(gf.shape),
                _layer_spec(*wg), _layer_spec(*wu), _layer_spec(*wd)]
    args = [h, gm, pw, pb, ps, gf, wg[0], wu[0], wd[0]]
    if first:
        in_specs.append(_const_spec(meta.shape))
        args.append(meta)
    if final:
        in_specs.append(_const_spec(final_g.shape))
        args.append(final_g)
        out_spec = frames
        out_shape = jax.ShapeDtypeStruct((b, lh - T_SEQ, d), F32)
    else:
        out_spec = tile
        out_shape = jax.ShapeDtypeStruct((b, lh, d), F32)
    return pl.pallas_call(
        functools.partial(_pool_ffn_kernel, first=first, final=final),
        grid=(b, nt),
        in_specs=in_specs,
        out_specs=out_spec,
        out_shape=out_shape,
        scratch_shapes=[pltpu.VMEM((POOL_HALO, d), F32)],
        compiler_params=pltpu.CompilerParams(
            dimension_semantics=("arbitrary", "arbitrary"), vmem_limit_bytes=VMEM_LIMIT_BYTES),
        name="pool_ffn_final" if final else "pool_ffn",
    )(*args)


def _qkv_kernel(h_ref, g_ref, w_ref, q_ref, k_ref, vt_ref, *, qscale):
    h = h_ref[0]
    d = h.shape[1]
    y = _rms(h, g_ref[...], RMS_EPS).astype(BF16)
    q = jnp.dot(y, w_ref[:, 0:d], preferred_element_type=F32) * qscale
    k = jnp.dot(y, w_ref[:, d:2 * d], preferred_element_type=F32)
    v = jnp.dot(y, w_ref[:, 2 * d:3 * d], preferred_element_type=F32)
    q_ref[0] = q.astype(BF16)
    k_ref[0] = k.astype(BF16)
    vt = v.T.astype(BF16)
    heads, hv, rows = vt_ref.shape[2:]
    hd = d // heads
    for hh in range(heads):
        vt_ref[0, 0, hh, 0:hd, :] = vt[hh * hd:(hh + 1) * hd, :]
        vt_ref[0, 0, hh, hd:hv, :] = jnp.ones((hv - hd, rows), BF16)


def _qkv(h, g, w, qscale):
    b, lh, d = h.shape
    nt = lh // T_SEQ
    hv = d // DA_HEADS + ONES_ROWS
    tile = pl.BlockSpec((1, T_SEQ, d), lambda bi, i: (bi, i, 0))
    return pl.pallas_call(
        functools.partial(_qkv_kernel, qscale=qscale),
        grid=(b, nt),
        in_specs=[tile, _const_spec(g.shape), _const_spec(w.shape)],
        out_specs=[tile, tile,
                   pl.BlockSpec((1, 1, DA_HEADS, hv, T_SEQ), lambda bi, i: (bi, i, 0, 0, 0))],
        out_shape=[jax.ShapeDtypeStruct((b, lh, d), BF16), jax.ShapeDtypeStruct((b, lh, d), BF16),
                   jax.ShapeDtypeStruct((b, nt, DA_HEADS, hv, T_SEQ), BF16)],
        compiler_params=pltpu.CompilerParams(
            dimension_semantics=("arbitrary", "arbitrary"), vmem_limit_bytes=VMEM_LIMIT_BYTES),
        name="qkv",
    )(h, g, w)


def _t5_bucket(rel):
    half = REL_BUCKETS // 2
    max_exact = half // 2
    offset = jnp.where(rel > 0, half, 0)
    n = jnp.abs(rel)
    log_ratio = jnp.log(jnp.maximum(n, 1).astype(F32) / max_exact) / math.log(REL_MAX_DIST / max_exact)
    large = jnp.minimum(max_exact + (log_ratio * (half - max_exact)).astype(jnp.int32), half - 1)
    return offset + jnp.where(n < max_exact, n, large)


def _bias_buckets():
    t = T_SEQ
    kl = lax.broadcasted_iota(jnp.int32, (t, t), 0)
    ql = lax.broadcasted_iota(jnp.int32, (t, t), 1)
    prev = _t5_bucket(kl - t - ql)[t - META_BLK:]
    diag = jnp.where(kl // CHUNK <= ql // CHUNK, _t5_bucket(kl - ql), -1)
    kr = lax.broadcasted_iota(jnp.int32, (META_BLK, t), 0)
    qc = lax.broadcasted_iota(jnp.int32, (META_BLK, t), 1)
    mk = kr - (META_BLK - N_META)
    mq = qc - (t - N_META)
    lead = jnp.where(mq >= 0, _t5_bucket(mk - mq), FAR_BUCKET)
    first = _t5_bucket(mk - N_META - qc)
    later = jnp.full((META_BLK, t), FAR_BUCKET, jnp.int32)
    meta = [jnp.where(mk >= 0, m, -1) for m in (lead, first, later)]
    nb = t // META_BLK
    return jnp.concatenate([prev[None], diag.reshape(nb, META_BLK, t), jnp.stack(meta)], axis=0)


def _bias_kernel(tab_ref, bk_ref, o_ref):
    hh = pl.program_id(0)
    far = tab_ref[FAR_BUCKET, hh]
    for blk in range(bk_ref.shape[0]):
        b = bk_ref[blk]
        acc = jnp.zeros(b.shape, F32)
        for t in range(REL_BUCKETS):
            acc = jnp.where(b == t, tab_ref[t, hh] - far, acc)
        o_ref[0, blk] = jnp.where(b < 0, NEG, acc * LOG2E)


def _bias_tiles(table, buckets):
    nb, r, c = buckets.shape
    heads = table.shape[1]
    return pl.pallas_call(
        _bias_kernel,
        grid=(heads,),
        in_specs=[pl.BlockSpec(memory_space=pltpu.SMEM),
                  pl.BlockSpec((nb, r, c), lambda hh: (0, 0, 0))],
        out_specs=pl.BlockSpec((1, nb, r, c), lambda hh: (hh, 0, 0, 0)),
        out_shape=jax.ShapeDtypeStruct((heads, nb, r, c), F32),
        compiler_params=pltpu.CompilerParams(dimension_semantics=("arbitrary",)),
        name="rel_bias",
    )(table, buckets)


def _attn_kernel(q_ref, k_ref, vt_ref, bias_ref, lam_ref, sg_ref, o_ref,
                 qzz_ref, m_ref, mt_ref, acc_ref, sa_ref, sb_ref, *, lam_init):
    lv = lam_ref[...]
    lam = (jnp.exp(jnp.sum(lv[0:1] * lv[1:2], axis=1, keepdims=True))
           - jnp.exp(jnp.sum(lv[2:3] * lv[3:4], axis=1, keepdims=True)) + lam_init)

    def query_tile(a, carry):
        _attn_query_tile(a, lam, q_ref, k_ref, vt_ref, bias_ref, sg_ref, o_ref,
                         qzz_ref, m_ref, mt_ref, acc_ref, sa_ref, sb_ref, lam_init=lam_init)
        return carry

    n_tiles = q_ref.shape[1] // T_SEQ
    lax.fori_loop(0, n_tiles, query_tile, 0)
    _attn_finalize(n_tiles - 1, lam, acc_ref, sg_ref, o_ref, lam_init)


def _attn_query_tile(a, lam, q_ref, k_ref, vt_ref, bias_ref, sg_ref, o_ref,
                     qzz_ref, m_ref, mt_ref, acc_ref, sa_ref, sb_ref, *, lam_init):
    t = T_SEQ
    hd = q_ref.shape[2]
    nb = t // META_BLK
    rows = pl.ds(pl.multiple_of(a * t, t), t)
    bnear_ref = bias_ref

    qt = q_ref[0, rows, :].astype(F32).T
    row = lax.broadcasted_iota(jnp.int32, (hd, t), 0)
    q1 = jnp.where(row < hd // 2, qt, 0.0)
    q2 = jnp.where(row >= hd // 2, qt, 0.0)
    qzz_ref[...] = jnp.concatenate([q1, q2], axis=1).astype(BF16)

    def produce(j, buf):
        kt = k_ref[0, pl.ds(pl.multiple_of(j * t, t), t), :]
        mts = []
        for cs in col_chunks:
            s = jnp.dot(kt, qzz_ref[:, cs], preferred_element_type=F32)
            buf[:, cs] = s
            mts.append(jnp.max(s, axis=0, keepdims=True))
        return jnp.concatenate(mts, axis=1)

    def probs(s, m):
        return jnp.exp2(s - m).astype(BF16)

    def consume(buf, mt, vt):
        for cs in col_chunks:
            m_old = m_ref[:, cs]
            m_new = jnp.maximum(m_old, mt[:, cs])
            alpha = jnp.exp2(m_old - m_new)
            acc_ref[:, cs] = alpha * acc_ref[:, cs] + jnp.dot(
                vt, probs(buf[:, cs], m_new), preferred_element_type=F32)
            m_ref[:, cs] = m_new

    half = t // 2
    maps = (0, t)

    def produce_diag(j, buf):
        k0 = pl.multiple_of(j * t, t)
        k_top = k_ref[0, pl.ds(k0, half), :]
        k_bot = k_ref[0, pl.ds(k0 + half, half), :]
        for base in maps:
            buf[0:half, base:base + t] = jnp.dot(
                k_top, qzz_ref[:, base:base + t], preferred_element_type=F32)
            buf[half:t, base + half:base + t] = jnp.dot(
                k_bot, qzz_ref[:, base + half:base + t], preferred_element_type=F32)

    def consume_diag(buf, j):
        vt = vt_ref[0, j, 0]
        for base in maps:
            cs, hi = slice(base, base + t), slice(base + half, base + t)
            s_top = buf[0:half, cs] + bnear_ref[0, 1:1 + nb // 2].reshape(half, t)
            s_bot = buf[half:t, hi] + bnear_ref[0, 1 + nb // 2:1 + nb, :, half:t].reshape(half, half)
            mt_top = jnp.max(s_top, axis=0, keepdims=True)
            mt = jnp.concatenate(
                [mt_top[:, 0:half], jnp.maximum(mt_top[:, half:t], jnp.max(s_bot, axis=0, keepdims=True))],
                axis=1)
            m_old = m_ref[:, cs]
            m_new = jnp.maximum(m_old, mt)
            alpha = jnp.exp2(m_old - m_new)
            p_top = probs(s_top, m_new)
            p_bot = probs(s_bot, m_new[:, half:t])
            pv = jnp.dot(vt[:, 0:half], p_top, preferred_element_type=F32)
            pv_hi = pv[:, half:t] + jnp.dot(vt[:, half:t], p_bot, preferred_element_type=F32)
            acc_ref[:, cs] = alpha * acc_ref[:, cs] + jnp.concatenate([pv[:, 0:half], pv_hi], axis=1)
            m_ref[:, cs] = m_new

    def fix_prev(buf, mt):
        corner = bnear_ref[0, 0, :, 0:LANES]
        pieces, start = [], 0
        for base in (0, t):
            cs = slice(base, base + LANES)
            buf[t - META_BLK:, cs] = buf[t - META_BLK:, cs] + corner
            if base > start:
                pieces.append(mt[:, start:base])
            pieces.append(jnp.max(buf[:, cs], axis=0, keepdims=True))
            start = base + LANES
        return jnp.concatenate(pieces + [mt[:, start:]], axis=1)

    def init():
        s0 = jnp.dot(k_ref[0, t - N_META:t, :], qzz_ref[...], preferred_element_type=F32)
        bmeta = bias_ref[0, nb + 1 + jnp.minimum(a, 2), META_BLK - N_META:, :]
        s0 = s0 + jnp.concatenate([bmeta, bmeta], axis=1)
        m0 = jnp.max(s0, axis=0, keepdims=True)
        m_ref[...] = m0
        vt_meta = vt_ref[0, 0, 0, :, t - META_BLK:t][:, META_BLK - N_META:]
        acc_ref[...] = jnp.dot(vt_meta, probs(s0, m0), preferred_element_type=F32)

    col_chunks = [slice(c, c + COL_CHUNK) for c in range(0, 2 * t, COL_CHUNK)]
    odd = lax.rem(a, 2)

    def finish_previous():
        _attn_finalize(a - 1, lam, acc_ref, sg_ref, o_ref, lam_init)

    @pl.when(a == 0)
    def _():
        init()

    @pl.when(a == 1)
    def _():
        finish_previous()
        init()
        produce_diag(1, sb_ref)
        consume_diag(sb_ref, 1)

    @pl.when((a >= 2) & (odd == 1))
    def _():
        finish_previous()
        init()
        mtb = produce(1, sb_ref)
        mt_ref[...] = produce(2, sa_ref)
        consume(sb_ref, mtb, vt_ref[0, 1, 0])

    @pl.when((a >= 2) & (odd == 0))
    def _():
        finish_previous()
        init()
        mt_ref[...] = produce(1, sa_ref)

    @pl.when(a >= 2)
    def _():
        def pair(j, mta):
            mtb = produce(j + 1, sb_ref)
            consume(sa_ref, mta, vt_ref[0, j, 0])
            mta_next = produce(j + 2, sa_ref)
            consume(sb_ref, mtb, vt_ref[0, j + 1, 0])
            return mta_next

        n_pairs = (a - 2) // 2
        odd_pairs = lax.rem(n_pairs, 2)

        @pl.when(odd_pairs == 1)
        def _():
            mt_ref[...] = pair(1 + odd, mt_ref[...])

        n_quads = n_pairs // 2
        odd_quads = lax.rem(n_quads, 2)

        @pl.when(odd_quads == 1)
        def _():
            j = 1 + odd + 2 * odd_pairs
            mt_ref[...] = pair(j + 2, pair(j, mt_ref[...]))

        j0 = 1 + odd + 2 * odd_pairs + 4 * odd_quads

        def oct_body(i, mta):
            j = j0 + 8 * i
            for step in range(0, 8, 2):
                mta = pair(j + step, mta)
            return mta

        mta = lax.fori_loop(0, n_quads // 2, oct_body, mt_ref[...])
        produce_diag(a, sb_ref)
        consume(sa_ref, fix_prev(sa_ref, mta), vt_ref[0, a - 1, 0])
        consume_diag(sb_ref, a)


def _attn_finalize(a, lam, acc_ref, sg_ref, o_ref, lam_init):
    t = T_SEQ
    hd = o_ref.shape[2]
    on = acc_ref[0:hd, :] * (1.0 / acc_ref[hd:hd + 1, :])
    o = on[:, :t] - lam * on[:, t:]
    o = (o * lax.rsqrt(jnp.mean(o * o, axis=0, keepdims=True) + SUBLN_EPS)) * sg_ref[...]
    o = o * (1.0 - lam_init)
    o_ref[0, pl.ds(pl.multiple_of(a * t, t), t), :] = o.T.astype(BF16)


def _attention(q, k, vt, bias, lamv, sg, lam_init):
    b, lh, d = q.shape
    nt = lh // T_SEQ
    hd = d // DA_HEADS
    hv = vt.shape[3]
    seq = pl.BlockSpec((1, lh, hd), lambda bi, hh: (bi, 0, hh))
    return pl.pallas_call(
        functools.partial(_attn_kernel, lam_init=lam_init),
        grid=(b, DA_HEADS),
        in_specs=[
            seq,
            seq,
            pl.BlockSpec((1, nt, 1, hv, T_SEQ), lambda bi, hh: (bi, 0, hh, 0, 0)),
            pl.BlockSpec((1,) + bias.shape[1:], lambda bi, hh: (hh, 0, 0, 0)),
            pl.BlockSpec(lamv.shape, lambda bi, hh: (0, 0)),
            pl.BlockSpec(sg.shape, lambda bi, hh: (0, 0)),
        ],
        out_specs=seq,
        out_shape=jax.ShapeDtypeStruct((b, lh, d), BF16),
        scratch_shapes=[pltpu.VMEM((hd, 2 * T_SEQ), BF16), pltpu.VMEM((1, 2 * T_SEQ), F32),
                        pltpu.VMEM((1, 2 * T_SEQ), F32), pltpu.VMEM((hv, 2 * T_SEQ), F32),
                        pltpu.VMEM((T_SEQ, 2 * T_SEQ), F32), pltpu.VMEM((T_SEQ, 2 * T_SEQ), F32)],
        compiler_params=pltpu.CompilerParams(
            dimension_semantics=("arbitrary", "arbitrary"), vmem_limit_bytes=VMEM_LIMIT_BYTES),
        name="diff_attn",
    )(q, k, vt, bias, lamv, sg)


def _attn_out_ffn_kernel(h_ref, o_ref, wo_ref, gf_ref, wg_ref, wu_ref, wd_ref, out_ref, *,
                         tiles_per_seq):
    t, d = h_ref.shape

    def rows_of_tile(h, o):
        hmid = h + jnp.dot(o, wo_ref[...], preferred_element_type=F32)
        return _ffn(hmid, gf_ref, wg_ref, wu_ref, wd_ref)

    lead = lax.rem(pl.program_id(0), tiles_per_seq) == 0

    @pl.when(lead)
    def _():
        out_ref[0:t - LEAD_ROWS, :] = jnp.zeros((t - LEAD_ROWS, d), F32)
        out_ref[t - LEAD_ROWS:, :] = rows_of_tile(h_ref[t - LEAD_ROWS:, :], o_ref[t - LEAD_ROWS:, :])

    @pl.when(jnp.logical_not(lead))
    def _():
        out_ref[...] = rows_of_tile(h_ref[...], o_ref[...])


def _attn_out_ffn(h, o, wo, gf, wg, wu, wd):
    b, lh, d = h.shape
    rows = b * lh
    tile_f32 = pl.BlockSpec((T_SEQ, d), lambda i: (i, 0))
    out = pl.pallas_call(
        functools.partial(_attn_out_ffn_kernel, tiles_per_seq=lh // T_SEQ),
        grid=(rows // T_SEQ,),
        in_specs=[tile_f32, pl.BlockSpec((T_SEQ, d), lambda i: (i, 0)), _const_spec(wo.shape),
                  _const_spec(gf.shape), _layer_spec(*wg), _layer_spec(*wu), _layer_spec(*wd)],
        out_specs=tile_f32,
        out_shape=jax.ShapeDtypeStruct((rows, d), F32),
        compiler_params=pltpu.CompilerParams(
            dimension_semantics=("arbitrary",), vmem_limit_bytes=VMEM_LIMIT_BYTES),
        name="attn_out_ffn",
    )(h.reshape(rows, d), o.reshape(rows, d), wo, gf, wg[0], wu[0], wd[0])
    return out.reshape(b, lh, d)


def _conv_ffn_kernel(h_ref, gm_ref, win_ref, bin_ref, dww_ref, dwb_ref, lng_ref, lnb_ref,
                     wout_ref, bout_ref, gf_ref, wg_ref, wu_ref, wd_ref, o_ref, uext_ref, conv_ref):
    i = pl.program_id(1)
    t, d = h_ref.shape[1:]

    @pl.when(i == 0)
    def _():
        uext_ref[0:CONV_HALO, :] = jnp.zeros((CONV_HALO, d), F32)
        for rows in (LEAD_ROWS, t):
            uext_ref[CONV_HALO + rows:CONV_HALO + rows + SUBLANES, :] = jnp.zeros((SUBLANES, d), F32)
        o_ref[0, 0:t - LEAD_ROWS, :] = jnp.zeros((t - LEAD_ROWS, d), F32)
        o_ref[0, t - LEAD_ROWS:, :] = _conv_ffn_rows(
            h_ref[0, t - LEAD_ROWS:, :], _positions(0, t)[t - LEAD_ROWS:], gm_ref, win_ref, bin_ref,
            dww_ref, dwb_ref, lng_ref, lnb_ref, wout_ref, bout_ref, gf_ref, wg_ref, wu_ref, wd_ref,
            uext_ref, conv_ref)

    @pl.when(i > 0)
    def _():
        o_ref[0] = _conv_ffn_rows(
            h_ref[0], _positions(i, t), gm_ref, win_ref, bin_ref, dww_ref, dwb_ref, lng_ref, lnb_ref,
            wout_ref, bout_ref, gf_ref, wg_ref, wu_ref, wd_ref, uext_ref, conv_ref)


def _conv_ffn_rows(h, pos, gm_ref, win_ref, bin_ref, dww_ref, dwb_ref, lng_ref, lnb_ref,
                   wout_ref, bout_ref, gf_ref, wg_ref, wu_ref, wd_ref, uext_ref, conv_ref):
    rows, d = h.shape
    y = _rms(h, gm_ref[...], RMS_EPS).astype(BF16)
    ag = jnp.dot(y, win_ref[...], preferred_element_type=F32) + bin_ref[...]
    u = ag[:, :d] * jax.nn.sigmoid(ag[:, d:])
    u = jnp.where(pos >= 0, u, 0.0)
    uext_ref[CONV_HALO:CONV_HALO + rows, :] = u

    base = CONV_HALO - (CONV_WIDTH - 1)
    span = CONV_ROWS + SUBLANES
    n_oct = (base + CONV_WIDTH - 1) // SUBLANES + 1

    def conv_body(c, carry):
        r0 = pl.multiple_of(c * CONV_ROWS, CONV_ROWS)
        for lb in range(d // LANES):
            cs = slice(lb * LANES, (lb + 1) * LANES)
            um = [uext_ref[pl.ds(r0 + SUBLANES * m, span), cs] for m in range(n_oct)]
            out = None
            for sh in range(SUBLANES):
                part = None
                for m in range(n_oct):
                    j = SUBLANES * m + sh - base
                    if 0 <= j < CONV_WIDTH:
                        term = dww_ref[j:j + 1, cs] * um[m]
                        part = term if part is None else part + term
                if sh:
                    part = pltpu.roll(part, span - sh, axis=0)
                out = part if out is None else out + part
            conv_ref[pl.ds(r0, CONV_ROWS), cs] = out[:CONV_ROWS] + dwb_ref[:, cs]
        return carry

    lax.fori_loop(0, rows // CONV_ROWS, conv_body, 0)
    uext_ref[0:CONV_HALO, :] = uext_ref[rows:rows + CONV_HALO, :]

    c = conv_ref[0:rows, :]
    mu = jnp.mean(c, axis=-1, keepdims=True)
    cc = c - mu
    var = jnp.mean(cc * cc, axis=-1, keepdims=True)
    z = (cc * lax.rsqrt(var + LN_EPS)) * lng_ref[...] + lnb_ref[...]
    z = (z * jax.nn.sigmoid(z)).astype(BF16)
    hmid = h + jnp.dot(z, wout_ref[...], preferred_element_type=F32) + bout_ref[...]
    return _ffn(hmid, gf_ref, wg_ref, wu_ref, wd_ref)


def _conv_ffn(h, gm, win, bin_, dww, dwb, lng, lnb, wout, bout, gf, wg, wu, wd):
    b, lh, d = h.shape
    nt = lh // T_SEQ
    tile = pl.BlockSpec((1, T_SEQ, d), lambda bi, i: (bi, i, 0))
    consts = [gm, win, bin_, dww, dwb, lng, lnb, wout, bout, gf]
    ffn = [wg, wu, wd]
    return pl.pallas_call(
        _conv_ffn_kernel,
        grid=(b, nt),
        in_specs=[tile] + [_const_spec(c.shape) for c in consts] + [_layer_spec(*w) for w in ffn],
        out_specs=tile,
        out_shape=jax.ShapeDtypeStruct((b, lh, d), F32),
        scratch_shapes=[pltpu.VMEM((T_SEQ + CONV_HALO + SUBLANES, d), F32), pltpu.VMEM((T_SEQ, d), F32)],
        compiler_params=pltpu.CompilerParams(
            dimension_semantics=("arbitrary", "arbitrary"), vmem_limit_bytes=VMEM_LIMIT_BYTES),
        name="conv_ffn",
    )(h, *consts, *(w[0] for w in ffn))


def kernel(x, meta_tokens, norm_mix_g, norm_ffn_g, final_norm_g, rel_bias_table, pool_w, pool_b, pool_scale, attn_w_qkv, attn_w_o, attn_lambda_q1, attn_lambda_k1, attn_lambda_q2, attn_lambda_k2, attn_subln_g, conv_w_in, conv_b_in, conv_dw_w, conv_dw_b, conv_ln_g, conv_ln_b, conv_w_out, conv_b_out, ffn_w_gate, ffn_w_up, ffn_w_down):
    b, seq, d = x.shape
    depth = norm_mix_g.shape[0]
    assert seq % T_SEQ == 0 and d % (LANES * len(POOL_WINDOWS)) == 0
    assert depth % N_MIXERS == 1, "the last layer must be a pooling layer (it applies the final norm)"

    row = lambda v: v.reshape(1, -1).astype(F32)
    h = x.astype(F32)
    bias = None
    ffn_stacks = [w.astype(BF16) for w in (ffn_w_gate, ffn_w_up, ffn_w_down)]
    for i in range(depth):
        mixer, j = i % N_MIXERS, i // N_MIXERS
        gm, gf = row(norm_mix_g[i]), row(norm_ffn_g[i])
        wg, wu, wd = ((w, i) for w in ffn_stacks)
        if mixer == 0:
            h = _pool_ffn(h, gm, pool_w[j].astype(BF16), row(pool_b[j]), row(pool_scale[j]),
                          gf, wg, wu, wd,
                          meta=meta_tokens.astype(F32) if i == 0 else None,
                          final_g=row(final_norm_g) if i == depth - 1 else None)
        elif mixer == 1:
            lam_init = 0.8 - 0.6 * math.exp(-0.3 * i)
            hd = d // DA_HEADS
            if bias is None:
                bias = _bias_tiles(rel_bias_table.astype(F32), _bias_buckets())
            qscale = (hd // 2) ** -0.5 * LOG2E
            q, k, vt = _qkv(h, gm, attn_w_qkv[j].astype(BF16), qscale)
            lamv = jnp.stack([attn_lambda_q1[j], attn_lambda_k1[j],
                              attn_lambda_q2[j], attn_lambda_k2[j]]).astype(F32)
            o = _attention(q, k, vt, bias, lamv, attn_subln_g[j].reshape(hd, 1).astype(F32), lam_init)
            h = _attn_out_ffn(h, o, attn_w_o[j].astype(BF16), gf, wg, wu, wd)
        else:
            h = _conv_ffn(h, gm, conv_w_in[j].astype(BF16), row(conv_b_in[j]),
                          conv_dw_w[j].astype(F32), row(conv_dw_b[j]), row(conv_ln_g[j]),
                          row(conv_ln_b[j]), conv_w_out[j].astype(BF16), row(conv_b_out[j]),
                          gf, wg, wu, wd)
    return h
```

```python
import functools
import math

import jax
import jax.numpy as jnp
from jax import lax
from jax.experimental import pallas as pl
from jax.experimental.pallas import tpu as pltpu

F32 = jnp.float32
BF16 = jnp.bfloat16

N_META = 16
CHUNK = 64
POOL_WINDOWS = (2, 4, 8, 16)
DA_HEADS = 8
REL_BUCKETS = 32
REL_MAX_DIST = 128
CONV_WIDTH = 31
RMS_EPS = 1e-6
LN_EPS = 1e-5
SUBLN_EPS = 1e-5
N_MIXERS = 3

LANES = 128
SUBLANES = 8
MXU_DIM = 256
VMEM_LIMIT_BYTES = 56 * 1024 * 1024

T_SEQ = 512
META_BLK = 128
ONES_ROWS = 16
COL_CHUNK = 2 * MXU_DIM
LEAD_ROWS = 32
POOL_HALO = 16
CONV_HALO = 32
CONV_ROWS = 128
NEG = -1e30
LOG2E = math.log2(math.e)
FAR_BUCKET = REL_BUCKETS // 2 - 1


def _ffn_chunks(f):
    step = 3 * MXU_DIM
    return [(s, min(s + step, f)) for s in range(0, f, step)]


def _rms(x, g, eps):
    ms = jnp.mean(x * x, axis=-1, keepdims=True)
    return (x * lax.rsqrt(ms + eps)) * g


def _ffn(hmid, g_ref, wg_ref, wu_ref, wd_ref):
    x = _rms(hmid, g_ref[...], RMS_EPS).astype(BF16)
    acc = None
    for f0, f1 in _ffn_chunks(wg_ref.shape[1]):
        gate = jnp.dot(x, wg_ref[:, f0:f1], preferred_element_type=F32)
        up = jnp.dot(x, wu_ref[:, f0:f1], preferred_element_type=F32)
        act = ((gate * jax.nn.sigmoid(gate)) * up).astype(BF16)
        part = jnp.dot(act, wd_ref[f0:f1, :], preferred_element_type=F32)
        acc = part if acc is None else acc + part
    return hmid + acc


def _positions(tile_idx, rows):
    r = lax.broadcasted_iota(jnp.int32, (rows, 1), 0)
    return tile_idx * rows + r - (T_SEQ - N_META)


def _pool_ffn_kernel(h_ref, gm_ref, pw_ref, pb_ref, ps_ref, gf_ref, wg_ref, wu_ref, wd_ref, *rest,
                     first, final):
    rest = list(rest)
    meta_ref = rest.pop(0) if first else None
    fg_ref = rest.pop(0) if final else None
    o_ref, halo_ref = rest
    i = pl.program_id(1)
    t, d = h_ref.shape[1:]

    def rows_of_tile(h, pos, halo):
        rows = h.shape[0]
        y = jnp.where(pos >= 0, _rms(h, gm_ref[...], RMS_EPS), 0.0)
        ext = jnp.concatenate([halo, y], axis=0)
        halo_ref[...] = y[rows - POOL_HALO:, :]
        group = d // len(POOL_WINDOWS)
        mixed = []
        for gi, win in enumerate(POOL_WINDOWS):
            c0 = gi * group
            s = ext[:, c0:c0 + group]
            shift = 1
            while shift < win:
                s = s + pltpu.roll(s, shift, axis=0)
                shift *= 2
            cnt = jnp.clip(pos + 1, 1, win).astype(F32)
            pooled = s[POOL_HALO:, :] / cnt - y[:, c0:c0 + group]
            mixed.append(jnp.dot(pooled.astype(BF16), pw_ref[gi], preferred_element_type=F32))
        mixed = jnp.concatenate(mixed, axis=1)
        hmid = h + (mixed + pb_ref[...]) * ps_ref[...]
        out = _ffn(hmid, gf_ref, wg_ref, wu_ref, wd_ref)
        if final:
            out = _rms(out, fg_ref[...], RMS_EPS)
        return out

    @pl.when(i == 0)
    def _():
        if first:
            h = jnp.concatenate([jnp.zeros((LEAD_ROWS - N_META, d), F32), meta_ref[...]], axis=0)
        else:
            h = h_ref[0, t - LEAD_ROWS:, :]
        pos = _positions(0, t)[t - LEAD_ROWS:]
        o_ref[0, 0:t - LEAD_ROWS, :] = jnp.zeros((t - LEAD_ROWS, d), F32)
        o_ref[0, t - LEAD_ROWS:, :] = rows_of_tile(h, pos, jnp.zeros((POOL_HALO, d), F32))

    @pl.when(i > 0)
    def _():
        o_ref[0] = rows_of_tile(h_ref[0], _positions(i, t), halo_ref[...])


def _const_spec(shape):
    nd = len(shape)
    return pl.BlockSpec(shape, lambda *_: (0,) * nd, pipeline_mode=pl.Buffered(1))


def _layer_spec(stack, layer):
    nd = stack.ndim - 1
    return pl.BlockSpec((None,) + stack.shape[1:], lambda *_: (layer,) + (0,) * nd,
                        pipeline_mode=pl.Buffered(1))


def _pool_ffn(h, gm, pw, pb, ps, gf, wg, wu, wd, meta=None, final_g=None):
    first, final = meta is not None, final_g is not None
    b, _, d = h.shape
    lh = h.shape[1] + T_SEQ if first else h.shape[1]
    nt = lh // T_SEQ
    tile = pl.BlockSpec((1, T_SEQ, d), lambda bi, i: (bi, i, 0))
    frames = pl.BlockSpec((1, T_SEQ, d), lambda bi, i: (bi, jnp.maximum(i - 1, 0), 0))
    in_specs = [frames if first else tile, _const_spec(gm.shape), _const_spec(pw.shape),
                _const_spec(pb.shape), _const_spec(ps.shape), _const_spec(gf.shape),
                _layer_spec(*wg), _layer_spec(*wu), _layer_spec(*wd)]
    args = [h, gm, pw, pb, ps, gf, wg[0], wu[0], wd[0]]
    if first:
        in_specs.append(_const_spec(meta.shape))
        args.append(meta)
    if final:
        in_specs.append(_const_spec(final_g.shape))
        args.append(final_g)
        out_spec = frames
        out_shape = jax.ShapeDtypeStruct((b, lh - T_SEQ, d), F32)
    else:
        out_spec = tile
        out_shape = jax.ShapeDtypeStruct((b, lh, d), F32)
    return pl.pallas_call(
        functools.partial(_pool_ffn_kernel, first=first, final=final),
        grid=(b, nt),
        in_specs=in_specs,
        out_specs=out_spec,
        out_shape=out_shape,
        scratch_shapes=[pltpu.VMEM((POOL_HALO, d), F32)],
        compiler_params=pltpu.CompilerParams(
            dimension_semantics=("arbitrary", "arbitrary"), vmem_limit_bytes=VMEM_LIMIT_BYTES),
        name="pool_ffn_final" if final else "pool_ffn",
    )(*args)


def _qkv_kernel(h_ref, g_ref, w_ref, q_ref, k_ref, vt_ref, *, qscale):
    h = h_ref[0]
    d = h.shape[1]
    y = _rms(h, g_ref[...], RMS_EPS).astype(BF16)
    q = jnp.dot(y, w_ref[:, 0:d], preferred_element_type=F32) * qscale
    k = jnp.dot(y, w_ref[:, d:2 * d], preferred_element_type=F32)
    v = jnp.dot(y, w_ref[:, 2 * d:3 * d], preferred_element_type=F32)
    q_ref[0] = q.astype(BF16)
    k_ref[0] = k.astype(BF16)
    vt = v.T.astype(BF16)
    heads, hv, rows = vt_ref.shape[2:]
    hd = d // heads
    for hh in range(heads):
        vt_ref[0, 0, hh, 0:hd, :] = vt[hh * hd:(hh + 1) * hd, :]
        vt_ref[0, 0, hh, hd:hv, :] = jnp.ones((hv - hd, rows), BF16)


def _qkv(h, g, w, qscale):
    b, lh, d = h.shape
    nt = lh // T_SEQ
    hv = d // DA_HEADS + ONES_ROWS
    tile = pl.BlockSpec((1, T_SEQ, d), lambda bi, i: (bi, i, 0))
    return pl.pallas_call(
        functools.partial(_qkv_kernel, qscale=qscale),
        grid=(b, nt),
        in_specs=[tile, _const_spec(g.shape), _const_spec(w.shape)],
        out_specs=[tile, tile,
                   pl.BlockSpec((1, 1, DA_HEADS, hv, T_SEQ), lambda bi, i: (bi, i, 0, 0, 0))],
        out_shape=[jax.ShapeDtypeStruct((b, lh, d), BF16), jax.ShapeDtypeStruct((b, lh, d), BF16),
                   jax.ShapeDtypeStruct((b, nt, DA_HEADS, hv, T_SEQ), BF16)],
        compiler_params=pltpu.CompilerParams(
            dimension_semantics=("arbitrary", "arbitrary"), vmem_limit_bytes=VMEM_LIMIT_BYTES),
        name="qkv",
    )(h, g, w)


def _t5_bucket(rel):
    half = REL_BUCKETS // 2
    max_exact = half // 2
    offset = jnp.where(rel > 0, half, 0)
    n = jnp.abs(rel)
    log_ratio = jnp.log(jnp.maximum(n, 1).astype(F32) / max_exact) / math.log(REL_MAX_DIST / max_exact)
    large = jnp.minimum(max_exact + (log_ratio * (half - max_exact)).astype(jnp.int32), half - 1)
    return offset + jnp.where(n < max_exact, n, large)


def _bias_buckets():
    t = T_SEQ
    kl = lax.broadcasted_iota(jnp.int32, (t, t), 0)
    ql = lax.broadcasted_iota(jnp.int32, (t, t), 1)
    prev = _t5_bucket(kl - t - ql)[t - META_BLK:]
    diag = jnp.where(kl // CHUNK <= ql // CHUNK, _t5_bucket(kl - ql), -1)
    kr = lax.broadcasted_iota(jnp.int32, (META_BLK, t), 0)
    qc = lax.broadcasted_iota(jnp.int32, (META_BLK, t), 1)
    mk = kr - (META_BLK - N_META)
    mq = qc - (t - N_META)
    lead = jnp.where(mq >= 0, _t5_bucket(mk - mq), FAR_BUCKET)
    first = _t5_bucket(mk - N_META - qc)
    later = jnp.full((META_BLK, t), FAR_BUCKET, jnp.int32)
    meta = [jnp.where(mk >= 0, m, -1) for m in (lead, first, later)]
    nb = t // META_BLK
    return jnp.concatenate([prev[None], diag.reshape(nb, META_BLK, t), jnp.stack(meta)], axis=0)


def _bias_kernel(tab_ref, bk_ref, o_ref):
    hh = pl.program_id(0)
    far = tab_ref[FAR_BUCKET, hh]
    for blk in range(bk_ref.shape[0]):
        b = bk_ref[blk]
        acc = jnp.zeros(b.shape, F32)
        for t in range(REL_BUCKETS):
            acc = jnp.where(b == t, tab_ref[t, hh] - far, acc)
        o_ref[0, blk] = jnp.where(b < 0, NEG, acc * LOG2E)


def _bias_tiles(table, buckets):
    nb, r, c = buckets.shape
    heads = table.shape[1]
    return pl.pallas_call(
        _bias_kernel,
        grid=(heads,),
        in_specs=[pl.BlockSpec(memory_space=pltpu.SMEM),
                  pl.BlockSpec((nb, r, c), lambda hh: (0, 0, 0))],
        out_specs=pl.BlockSpec((1, nb, r, c), lambda hh: (hh, 0, 0, 0)),
        out_shape=jax.ShapeDtypeStruct((heads, nb, r, c), F32),
        compiler_params=pltpu.CompilerParams(dimension_semantics=("arbitrary",)),
        name="rel_bias",
    )(table, buckets)


def _attn_kernel(q_ref, k_ref, vt_ref, bias_ref, lam_ref, sg_ref, o_ref,
                 qzz_ref, m_ref, mt_ref, acc_ref, sa_ref, sb_ref, *, lam_init):
    lv = lam_ref[...]
    lam = (jnp.exp(jnp.sum(lv[0:1] * lv[1:2], axis=1, keepdims=True))
           - jnp.exp(jnp.sum(lv[2:3] * lv[3:4], axis=1, keepdims=True)) + lam_init)

    def query_tile(a, carry):
        _attn_query_tile(a, lam, q_ref, k_ref, vt_ref, bias_ref, sg_ref, o_ref,
                         qzz_ref, m_ref, mt_ref, acc_ref, sa_ref, sb_ref, lam_init=lam_init)
        return carry

    n_tiles = q_ref.shape[1] // T_SEQ
    lax.fori_loop(0, n_tiles, query_tile, 0)
    _attn_finalize(n_tiles - 1, lam, acc_ref, sg_ref, o_ref, lam_init)


def _attn_query_tile(a, lam, q_ref, k_ref, vt_ref, bias_ref, sg_ref, o_ref,
                     qzz_ref, m_ref, mt_ref, acc_ref, sa_ref, sb_ref, *, lam_init):
    t = T_SEQ
    hd = q_ref.shape[2]
    nb = t // META_BLK
    rows = pl.ds(pl.multiple_of(a * t, t), t)
    bnear_ref = bias_ref

    qt = q_ref[0, rows, :].astype(F32).T
    row = lax.broadcasted_iota(jnp.int32, (hd, t), 0)
    q1 = jnp.where(row < hd // 2, qt, 0.0)
    q2 = jnp.where(row >= hd // 2, qt, 0.0)
    qzz_ref[...] = jnp.concatenate([q1, q2], axis=1).astype(BF16)

    def produce(j, buf):
        kt = k_ref[0, pl.ds(pl.multiple_of(j * t, t), t), :]
        mts = []
        for cs in col_chunks:
            s = jnp.dot(kt, qzz_ref[:, cs], preferred_element_type=F32)
            buf[:, cs] = s
            mts.append(jnp.max(s, axis=0, keepdims=True))
        return jnp.concatenate(mts, axis=1)

    def probs(s, m):
        return jnp.exp2(s - m).astype(BF16)

    def consume(buf, mt, vt):
        for cs in col_chunks:
            m_old = m_ref[:, cs]
            m_new = jnp.maximum(m_old, mt[:, cs])
            alpha = jnp.exp2(m_old - m_new)
            acc_ref[:, cs] = alpha * acc_ref[:, cs] + jnp.dot(
                vt, probs(buf[:, cs], m_new), preferred_element_type=F32)
            m_ref[:, cs] = m_new

    half = t // 2
    maps = (0, t)

    def produce_diag(j, buf):
        k0 = pl.multiple_of(j * t, t)
        k_top = k_ref[0, pl.ds(k0, half), :]
        k_bot = k_ref[0, pl.ds(k0 + half, half), :]
        for base in maps:
            buf[0:half, base:base + t] = jnp.dot(
                k_top, qzz_ref[:, base:base + t], preferred_element_type=F32)
            buf[half:t, base + half:base + t] = jnp.dot(
                k_bot, qzz_ref[:, base + half:base + t], preferred_element_type=F32)

    def consume_diag(buf, j):
        vt = vt_ref[0, j, 0]
        for base in maps:
            cs, hi = slice(base, base + t), slice(base + half, base + t)
            s_top = buf[0:half, cs] + bnear_ref[0, 1:1 + nb // 2].reshape(half, t)
            s_bot = buf[half:t, hi] + bnear_ref[0, 1 + nb // 2:1 + nb, :, half:t].reshape(half, half)
            mt_top = jnp.max(s_top, axis=0, keepdims=True)
            mt = jnp.concatenate(
                [mt_top[:, 0:half], jnp.maximum(mt_top[:, half:t], jnp.max(s_bot, axis=0, keepdims=True))],
                axis=1)
            m_old = m_ref[:, cs]
            m_new = jnp.maximum(m_old, mt)
            alpha = jnp.exp2(m_old - m_new)
            p_top = probs(s_top, m_new)
            p_bot = probs(s_bot, m_new[:, half:t])
            pv = jnp.dot(vt[:, 0:half], p_top, preferred_element_type=F32)
            pv_hi = pv[:, half:t] + jnp.dot(vt[:, half:t], p_bot, preferred_element_type=F32)
            acc_ref[:, cs] = alpha * acc_ref[:, cs] + jnp.concatenate([pv[:, 0:half], pv_hi], axis=1)
            m_ref[:, cs] = m_new

    def fix_prev(buf, mt):
        corner = bnear_ref[0, 0, :, 0:LANES]
        pieces, start = [], 0
        for base in (0, t):
            cs = slice(base, base + LANES)
            buf[t - META_BLK:, cs] = buf[t - META_BLK:, cs] + corner
            if base > start:
                pieces.append(mt[:, start:base])
            pieces.append(jnp.max(buf[:, cs], axis=0, keepdims=True))
            start = base + LANES
        return jnp.concatenate(pieces + [mt[:, start:]], axis=1)

    def init():
        s0 = jnp.dot(k_ref[0, t - N_META:t, :], qzz_ref[...], preferred_element_type=F32)
        bmeta = bias_ref[0, nb + 1 + jnp.minimum(a, 2), META_BLK - N_META:, :]
        s0 = s0 + jnp.concatenate([bmeta, bmeta], axis=1)
        m0 = jnp.max(s0, axis=0, keepdims=True)
        m_ref[...] = m0
        vt_meta = vt_ref[0, 0, 0, :, t - META_BLK:t][:, META_BLK - N_META:]
        acc_ref[...] = jnp.dot(vt_meta, probs(s0, m0), preferred_element_type=F32)

    col_chunks = [slice(c, c + COL_CHUNK) for c in range(0, 2 * t, COL_CHUNK)]
    odd = lax.rem(a, 2)

    def finish_previous():
        _attn_finalize(a - 1, lam, acc_ref, sg_ref, o_ref, lam_init)

    @pl.when(a == 0)
    def _():
        init()

    @pl.when(a == 1)
    def _():
        finish_previous()
        init()
        produce_diag(1, sb_ref)
        consume_diag(sb_ref, 1)

    @pl.when((a >= 2) & (odd == 1))
    def _():
        finish_previous()
        init()
        mtb = produce(1, sb_ref)
        mt_ref[...] = produce(2, sa_ref)
        consume(sb_ref, mtb, vt_ref[0, 1, 0])

    @pl.when((a >= 2) & (odd == 0))
    def _():
        finish_previous()
        init()
        mt_ref[...] = produce(1, sa_ref)

    @pl.when(a >= 2)
    def _():
        def pair(j, mta):
            mtb = produce(j + 1, sb_ref)
            consume(sa_ref, mta, vt_ref[0, j, 0])
            mta_next = produce(j + 2, sa_ref)
            consume(sb_ref, mtb, vt_ref[0, j + 1, 0])
            return mta_next

        n_pairs = (a - 2) // 2
        odd_pairs = lax.rem(n_pairs, 2)

        @pl.when(odd_pairs == 1)
        def _():
            mt_ref[...] = pair(1 + odd, mt_ref[...])

        n_quads = n_pairs // 2
        odd_quads = lax.rem(n_quads, 2)

        @pl.when(odd_quads == 1)
        def _():
            j = 1 + odd + 2 * odd_pairs
            mt_ref[...] = pair(j + 2, pair(j, mt_ref[...]))

        j0 = 1 + odd + 2 * odd_pairs + 4 * odd_quads

        def oct_body(i, mta):
            j = j0 + 8 * i
            for step in range(0, 8, 2):
                mta = pair(j + step, mta)
            return mta

        mta = lax.fori_loop(0, n_quads // 2, oct_body, mt_ref[...])
        produce_diag(a, sb_ref)
        consume(sa_ref, fix_prev(sa_ref, mta), vt_ref[0, a - 1, 0])
        consume_diag(sb_ref, a)


def _attn_finalize(a, lam, acc_ref, sg_ref, o_ref, lam_init):
    t = T_SEQ
    hd = o_ref.shape[2]
    on = acc_ref[0:hd, :] * (1.0 / acc_ref[hd:hd + 1, :])
    o = on[:, :t] - lam * on[:, t:]
    o = (o * lax.rsqrt(jnp.mean(o * o, axis=0, keepdims=True) + SUBLN_EPS)) * sg_ref[...]
    o = o * (1.0 - lam_init)
    o_ref[0, pl.ds(pl.multiple_of(a * t, t), t), :] = o.T.astype(BF16)


def _attention(q, k, vt, bias, lamv, sg, lam_init):
    b, lh, d = q.shape
    nt = lh // T_SEQ
    hd = d // DA_HEADS
    hv = vt.shape[3]
    seq = pl.BlockSpec((1, lh, hd), lambda bi, hh: (bi, 0, hh))
    return pl.pallas_call(
        functools.partial(_attn_kernel, lam_init=lam_init),
        grid=(b, DA_HEADS),
        in_specs=[
            seq,
            seq,
            pl.BlockSpec((1, nt, 1, hv, T_SEQ), lambda bi, hh: (bi, 0, hh, 0, 0)),
            pl.BlockSpec((1,) + bias.shape[1:], lambda bi, hh: (hh, 0, 0, 0)),
            pl.BlockSpec(lamv.shape, lambda bi, hh: (0, 0)),
            pl.BlockSpec(sg.shape, lambda bi, hh: (0, 0)),
        ],
        out_specs=seq,
        out_shape=jax.ShapeDtypeStruct((b, lh, d), BF16),
        scratch_shapes=[pltpu.VMEM((hd, 2 * T_SEQ), BF16), pltpu.VMEM((1, 2 * T_SEQ), F32),
                        pltpu.VMEM((1, 2 * T_SEQ), F32), pltpu.VMEM((hv, 2 * T_SEQ), F32),
                        pltpu.VMEM((T_SEQ, 2 * T_SEQ), F32), pltpu.VMEM((T_SEQ, 2 * T_SEQ), F32)],
        compiler_params=pltpu.CompilerParams(
            dimension_semantics=("arbitrary", "arbitrary"), vmem_limit_bytes=VMEM_LIMIT_BYTES),
        name="diff_attn",
    )(q, k, vt, bias, lamv, sg)


def _attn_out_ffn_kernel(h_ref, o_ref, wo_ref, gf_ref, wg_ref, wu_ref, wd_ref, out_ref, *,
                         tiles_per_seq):
    t, d = h_ref.shape

    def rows_of_tile(h, o):
        hmid = h + jnp.dot(o, wo_ref[...], preferred_element_type=F32)
        return _ffn(hmid, gf_ref, wg_ref, wu_ref, wd_ref)

    lead = lax.rem(pl.program_id(0), tiles_per_seq) == 0

    @pl.when(lead)
    def _():
        out_ref[0:t - LEAD_ROWS, :] = jnp.zeros((t - LEAD_ROWS, d), F32)
        out_ref[t - LEAD_ROWS:, :] = rows_of_tile(h_ref[t - LEAD_ROWS:, :], o_ref[t - LEAD_ROWS:, :])

    @pl.when(jnp.logical_not(lead))
    def _():
        out_ref[...] = rows_of_tile(h_ref[...], o_ref[...])


def _attn_out_ffn(h, o, wo, gf, wg, wu, wd):
    b, lh, d = h.shape
    rows = b * lh
    tile_f32 = pl.BlockSpec((T_SEQ, d), lambda i: (i, 0))
    out = pl.pallas_call(
        functools.partial(_attn_out_ffn_kernel, tiles_per_seq=lh // T_SEQ),
        grid=(rows // T_SEQ,),
        in_specs=[tile_f32, pl.BlockSpec((T_SEQ, d), lambda i: (i, 0)), _const_spec(wo.shape),
                  _const_spec(gf.shape), _layer_spec(*wg), _layer_spec(*wu), _layer_spec(*wd)],
        out_specs=tile_f32,
        out_shape=jax.ShapeDtypeStruct((rows, d), F32),
        compiler_params=pltpu.CompilerParams(
            dimension_semantics=("arbitrary",), vmem_limit_bytes=VMEM_LIMIT_BYTES),
        name="attn_out_ffn",
    )(h.reshape(rows, d), o.reshape(rows, d), wo, gf, wg[0], wu[0], wd[0])
    return out.reshape(b, lh, d)


def _conv_ffn_kernel(h_ref, gm_ref, win_ref, bin_ref, dww_ref, dwb_ref, lng_ref, lnb_ref,
                     wout_ref, bout_ref, gf_ref, wg_ref, wu_ref, wd_ref, o_ref, uext_ref, conv_ref):
    i = pl.program_id(1)
    t, d = h_ref.shape[1:]

    lead = max(LEAD_ROWS, CONV_ROWS)

    @pl.when(i == 0)
    def _():
        uext_ref[0:CONV_HALO, :] = jnp.zeros((CONV_HALO, d), F32)
        for rows in (lead, t):
            uext_ref[CONV_HALO + rows:CONV_HALO + rows + SUBLANES, :] = jnp.zeros((SUBLANES, d), F32)
        o_ref[0, 0:t - lead, :] = jnp.zeros((t - lead, d), F32)
        o_ref[0, t - lead:, :] = _conv_ffn_rows(
            h_ref[0, t - lead:, :], _positions(0, t)[t - lead:], gm_ref, win_ref, bin_ref,
            dww_ref, dwb_ref, lng_ref, lnb_ref, wout_ref, bout_ref, gf_ref, wg_ref, wu_ref, wd_ref,
            uext_ref, conv_ref)

    @pl.when(i > 0)
    def _():
        o_ref[0] = _conv_ffn_rows(
            h_ref[0], _positions(i, t), gm_ref, win_ref, bin_ref, dww_ref, dwb_ref, lng_ref, lnb_ref,
            wout_ref, bout_ref, gf_ref, wg_ref, wu_ref, wd_ref, uext_ref, conv_ref)


def _conv_ffn_rows(h, pos, gm_ref, win_ref, bin_ref, dww_ref, dwb_ref, lng_ref, lnb_ref,
                   wout_ref, bout_ref, gf_ref, wg_ref, wu_ref, wd_ref, uext_ref, conv_ref):
    rows, d = h.shape
    y = _rms(h, gm_ref[...], RMS_EPS).astype(BF16)
    ag = jnp.dot(y, win_ref[...], preferred_element_type=F32) + bin_ref[...]
    u = ag[:, :d] * jax.nn.sigmoid(ag[:, d:])
    u = jnp.where(pos >= 0, u, 0.0)
    uext_ref[CONV_HALO:CONV_HALO + rows, :] = u

    base = CONV_HALO - (CONV_WIDTH - 1)
    span = CONV_ROWS + SUBLANES
    n_oct = (base + CONV_WIDTH - 1) // SUBLANES + 1

    def conv_body(c, carry):
        r0 = pl.multiple_of(c * CONV_ROWS, CONV_ROWS)
        for lb in range(d // LANES):
            cs = slice(lb * LANES, (lb + 1) * LANES)
            um = [uext_ref[pl.ds(r0 + SUBLANES * m, span), cs] for m in range(n_oct)]
            out = None
            for sh in range(SUBLANES):
                part = None
                for m in range(n_oct):
                    j = SUBLANES * m + sh - base
                    if 0 <= j < CONV_WIDTH:
                        term = dww_ref[j:j + 1, cs] * um[m]
                        part = term if part is None else part + term
                if sh:
                    part = pltpu.roll(part, span - sh, axis=0)
                out = part if out is None else out + part
            conv_ref[pl.ds(r0, CONV_ROWS), cs] = out[:CONV_ROWS] + dwb_ref[:, cs]
        return carry

    lax.fori_loop(0, rows // CONV_ROWS, conv_body, 0)
    uext_ref[0:CONV_HALO, :] = uext_ref[rows:rows + CONV_HALO, :]

    c = conv_ref[0:rows, :]
    mu = jnp.mean(c, axis=-1, keepdims=True)
    cc = c - mu
    var = jnp.mean(cc * cc, axis=-1, keepdims=True)
    z = (cc * lax.rsqrt(var + LN_EPS)) * lng_ref[...] + lnb_ref[...]
    z = (z * jax.nn.sigmoid(z)).astype(BF16)
    hmid = h + jnp.dot(z, wout_ref[...], preferred_element_type=F32) + bout_ref[...]
    return _ffn(hmid, gf_ref, wg_ref, wu_ref, wd_ref)


def _conv_ffn(h, gm, win, bin_, dww, dwb, lng, lnb, wout, bout, gf, wg, wu, wd):
    b, lh, d = h.shape
    nt = lh // T_SEQ
    tile = pl.BlockSpec((1, T_SEQ, d), lambda bi, i: (bi, i, 0))
    consts = [gm, win, bin_, dww, dwb, lng, lnb, wout, bout, gf]
    ffn = [wg, wu, wd]
    return pl.pallas_call(
        _conv_ffn_kernel,
        grid=(b, nt),
        in_specs=[tile] + [_const_spec(c.shape) for c in consts] + [_layer_spec(*w) for w in ffn],
        out_specs=tile,
        out_shape=jax.ShapeDtypeStruct((b, lh, d), F32),
        scratch_shapes=[pltpu.VMEM((T_SEQ + CONV_HALO + SUBLANES, d), F32), pltpu.VMEM((T_SEQ, d), F32)],
        compiler_params=pltpu.CompilerParams(
            dimension_semantics=("arbitrary", "arbitrary"), vmem_limit_bytes=VMEM_LIMIT_BYTES),
        name="conv_ffn",
    )(h, *consts, *(w[0] for w in ffn))


def kernel(x, meta_tokens, norm_mix_g, norm_ffn_g, final_norm_g, rel_bias_table, pool_w, pool_b, pool_scale, attn_w_qkv, attn_w_o, attn_lambda_q1, attn_lambda_k1, attn_lambda_q2, attn_lambda_k2, attn_subln_g, conv_w_in, conv_b_in, conv_dw_w, conv_dw_b, conv_ln_g, conv_ln_b, conv_w_out, conv_b_out, ffn_w_gate, ffn_w_up, ffn_w_down):
    b, seq, d = x.shape
    depth = norm_mix_g.shape[0]
    assert seq % T_SEQ == 0 and d % (LANES * len(POOL_WINDOWS)) == 0
    assert depth % N_MIXERS == 1, "the last layer must be a pooling layer (it applies the final norm)"

    row = lambda v: v.reshape(1, -1).astype(F32)
    h = x.astype(F32)
    bias = None
    ffn_stacks = [w.astype(BF16) for w in (ffn_w_gate, ffn_w_up, ffn_w_down)]
    for i in range(depth):
        mixer, j = i % N_MIXERS, i // N_MIXERS
        gm, gf = row(norm_mix_g[i]), row(norm_ffn_g[i])
        wg, wu, wd = ((w, i) for w in ffn_stacks)
        if mixer == 0:
            h = _pool_ffn(h, gm, pool_w[j].astype(BF16), row(pool_b[j]), row(pool_scale[j]),
                          gf, wg, wu, wd,
                          meta=meta_tokens.astype(F32) if i == 0 else None,
                          final_g=row(final_norm_g) if i == depth - 1 else None)
        elif mixer == 1:
            lam_init = 0.8 - 0.6 * math.exp(-0.3 * i)
            hd = d // DA_HEADS
            if bias is None:
                bias = _bias_tiles(rel_bias_table.astype(F32), _bias_buckets())
            qscale = (hd // 2) ** -0.5 * LOG2E
            q, k, vt = _qkv(h, gm, attn_w_qkv[j].astype(BF16), qscale)
            lamv = jnp.stack([attn_lambda_q1[j], attn_lambda_k1[j],
                              attn_lambda_q2[j], attn_lambda_k2[j]]).astype(F32)
            o = _attention(q, k, vt, bias, lamv, attn_subln_g[j].reshape(hd, 1).astype(F32), lam_init)
            h = _attn_out_ffn(h, o, attn_w_o[j].astype(BF16), gf, wg, wu, wd)
        else:
            h = _conv_ffn(h, gm, conv_w_in[j].astype(BF16), row(conv_b_in[j]),
                          conv_dw_w[j].astype(F32), row(conv_dw_b[j]), row(conv_ln_g[j]),
                          row(conv_ln_b[j]), conv_w_out[j].astype(BF16), row(conv_b_out[j]),
                          gf, wg, wu, wd)
    return h
```

```python
import functools
import math

import jax
import jax.numpy as jnp
from jax import lax
from jax.experimental import pallas as pl
from jax.experimental.pallas import tpu as pltpu

F32 = jnp.float32
BF16 = jnp.bfloat16

N_META = 16
CHUNK = 64
POOL_WINDOWS = (2, 4, 8, 16)
DA_HEADS = 8
REL_BUCKETS = 32
REL_MAX_DIST = 128
CONV_WIDTH = 31
RMS_EPS = 1e-6
LN_EPS = 1e-5
SUBLN_EPS = 1e-5
N_MIXERS = 3

LANES = 128
SUBLANES = 8
MXU_DIM = 256
VMEM_LIMIT_BYTES = 56 * 1024 * 1024

T_SEQ = 512
META_BLK = 128
ONES_ROWS = 16
COL_CHUNK = 2 * MXU_DIM
LEAD_ROWS = 32
POOL_HALO = 16
CONV_HALO = 32
CONV_ROWS = 128
NEG = -1e30
LOG2E = math.log2(math.e)
FAR_BUCKET = REL_BUCKETS // 2 - 1


def _ffn_chunks(f):
    step = 3 * MXU_DIM
    return [(s, min(s + step, f)) for s in range(0, f, step)]


def _rms(x, g, eps):
    ms = jnp.mean(x * x, axis=-1, keepdims=True)
    return (x * lax.rsqrt(ms + eps)) * g


def _ffn(hmid, g_ref, wg_ref, wu_ref, wd_ref):
    x = _rms(hmid, g_ref[...], RMS_EPS).astype(BF16)
    acc = None
    for f0, f1 in _ffn_chunks(wg_ref.shape[1]):
        gate = jnp.dot(x, wg_ref[:, f0:f1], preferred_element_type=F32)
        up = jnp.dot(x, wu_ref[:, f0:f1], preferred_element_type=F32)
        act = ((gate * jax.nn.sigmoid(gate)) * up).astype(BF16)
        part = jnp.dot(act, wd_ref[f0:f1, :], preferred_element_type=F32)
        acc = part if acc is None else acc + part
    return hmid + acc


def _positions(tile_idx, rows):
    r = lax.broadcasted_iota(jnp.int32, (rows, 1), 0)
    return tile_idx * rows + r - (T_SEQ - N_META)


def _pool_ffn_kernel(h_ref, gm_ref, pw_ref, pb_ref, ps_ref, gf_ref, wg_ref, wu_ref, wd_ref, *rest,
                     first, final):
    rest = list(rest)
    meta_ref = rest.pop(0) if first else None
    fg_ref = rest.pop(0) if final else None
    o_ref, halo_ref = rest
    i = pl.program_id(1)
    t, d = h_ref.shape[1:]

    def rows_of_tile(h, pos, halo):
        rows = h.shape[0]
        y = _rms(h, gm_ref[...], RMS_EPS)
        if pos is not None:
            y = jnp.where(pos >= 0, y, 0.0)
        ext = jnp.concatenate([halo, y], axis=0)
        halo_ref[...] = y[rows - POOL_HALO:, :]
        group = d // len(POOL_WINDOWS)
        mixed = []
        for gi, win in enumerate(POOL_WINDOWS):
            c0 = gi * group
            s = ext[:, c0:c0 + group]
            shift = 1
            while shift < win:
                s = s + pltpu.roll(s, shift, axis=0)
                shift *= 2
            if pos is None:
                mean = s[POOL_HALO:, :] * (1.0 / win)
            else:
                mean = s[POOL_HALO:, :] / jnp.clip(pos + 1, 1, win).astype(F32)
            pooled = mean - y[:, c0:c0 + group]
            mixed.append(jnp.dot(pooled.astype(BF16), pw_ref[gi], preferred_element_type=F32))
        mixed = jnp.concatenate(mixed, axis=1)
        hmid = h + (mixed + pb_ref[...]) * ps_ref[...]
        out = _ffn(hmid, gf_ref, wg_ref, wu_ref, wd_ref)
        if final:
            out = _rms(out, fg_ref[...], RMS_EPS)
        return out

    @pl.when(i == 0)
    def _():
        if first:
            h = jnp.concatenate([jnp.zeros((LEAD_ROWS - N_META, d), F32), meta_ref[...]], axis=0)
        else:
            h = h_ref[0, t - LEAD_ROWS:, :]
        pos = _positions(0, t)[t - LEAD_ROWS:]
        o_ref[0, 0:t - LEAD_ROWS, :] = jnp.zeros((t - LEAD_ROWS, d), F32)
        o_ref[0, t - LEAD_ROWS:, :] = rows_of_tile(h, pos, jnp.zeros((POOL_HALO, d), F32))

    @pl.when(i > 0)
    def _():
        o_ref[0] = rows_of_tile(h_ref[0], None, halo_ref[...])


def _const_spec(shape):
    nd = len(shape)
    return pl.BlockSpec(shape, lambda *_: (0,) * nd, pipeline_mode=pl.Buffered(1))


def _layer_spec(stack, layer):
    nd = stack.ndim - 1
    return pl.BlockSpec((None,) + stack.shape[1:], lambda *_: (layer,) + (0,) * nd,
                        pipeline_mode=pl.Buffered(1))


def _pool_ffn(h, gm, pw, pb, ps, gf, wg, wu, wd, meta=None, final_g=None):
    first, final = meta is not None, final_g is not None
    b, _, d = h.shape
    lh = h.shape[1] + T_SEQ if first else h.shape[1]
    nt = lh // T_SEQ
    tile = pl.BlockSpec((1, T_SEQ, d), lambda bi, i: (bi, i, 0))
    frames = pl.BlockSpec((1, T_SEQ, d), lambda bi, i: (bi, jnp.maximum(i - 1, 0), 0))
    in_specs = [frames if first else tile, _const_spec(gm.shape), _const_spec(pw.shape),
                _const_spec(pb.shape), _const_spec(ps.shape), _const_spec(gf.shape),
                _layer_spec(*wg), _layer_spec(*wu), _layer_spec(*wd)]
    args = [h, gm, pw, pb, ps, gf, wg[0], wu[0], wd[0]]
    if first:
        in_specs.append(_const_spec(meta.shape))
        args.append(meta)
    if final:
        in_specs.append(_const_spec(final_g.shape))
        args.append(final_g)
        out_spec = frames
        out_shape = jax.ShapeDtypeStruct((b, lh - T_SEQ, d), F32)
    else:
        out_spec = tile
        out_shape = jax.ShapeDtypeStruct((b, lh, d), F32)
    return pl.pallas_call(
        functools.partial(_pool_ffn_kernel, first=first, final=final),
        grid=(b, nt),
        in_specs=in_specs,
        out_specs=out_spec,
        out_shape=out_shape,
        scratch_shapes=[pltpu.VMEM((POOL_HALO, d), F32)],
        compiler_params=pltpu.CompilerParams(
            dimension_semantics=("arbitrary", "arbitrary"), vmem_limit_bytes=VMEM_LIMIT_BYTES),
        name="pool_ffn_final" if final else "pool_ffn",
    )(*args)


def _qkv_kernel(h_ref, g_ref, w_ref, q_ref, k_ref, vt_ref, *, qscale):
    h = h_ref[0]
    d = h.shape[1]
    y = _rms(h, g_ref[...], RMS_EPS).astype(BF16)
    q = jnp.dot(y, w_ref[:, 0:d], preferred_element_type=F32) * qscale
    k = jnp.dot(y, w_ref[:, d:2 * d], preferred_element_type=F32)
    v = jnp.dot(y, w_ref[:, 2 * d:3 * d], preferred_element_type=F32)
    q_ref[0] = q.astype(BF16)
    k_ref[0] = k.astype(BF16)
    vt = v.T.astype(BF16)
    heads, hv, rows = vt_ref.shape[2:]
    hd = d // heads
    for hh in range(heads):
        vt_ref[0, 0, hh, 0:hd, :] = vt[hh * hd:(hh + 1) * hd, :]
        vt_ref[0, 0, hh, hd:hv, :] = jnp.ones((hv - hd, rows), BF16)


def _qkv(h, g, w, qscale):
    b, lh, d = h.shape
    nt = lh // T_SEQ
    hv = d // DA_HEADS + ONES_ROWS
    tile = pl.BlockSpec((1, T_SEQ, d), lambda bi, i: (bi, i, 0))
    return pl.pallas_call(
        functools.partial(_qkv_kernel, qscale=qscale),
        grid=(b, nt),
        in_specs=[tile, _const_spec(g.shape), _const_spec(w.shape)],
        out_specs=[tile, tile,
                   pl.BlockSpec((1, 1, DA_HEADS, hv, T_SEQ), lambda bi, i: (bi, i, 0, 0, 0))],
        out_shape=[jax.ShapeDtypeStruct((b, lh, d), BF16), jax.ShapeDtypeStruct((b, lh, d), BF16),
                   jax.ShapeDtypeStruct((b, nt, DA_HEADS, hv, T_SEQ), BF16)],
        compiler_params=pltpu.CompilerParams(
            dimension_semantics=("arbitrary", "arbitrary"), vmem_limit_bytes=VMEM_LIMIT_BYTES),
        name="qkv",
    )(h, g, w)


def _t5_bucket(rel):
    half = REL_BUCKETS // 2
    max_exact = half // 2
    offset = jnp.where(rel > 0, half, 0)
    n = jnp.abs(rel)
    log_ratio = jnp.log(jnp.maximum(n, 1).astype(F32) / max_exact) / math.log(REL_MAX_DIST / max_exact)
    large = jnp.minimum(max_exact + (log_ratio * (half - max_exact)).astype(jnp.int32), half - 1)
    return offset + jnp.where(n < max_exact, n, large)


def _bias_buckets():
    t = T_SEQ
    kl = lax.broadcasted_iota(jnp.int32, (t, t), 0)
    ql = lax.broadcasted_iota(jnp.int32, (t, t), 1)
    prev = _t5_bucket(kl - t - ql)[t - META_BLK:]
    diag = jnp.where(kl // CHUNK <= ql // CHUNK, _t5_bucket(kl - ql), -1)
    kr = lax.broadcasted_iota(jnp.int32, (META_BLK, t), 0)
    qc = lax.broadcasted_iota(jnp.int32, (META_BLK, t), 1)
    mk = kr - (META_BLK - N_META)
    mq = qc - (t - N_META)
    lead = jnp.where(mq >= 0, _t5_bucket(mk - mq), FAR_BUCKET)
    first = _t5_bucket(mk - N_META - qc)
    later = jnp.full((META_BLK, t), FAR_BUCKET, jnp.int32)
    meta = [jnp.where(mk >= 0, m, -1) for m in (lead, first, later)]
    nb = t // META_BLK
    return jnp.concatenate([prev[None], diag.reshape(nb, META_BLK, t), jnp.stack(meta)], axis=0)


def _bias_kernel(tab_ref, bk_ref, o_ref):
    hh = pl.program_id(0)
    far = tab_ref[FAR_BUCKET, hh]
    for blk in range(bk_ref.shape[0]):
        b = bk_ref[blk]
        acc = jnp.zeros(b.shape, F32)
        for t in range(REL_BUCKETS):
            acc = jnp.where(b == t, tab_ref[t, hh] - far, acc)
        o_ref[0, blk] = jnp.where(b < 0, NEG, acc * LOG2E)


def _bias_tiles(table, buckets):
    nb, r, c = buckets.shape
    heads = table.shape[1]
    return pl.pallas_call(
        _bias_kernel,
        grid=(heads,),
        in_specs=[pl.BlockSpec(memory_space=pltpu.SMEM),
                  pl.BlockSpec((nb, r, c), lambda hh: (0, 0, 0))],
        out_specs=pl.BlockSpec((1, nb, r, c), lambda hh: (hh, 0, 0, 0)),
        out_shape=jax.ShapeDtypeStruct((heads, nb, r, c), F32),
        compiler_params=pltpu.CompilerParams(dimension_semantics=("arbitrary",)),
        name="rel_bias",
    )(table, buckets)


def _attn_kernel(q_ref, k_ref, vt_ref, bias_ref, lam_ref, sg_ref, o_ref,
                 qzz_ref, m_ref, mt_ref, acc_ref, sa_ref, sb_ref, *, lam_init):
    lv = lam_ref[...]
    lam = (jnp.exp(jnp.sum(lv[0:1] * lv[1:2], axis=1, keepdims=True))
           - jnp.exp(jnp.sum(lv[2:3] * lv[3:4], axis=1, keepdims=True)) + lam_init)

    def query_tile(a, carry):
        _attn_query_tile(a, lam, q_ref, k_ref, vt_ref, bias_ref, sg_ref, o_ref,
                         qzz_ref, m_ref, mt_ref, acc_ref, sa_ref, sb_ref, lam_init=lam_init)
        return carry

    n_tiles = q_ref.shape[1] // T_SEQ
    lax.fori_loop(0, n_tiles, query_tile, 0)
    _attn_finalize(n_tiles - 1, lam, acc_ref, sg_ref, o_ref, lam_init)


def _attn_query_tile(a, lam, q_ref, k_ref, vt_ref, bias_ref, sg_ref, o_ref,
                     qzz_ref, m_ref, mt_ref, acc_ref, sa_ref, sb_ref, *, lam_init):
    t = T_SEQ
    hd = q_ref.shape[2]
    nb = t // META_BLK
    rows = pl.ds(pl.multiple_of(a * t, t), t)
    bnear_ref = bias_ref

    qt = q_ref[0, rows, :].astype(F32).T
    row = lax.broadcasted_iota(jnp.int32, (hd, t), 0)
    q1 = jnp.where(row < hd // 2, qt, 0.0)
    q2 = jnp.where(row >= hd // 2, qt, 0.0)
    qzz_ref[...] = jnp.concatenate([q1, q2], axis=1).astype(BF16)

    def produce(j, buf):
        kt = k_ref[0, pl.ds(pl.multiple_of(j * t, t), t), :]
        mts = []
        for cs in col_chunks:
            s = jnp.dot(kt, qzz_ref[:, cs], preferred_element_type=F32)
            buf[:, cs] = s
            mts.append(jnp.max(s, axis=0, keepdims=True))
        return jnp.concatenate(mts, axis=1)

    def probs(s, m):
        return jnp.exp2(s - m).astype(BF16)

    def consume(buf, mt, vt):
        for cs in col_chunks:
            m_old = m_ref[:, cs]
            m_new = jnp.maximum(m_old, mt[:, cs])
            alpha = jnp.exp2(m_old - m_new)
            acc_ref[:, cs] = alpha * acc_ref[:, cs] + jnp.dot(
                vt, probs(buf[:, cs], m_new), preferred_element_type=F32)
            m_ref[:, cs] = m_new

    half = t // 2
    maps = (0, t)

    def produce_diag(j, buf):
        k0 = pl.multiple_of(j * t, t)
        k_top = k_ref[0, pl.ds(k0, half), :]
        k_bot = k_ref[0, pl.ds(k0 + half, half), :]
        for base in maps:
            buf[0:half, base:base + t] = jnp.dot(
                k_top, qzz_ref[:, base:base + t], preferred_element_type=F32)
            buf[half:t, base + half:base + t] = jnp.dot(
                k_bot, qzz_ref[:, base + half:base + t], preferred_element_type=F32)

    def consume_diag(buf, j):
        vt = vt_ref[0, j, 0]
        for base in maps:
            cs, hi = slice(base, base + t), slice(base + half, base + t)
            s_top = buf[0:half, cs] + bnear_ref[0, 1:1 + nb // 2].reshape(half, t)
            s_bot = buf[half:t, hi] + bnear_ref[0, 1 + nb // 2:1 + nb, :, half:t].reshape(half, half)
            mt_top = jnp.max(s_top, axis=0, keepdims=True)
            mt = jnp.concatenate(
                [mt_top[:, 0:half], jnp.maximum(mt_top[:, half:t], jnp.max(s_bot, axis=0, keepdims=True))],
                axis=1)
            m_old = m_ref[:, cs]
            m_new = jnp.maximum(m_old, mt)
            alpha = jnp.exp2(m_old - m_new)
            p_top = probs(s_top, m_new)
            p_bot = probs(s_bot, m_new[:, half:t])
            pv = jnp.dot(vt[:, 0:half], p_top, preferred_element_type=F32)
            pv_hi = pv[:, half:t] + jnp.dot(vt[:, half:t], p_bot, preferred_element_type=F32)
            acc_ref[:, cs] = alpha * acc_ref[:, cs] + jnp.concatenate([pv[:, 0:half], pv_hi], axis=1)
            m_ref[:, cs] = m_new

    def fix_prev(buf, mt):
        corner = bnear_ref[0, 0, :, 0:LANES]
        pieces, start = [], 0
        for base in (0, t):
            cs = slice(base, base + LANES)
            buf[t - META_BLK:, cs] = buf[t - META_BLK:, cs] + corner
            if base > start:
                pieces.append(mt[:, start:base])
            pieces.append(jnp.max(buf[:, cs], axis=0, keepdims=True))
            start = base + LANES
        return jnp.concatenate(pieces + [mt[:, start:]], axis=1)

    def init():
        s0 = jnp.dot(k_ref[0, t - N_META:t, :], qzz_ref[...], preferred_element_type=F32)
        bmeta = bias_ref[0, nb + 1 + jnp.minimum(a, 2), META_BLK - N_META:, :]
        s0 = s0 + jnp.concatenate([bmeta, bmeta], axis=1)
        m0 = jnp.max(s0, axis=0, keepdims=True)
        m_ref[...] = m0
        vt_meta = vt_ref[0, 0, 0, :, t - META_BLK:t][:, META_BLK - N_META:]
        acc_ref[...] = jnp.dot(vt_meta, probs(s0, m0), preferred_element_type=F32)

    col_chunks = [slice(c, c + COL_CHUNK) for c in range(0, 2 * t, COL_CHUNK)]
    odd = lax.rem(a, 2)

    def finish_previous():
        _attn_finalize(a - 1, lam, acc_ref, sg_ref, o_ref, lam_init)

    @pl.when(a == 0)
    def _():
        init()

    @pl.when(a == 1)
    def _():
        finish_previous()
        init()
        produce_diag(1, sb_ref)
        consume_diag(sb_ref, 1)

    @pl.when((a >= 2) & (odd == 1))
    def _():
        finish_previous()
        init()
        mtb = produce(1, sb_ref)
        mt_ref[...] = produce(2, sa_ref)
        consume(sb_ref, mtb, vt_ref[0, 1, 0])

    @pl.when((a >= 2) & (odd == 0))
    def _():
        finish_previous()
        init()
        mt_ref[...] = produce(1, sa_ref)

    @pl.when(a >= 2)
    def _():
        def pair(j, mta):
            mtb = produce(j + 1, sb_ref)
            consume(sa_ref, mta, vt_ref[0, j, 0])
            mta_next = produce(j + 2, sa_ref)
            consume(sb_ref, mtb, vt_ref[0, j + 1, 0])
            return mta_next

        n_pairs = (a - 2) // 2
        odd_pairs = lax.rem(n_pairs, 2)

        @pl.when(odd_pairs == 1)
        def _():
            mt_ref[...] = pair(1 + odd, mt_ref[...])

        n_quads = n_pairs // 2
        odd_quads = lax.rem(n_quads, 2)

        @pl.when(odd_quads == 1)
        def _():
            j = 1 + odd + 2 * odd_pairs
            mt_ref[...] = pair(j + 2, pair(j, mt_ref[...]))

        j0 = 1 + odd + 2 * odd_pairs + 4 * odd_quads

        def oct_body(i, mta):
            j = j0 + 8 * i
            for step in range(0, 8, 2):
                mta = pair(j + step, mta)
            return mta

        mta = lax.fori_loop(0, n_quads // 2, oct_body, mt_ref[...])
        produce_diag(a, sb_ref)
        consume(sa_ref, fix_prev(sa_ref, mta), vt_ref[0, a - 1, 0])
        consume_diag(sb_ref, a)


def _attn_finalize(a, lam, acc_ref, sg_ref, o_ref, lam_init):
    t = T_SEQ
    hd = o_ref.shape[2]
    on = acc_ref[0:hd, :] * (1.0 / acc_ref[hd:hd + 1, :])
    o = on[:, :t] - lam * on[:, t:]
    o = (o * lax.rsqrt(jnp.mean(o * o, axis=0, keepdims=True) + SUBLN_EPS)) * sg_ref[...]
    o = o * (1.0 - lam_init)
    o_ref[0, pl.ds(pl.multiple_of(a * t, t), t), :] = o.T.astype(BF16)


def _attention(q, k, vt, bias, lamv, sg, lam_init):
    b, lh, d = q.shape
    nt = lh // T_SEQ
    hd = d // DA_HEADS
    hv = vt.shape[3]
    seq = pl.BlockSpec((1, lh, hd), lambda bi, hh: (bi, 0, hh))
    return pl.pallas_call(
        functools.partial(_attn_kernel, lam_init=lam_init),
        grid=(b, DA_HEADS),
        in_specs=[
            seq,
            seq,
            pl.BlockSpec((1, nt, 1, hv, T_SEQ), lambda bi, hh: (bi, 0, hh, 0, 0)),
            pl.BlockSpec((1,) + bias.shape[1:], lambda bi, hh: (hh, 0, 0, 0)),
            pl.BlockSpec(lamv.shape, lambda bi, hh: (0, 0)),
            pl.BlockSpec(sg.shape, lambda bi, hh: (0, 0)),
        ],
        out_specs=seq,
        out_shape=jax.ShapeDtypeStruct((b, lh, d), BF16),
        scratch_shapes=[pltpu.VMEM((hd, 2 * T_SEQ), BF16), pltpu.VMEM((1, 2 * T_SEQ), F32),
                        pltpu.VMEM((1, 2 * T_SEQ), F32), pltpu.VMEM((hv, 2 * T_SEQ), F32),
                        pltpu.VMEM((T_SEQ, 2 * T_SEQ), F32), pltpu.VMEM((T_SEQ, 2 * T_SEQ), F32)],
        compiler_params=pltpu.CompilerParams(
            dimension_semantics=("arbitrary", "arbitrary"), vmem_limit_bytes=VMEM_LIMIT_BYTES),
        name="diff_attn",
    )(q, k, vt, bias, lamv, sg)


def _attn_out_ffn_kernel(h_ref, o_ref, wo_ref, gf_ref, wg_ref, wu_ref, wd_ref, out_ref, *,
                         tiles_per_seq):
    t, d = h_ref.shape

    def rows_of_tile(h, o):
        hmid = h + jnp.dot(o, wo_ref[...], preferred_element_type=F32)
        return _ffn(hmid, gf_ref, wg_ref, wu_ref, wd_ref)

    lead = lax.rem(pl.program_id(0), tiles_per_seq) == 0

    @pl.when(lead)
    def _():
        out_ref[0:t - LEAD_ROWS, :] = jnp.zeros((t - LEAD_ROWS, d), F32)
        out_ref[t - LEAD_ROWS:, :] = rows_of_tile(h_ref[t - LEAD_ROWS:, :], o_ref[t - LEAD_ROWS:, :])

    @pl.when(jnp.logical_not(lead))
    def _():
        out_ref[...] = rows_of_tile(h_ref[...], o_ref[...])


def _attn_out_ffn(h, o, wo, gf, wg, wu, wd):
    b, lh, d = h.shape
    rows = b * lh
    tile_f32 = pl.BlockSpec((T_SEQ, d), lambda i: (i, 0))
    out = pl.pallas_call(
        functools.partial(_attn_out_ffn_kernel, tiles_per_seq=lh // T_SEQ),
        grid=(rows // T_SEQ,),
        in_specs=[tile_f32, pl.BlockSpec((T_SEQ, d), lambda i: (i, 0)), _const_spec(wo.shape),
                  _const_spec(gf.shape), _layer_spec(*wg), _layer_spec(*wu), _layer_spec(*wd)],
        out_specs=tile_f32,
        out_shape=jax.ShapeDtypeStruct((rows, d), F32),
        compiler_params=pltpu.CompilerParams(
            dimension_semantics=("arbitrary",), vmem_limit_bytes=VMEM_LIMIT_BYTES),
        name="attn_out_ffn",
    )(h.reshape(rows, d), o.reshape(rows, d), wo, gf, wg[0], wu[0], wd[0])
    return out.reshape(b, lh, d)


def _conv_ffn_kernel(h_ref, gm_ref, win_ref, bin_ref, dww_ref, dwb_ref, lng_ref, lnb_ref,
                     wout_ref, bout_ref, gf_ref, wg_ref, wu_ref, wd_ref, o_ref, uext_ref, conv_ref):
    i = pl.program_id(1)
    t, d = h_ref.shape[1:]

    lead = max(LEAD_ROWS, CONV_ROWS)

    @pl.when(i == 0)
    def _():
        uext_ref[0:CONV_HALO, :] = jnp.zeros((CONV_HALO, d), F32)
        for rows in (lead, t):
            uext_ref[CONV_HALO + rows:CONV_HALO + rows + SUBLANES, :] = jnp.zeros((SUBLANES, d), F32)
        o_ref[0, 0:t - lead, :] = jnp.zeros((t - lead, d), F32)
        o_ref[0, t - lead:, :] = _conv_ffn_rows(
            h_ref[0, t - lead:, :], _positions(0, t)[t - lead:], gm_ref, win_ref, bin_ref,
            dww_ref, dwb_ref, lng_ref, lnb_ref, wout_ref, bout_ref, gf_ref, wg_ref, wu_ref, wd_ref,
            uext_ref, conv_ref)

    @pl.when(i > 0)
    def _():
        o_ref[0] = _conv_ffn_rows(
            h_ref[0], None, gm_ref, win_ref, bin_ref, dww_ref, dwb_ref, lng_ref, lnb_ref,
            wout_ref, bout_ref, gf_ref, wg_ref, wu_ref, wd_ref, uext_ref, conv_ref)


def _conv_ffn_rows(h, pos, gm_ref, win_ref, bin_ref, dww_ref, dwb_ref, lng_ref, lnb_ref,
                   wout_ref, bout_ref, gf_ref, wg_ref, wu_ref, wd_ref, uext_ref, conv_ref):
    rows, d = h.shape
    y = _rms(h, gm_ref[...], RMS_EPS).astype(BF16)
    ag = jnp.dot(y, win_ref[...], preferred_element_type=F32) + bin_ref[...]
    u = ag[:, :d] * jax.nn.sigmoid(ag[:, d:])
    if pos is not None:
        u = jnp.where(pos >= 0, u, 0.0)
    uext_ref[CONV_HALO:CONV_HALO + rows, :] = u

    base = CONV_HALO - (CONV_WIDTH - 1)
    span = CONV_ROWS + SUBLANES
    n_oct = (base + CONV_WIDTH - 1) // SUBLANES + 1

    def conv_body(c, carry):
        r0 = pl.multiple_of(c * CONV_ROWS, CONV_ROWS)
        for lb in range(d // LANES):
            cs = slice(lb * LANES, (lb + 1) * LANES)
            um = [uext_ref[pl.ds(r0 + SUBLANES * m, span), cs] for m in range(n_oct)]
            out = None
            for sh in range(SUBLANES):
                part = None
                for m in range(n_oct):
                    j = SUBLANES * m + sh - base
                    if 0 <= j < CONV_WIDTH:
                        term = dww_ref[j:j + 1, cs] * um[m]
                        part = term if part is None else part + term
                if sh:
                    part = pltpu.roll(part, span - sh, axis=0)
                out = part if out is None else out + part
            conv_ref[pl.ds(r0, CONV_ROWS), cs] = out[:CONV_ROWS] + dwb_ref[:, cs]
        return carry

    lax.fori_loop(0, rows // CONV_ROWS, conv_body, 0)
    uext_ref[0:CONV_HALO, :] = uext_ref[rows:rows + CONV_HALO, :]

    c = conv_ref[0:rows, :]
    mu = jnp.mean(c, axis=-1, keepdims=True)
    cc = c - mu
    var = jnp.mean(cc * cc, axis=-1, keepdims=True)
    z = (cc * lax.rsqrt(var + LN_EPS)) * lng_ref[...] + lnb_ref[...]
    z = (z * jax.nn.sigmoid(z)).astype(BF16)
    hmid = h + jnp.dot(z, wout_ref[...], preferred_element_type=F32) + bout_ref[...]
    return _ffn(hmid, gf_ref, wg_ref, wu_ref, wd_ref)


def _conv_ffn(h, gm, win, bin_, dww, dwb, lng, lnb, wout, bout, gf, wg, wu, wd):
    b, lh, d = h.shape
    nt = lh // T_SEQ
    tile = pl.BlockSpec((1, T_SEQ, d), lambda bi, i: (bi, i, 0))
    consts = [gm, win, bin_, dww, dwb, lng, lnb, wout, bout, gf]
    ffn = [wg, wu, wd]
    return pl.pallas_call(
        _conv_ffn_kernel,
        grid=(b, nt),
        in_specs=[tile] + [_const_spec(c.shape) for c in consts] + [_layer_spec(*w) for w in ffn],
        out_specs=tile,
        out_shape=jax.ShapeDtypeStruct((b, lh, d), F32),
        scratch_shapes=[pltpu.VMEM((T_SEQ + CONV_HALO + SUBLANES, d), F32), pltpu.VMEM((T_SEQ, d), F32)],
        compiler_params=pltpu.CompilerParams(
            dimension_semantics=("arbitrary", "arbitrary"), vmem_limit_bytes=VMEM_LIMIT_BYTES),
        name="conv_ffn",
    )(h, *consts, *(w[0] for w in ffn))


def kernel(x, meta_tokens, norm_mix_g, norm_ffn_g, final_norm_g, rel_bias_table, pool_w, pool_b, pool_scale, attn_w_qkv, attn_w_o, attn_lambda_q1, attn_lambda_k1, attn_lambda_q2, attn_lambda_k2, attn_subln_g, conv_w_in, conv_b_in, conv_dw_w, conv_dw_b, conv_ln_g, conv_ln_b, conv_w_out, conv_b_out, ffn_w_gate, ffn_w_up, ffn_w_down):
    b, seq, d = x.shape
    depth = norm_mix_g.shape[0]
    assert seq % T_SEQ == 0 and d % (LANES * len(POOL_WINDOWS)) == 0
    assert depth % N_MIXERS == 1, "the last layer must be a pooling layer (it applies the final norm)"

    row = lambda v: v.reshape(1, -1).astype(F32)
    h = x.astype(F32)
    bias = None
    ffn_stacks = [w.astype(BF16) for w in (ffn_w_gate, ffn_w_up, ffn_w_down)]
    for i in range(depth):
        mixer, j = i % N_MIXERS, i // N_MIXERS
        gm, gf = row(norm_mix_g[i]), row(norm_ffn_g[i])
        wg, wu, wd = ((w, i) for w in ffn_stacks)
        if mixer == 0:
            h = _pool_ffn(h, gm, pool_w[j].astype(BF16), row(pool_b[j]), row(pool_scale[j]),
                          gf, wg, wu, wd,
                          meta=meta_tokens.astype(F32) if i == 0 else None,
                          final_g=row(final_norm_g) if i == depth - 1 else None)
        elif mixer == 1:
            lam_init = 0.8 - 0.6 * math.exp(-0.3 * i)
            hd = d // DA_HEADS
            if bias is None:
                bias = _bias_tiles(rel_bias_table.astype(F32), _bias_buckets())
            qscale = (hd // 2) ** -0.5 * LOG2E
            q, k, vt = _qkv(h, gm, attn_w_qkv[j].astype(BF16), qscale)
            lamv = jnp.stack([attn_lambda_q1[j], attn_lambda_k1[j],
                              attn_lambda_q2[j], attn_lambda_k2[j]]).astype(F32)
            o = _attention(q, k, vt, bias, lamv, attn_subln_g[j].reshape(hd, 1).astype(F32), lam_init)
            h = _attn_out_ffn(h, o, attn_w_o[j].astype(BF16), gf, wg, wu, wd)
        else:
            h = _conv_ffn(h, gm, conv_w_in[j].astype(BF16), row(conv_b_in[j]),
                          conv_dw_w[j].astype(F32), row(conv_dw_b[j]), row(conv_ln_g[j]),
                          row(conv_ln_b[j]), conv_w_out[j].astype(BF16), row(conv_b_out[j]),
                          gf, wg, wu, wd)
    return h
```

```python
import functools
import math

import jax
import jax.numpy as jnp
from jax import lax
from jax.experimental import pallas as pl
from jax.experimental.pallas import tpu as pltpu

F32 = jnp.float32
BF16 = jnp.bfloat16

N_META = 16
CHUNK = 64
POOL_WINDOWS = (2, 4, 8, 16)
DA_HEADS = 8
REL_BUCKETS = 32
REL_MAX_DIST = 128
CONV_WIDTH = 31
RMS_EPS = 1e-6
LN_EPS = 1e-5
SUBLN_EPS = 1e-5
N_MIXERS = 3

LANES = 128
SUBLANES = 8
MXU_DIM = 256
VMEM_LIMIT_BYTES = 56 * 1024 * 1024

T_SEQ = 512
META_BLK = 128
ONES_ROWS = 16
COL_CHUNK = 2 * MXU_DIM
LEAD_ROWS = 32
POOL_HALO = 16
CONV_HALO = 32
CONV_ROWS = 128
NEG = -1e30
LOG2E = math.log2(math.e)
FAR_BUCKET = REL_BUCKETS // 2 - 1


def _ffn_chunks(f):
    step = 3 * MXU_DIM
    return [(s, min(s + step, f)) for s in range(0, f, step)]


def _rms(x, g, eps):
    ms = jnp.mean(x * x, axis=-1, keepdims=True)
    return (x * lax.rsqrt(ms + eps)) * g


def _ffn(hmid, g_ref, wg_ref, wu_ref, wd_ref):
    x = _rms(hmid, g_ref[...], RMS_EPS).astype(BF16)
    acc = None
    for f0, f1 in _ffn_chunks(wg_ref.shape[1]):
        gate = jnp.dot(x, wg_ref[:, f0:f1], preferred_element_type=F32)
        up = jnp.dot(x, wu_ref[:, f0:f1], preferred_element_type=F32)
        act = ((gate * jax.nn.sigmoid(gate)) * up).astype(BF16)
        part = jnp.dot(act, wd_ref[f0:f1, :], preferred_element_type=F32)
        acc = part if acc is None else acc + part
    return hmid + acc


def _positions(tile_idx, rows):
    r = lax.broadcasted_iota(jnp.int32, (rows, 1), 0)
    return tile_idx * rows + r - (T_SEQ - N_META)


def _pool_ffn_kernel(h_ref, gm_ref, pw_ref, pb_ref, ps_ref, gf_ref, wg_ref, wu_ref, wd_ref, *rest,
                     first, final):
    rest = list(rest)
    meta_ref = rest.pop(0) if first else None
    fg_ref = rest.pop(0) if final else None
    o_ref, halo_ref = rest
    i = pl.program_id(1)
    t, d = h_ref.shape[1:]

    def rows_of_tile(h, pos, halo):
        rows = h.shape[0]
        y = _rms(h, gm_ref[...], RMS_EPS)
        if pos is not None:
            y = jnp.where(pos >= 0, y, 0.0)
        ext = jnp.concatenate([halo, y], axis=0)
        halo_ref[...] = y[rows - POOL_HALO:, :]
        group = d // len(POOL_WINDOWS)
        mixed = []
        for gi, win in enumerate(POOL_WINDOWS):
            c0 = gi * group
            s = ext[:, c0:c0 + group]
            shift = 1
            while shift < win:
                s = s + pltpu.roll(s, shift, axis=0)
                shift *= 2
            if pos is None:
                mean = s[POOL_HALO:, :] * (1.0 / win)
            else:
                mean = s[POOL_HALO:, :] / jnp.clip(pos + 1, 1, win).astype(F32)
            pooled = mean - y[:, c0:c0 + group]
            mixed.append(jnp.dot(pooled.astype(BF16), pw_ref[gi], preferred_element_type=F32))
        mixed = jnp.concatenate(mixed, axis=1)
        hmid = h + (mixed + pb_ref[...]) * ps_ref[...]
        out = _ffn(hmid, gf_ref, wg_ref, wu_ref, wd_ref)
        if final:
            out = _rms(out, fg_ref[...], RMS_EPS)
        return out

    @pl.when(i == 0)
    def _():
        if first:
            h = jnp.concatenate([jnp.zeros((LEAD_ROWS - N_META, d), F32), meta_ref[...]], axis=0)
        else:
            h = h_ref[0, t - LEAD_ROWS:, :]
        pos = _positions(0, t)[t - LEAD_ROWS:]
        o_ref[0, 0:t - LEAD_ROWS, :] = jnp.zeros((t - LEAD_ROWS, d), F32)
        o_ref[0, t - LEAD_ROWS:, :] = rows_of_tile(h, pos, jnp.zeros((POOL_HALO, d), F32))

    @pl.when(i > 0)
    def _():
        o_ref[0] = rows_of_tile(h_ref[0], None, halo_ref[...])


def _const_spec(shape):
    nd = len(shape)
    return pl.BlockSpec(shape, lambda *_: (0,) * nd, pipeline_mode=pl.Buffered(1))


def _layer_spec(stack, layer):
    nd = stack.ndim - 1
    return pl.BlockSpec((None,) + stack.shape[1:], lambda *_: (layer,) + (0,) * nd,
                        pipeline_mode=pl.Buffered(1))


def _pool_ffn(h, gm, pw, pb, ps, gf, wg, wu, wd, meta=None, final_g=None):
    first, final = meta is not None, final_g is not None
    b, _, d = h.shape
    lh = h.shape[1] + T_SEQ if first else h.shape[1]
    nt = lh // T_SEQ
    tile = pl.BlockSpec((1, T_SEQ, d), lambda bi, i: (bi, i, 0))
    frames = pl.BlockSpec((1, T_SEQ, d), lambda bi, i: (bi, jnp.maximum(i - 1, 0), 0))
    in_specs = [frames if first else tile, _const_spec(gm.shape), _const_spec(pw.shape),
                _const_spec(pb.shape), _const_spec(ps.shape), _const_spec(gf.shape),
                _layer_spec(*wg), _layer_spec(*wu), _layer_spec(*wd)]
    args = [h, gm, pw, pb, ps, gf, wg[0], wu[0], wd[0]]
    if first:
        in_specs.append(_const_spec(meta.shape))
        args.append(meta)
    if final:
        in_specs.append(_const_spec(final_g.shape))
        args.append(final_g)
        out_spec = frames
        out_shape = jax.ShapeDtypeStruct((b, lh - T_SEQ, d), F32)
    else:
        out_spec = tile
        out_shape = jax.ShapeDtypeStruct((b, lh, d), F32)
    return pl.pallas_call(
        functools.partial(_pool_ffn_kernel, first=first, final=final),
        grid=(b, nt),
        in_specs=in_specs,
        out_specs=out_spec,
        out_shape=out_shape,
        scratch_shapes=[pltpu.VMEM((POOL_HALO, d), F32)],
        compiler_params=pltpu.CompilerParams(
            dimension_semantics=("arbitrary", "arbitrary"), vmem_limit_bytes=VMEM_LIMIT_BYTES),
        name="pool_ffn_final" if final else "pool_ffn",
    )(*args)


def _qkv_kernel(h_ref, g_ref, w_ref, q_ref, k_ref, vt_ref, *, qscale):
    h = h_ref[0]
    d = h.shape[1]
    y = _rms(h, g_ref[...], RMS_EPS).astype(BF16)
    q = jnp.dot(y, w_ref[:, 0:d], preferred_element_type=F32) * qscale
    k = jnp.dot(y, w_ref[:, d:2 * d], preferred_element_type=F32)
    v = jnp.dot(y, w_ref[:, 2 * d:3 * d], preferred_element_type=F32)
    q_ref[0, 0] = q.T.astype(BF16)
    k_ref[0] = k.astype(BF16)
    vt = v.T.astype(BF16)
    heads, hv, rows = vt_ref.shape[2:]
    hd = d // heads
    for hh in range(heads):
        vt_ref[0, 0, hh, 0:hd, :] = vt[hh * hd:(hh + 1) * hd, :]
        vt_ref[0, 0, hh, hd:hv, :] = jnp.ones((hv - hd, rows), BF16)


def _qkv(h, g, w, qscale):
    b, lh, d = h.shape
    nt = lh // T_SEQ
    hv = d // DA_HEADS + ONES_ROWS
    tile = pl.BlockSpec((1, T_SEQ, d), lambda bi, i: (bi, i, 0))
    return pl.pallas_call(
        functools.partial(_qkv_kernel, qscale=qscale),
        grid=(b, nt),
        in_specs=[tile, _const_spec(g.shape), _const_spec(w.shape)],
        out_specs=[pl.BlockSpec((1, 1, d, T_SEQ), lambda bi, i: (bi, i, 0, 0)), tile,
                   pl.BlockSpec((1, 1, DA_HEADS, hv, T_SEQ), lambda bi, i: (bi, i, 0, 0, 0))],
        out_shape=[jax.ShapeDtypeStruct((b, nt, d, T_SEQ), BF16), jax.ShapeDtypeStruct((b, lh, d), BF16),
                   jax.ShapeDtypeStruct((b, nt, DA_HEADS, hv, T_SEQ), BF16)],
        compiler_params=pltpu.CompilerParams(
            dimension_semantics=("arbitrary", "arbitrary"), vmem_limit_bytes=VMEM_LIMIT_BYTES),
        name="qkv",
    )(h, g, w)


def _t5_bucket(rel):
    half = REL_BUCKETS // 2
    max_exact = half // 2
    offset = jnp.where(rel > 0, half, 0)
    n = jnp.abs(rel)
    log_ratio = jnp.log(jnp.maximum(n, 1).astype(F32) / max_exact) / math.log(REL_MAX_DIST / max_exact)
    large = jnp.minimum(max_exact + (log_ratio * (half - max_exact)).astype(jnp.int32), half - 1)
    return offset + jnp.where(n < max_exact, n, large)


def _bias_buckets():
    t = T_SEQ
    kl = lax.broadcasted_iota(jnp.int32, (t, t), 0)
    ql = lax.broadcasted_iota(jnp.int32, (t, t), 1)
    prev = _t5_bucket(kl - t - ql)[t - META_BLK:]
    diag = jnp.where(kl // CHUNK <= ql // CHUNK, _t5_bucket(kl - ql), -1)
    kr = lax.broadcasted_iota(jnp.int32, (META_BLK, t), 0)
    qc = lax.broadcasted_iota(jnp.int32, (META_BLK, t), 1)
    mk = kr - (META_BLK - N_META)
    mq = qc - (t - N_META)
    lead = jnp.where(mq >= 0, _t5_bucket(mk - mq), FAR_BUCKET)
    first = _t5_bucket(mk - N_META - qc)
    later = jnp.full((META_BLK, t), FAR_BUCKET, jnp.int32)
    meta = [jnp.where(mk >= 0, m, -1) for m in (lead, first, later)]
    nb = t // META_BLK
    return jnp.concatenate([prev[None], diag.reshape(nb, META_BLK, t), jnp.stack(meta)], axis=0)


def _bias_kernel(tab_ref, bk_ref, o_ref):
    hh = pl.program_id(0)
    far = tab_ref[FAR_BUCKET, hh]
    for blk in range(bk_ref.shape[0]):
        b = bk_ref[blk]
        acc = jnp.zeros(b.shape, F32)
        for t in range(REL_BUCKETS):
            acc = jnp.where(b == t, tab_ref[t, hh] - far, acc)
        o_ref[0, blk] = jnp.where(b < 0, NEG, acc * LOG2E)


def _bias_tiles(table, buckets):
    nb, r, c = buckets.shape
    heads = table.shape[1]
    return pl.pallas_call(
        _bias_kernel,
        grid=(heads,),
        in_specs=[pl.BlockSpec(memory_space=pltpu.SMEM),
                  pl.BlockSpec((nb, r, c), lambda hh: (0, 0, 0))],
        out_specs=pl.BlockSpec((1, nb, r, c), lambda hh: (hh, 0, 0, 0)),
        out_shape=jax.ShapeDtypeStruct((heads, nb, r, c), F32),
        compiler_params=pltpu.CompilerParams(dimension_semantics=("arbitrary",)),
        name="rel_bias",
    )(table, buckets)


def _attn_kernel(q_ref, k_ref, vt_ref, bias_ref, lam_ref, sg_ref, o_ref,
                 qzz_ref, m_ref, mt_ref, acc_ref, sa_ref, sb_ref, *, lam_init):
    lv = lam_ref[...]
    lam = (jnp.exp(jnp.sum(lv[0:1] * lv[1:2], axis=1, keepdims=True))
           - jnp.exp(jnp.sum(lv[2:3] * lv[3:4], axis=1, keepdims=True)) + lam_init)

    def query_tile(a, carry):
        _attn_query_tile(a, lam, q_ref, k_ref, vt_ref, bias_ref, sg_ref, o_ref,
                         qzz_ref, m_ref, mt_ref, acc_ref, sa_ref, sb_ref, lam_init=lam_init)
        return carry

    n_tiles = q_ref.shape[1]
    lax.fori_loop(0, n_tiles, query_tile, 0)
    _attn_finalize(n_tiles - 1, lam, acc_ref, sg_ref, o_ref, lam_init)


def _attn_query_tile(a, lam, q_ref, k_ref, vt_ref, bias_ref, sg_ref, o_ref,
                     qzz_ref, m_ref, mt_ref, acc_ref, sa_ref, sb_ref, *, lam_init):
    t = T_SEQ
    hd = q_ref.shape[2]
    nb = t // META_BLK
    bnear_ref = bias_ref

    qt = q_ref[0, a]
    zeros = jnp.zeros((hd // 2, t), BF16)
    qzz_ref[0:hd // 2, 0:t] = qt[0:hd // 2]
    qzz_ref[hd // 2:, 0:t] = zeros
    qzz_ref[0:hd // 2, t:] = zeros
    qzz_ref[hd // 2:, t:] = qt[hd // 2:]

    def produce(j, buf):
        kt = k_ref[0, pl.ds(pl.multiple_of(j * t, t), t), :]
        mts = []
        for cs in col_chunks:
            s = jnp.dot(kt, qzz_ref[:, cs], preferred_element_type=F32)
            buf[:, cs] = s
            mts.append(jnp.max(s, axis=0, keepdims=True))
        return jnp.concatenate(mts, axis=1)

    def probs(s, m):
        return jnp.exp2(s - m).astype(BF16)

    def consume(buf, mt, vt):
        for cs in col_chunks:
            m_old = m_ref[:, cs]
            m_new = jnp.maximum(m_old, mt[:, cs])
            alpha = jnp.exp2(m_old - m_new)
            acc_ref[:, cs] = alpha * acc_ref[:, cs] + jnp.dot(
                vt, probs(buf[:, cs], m_new), preferred_element_type=F32)
            m_ref[:, cs] = m_new

    half = t // 2
    maps = (0, t)

    def produce_diag(j, buf):
        k0 = pl.multiple_of(j * t, t)
        k_top = k_ref[0, pl.ds(k0, half), :]
        k_bot = k_ref[0, pl.ds(k0 + half, half), :]
        for base in maps:
            buf[0:half, base:base + t] = jnp.dot(
                k_top, qzz_ref[:, base:base + t], preferred_element_type=F32)
            buf[half:t, base + half:base + t] = jnp.dot(
                k_bot, qzz_ref[:, base + half:base + t], preferred_element_type=F32)

    def consume_diag(buf, j):
        vt = vt_ref[0, j, 0]
        for base in maps:
            cs, hi = slice(base, base + t), slice(base + half, base + t)
            s_top = buf[0:half, cs] + bnear_ref[0, 1:1 + nb // 2].reshape(half, t)
            s_bot = buf[half:t, hi] + bnear_ref[0, 1 + nb // 2:1 + nb, :, half:t].reshape(half, half)
            mt_top = jnp.max(s_top, axis=0, keepdims=True)
            mt = jnp.concatenate(
                [mt_top[:, 0:half], jnp.maximum(mt_top[:, half:t], jnp.max(s_bot, axis=0, keepdims=True))],
                axis=1)
            m_old = m_ref[:, cs]
            m_new = jnp.maximum(m_old, mt)
            alpha = jnp.exp2(m_old - m_new)
            p_top = probs(s_top, m_new)
            p_bot = probs(s_bot, m_new[:, half:t])
            pv = jnp.dot(vt[:, 0:half], p_top, preferred_element_type=F32)
            pv_hi = pv[:, half:t] + jnp.dot(vt[:, half:t], p_bot, preferred_element_type=F32)
            acc_ref[:, cs] = alpha * acc_ref[:, cs] + jnp.concatenate([pv[:, 0:half], pv_hi], axis=1)
            m_ref[:, cs] = m_new

    def fix_prev(buf, mt):
        corner = bnear_ref[0, 0, :, 0:LANES]
        pieces, start = [], 0
        for base in (0, t):
            cs = slice(base, base + LANES)
            buf[t - META_BLK:, cs] = buf[t - META_BLK:, cs] + corner
            if base > start:
                pieces.append(mt[:, start:base])
            pieces.append(jnp.max(buf[:, cs], axis=0, keepdims=True))
            start = base + LANES
        return jnp.concatenate(pieces + [mt[:, start:]], axis=1)

    def init():
        s0 = jnp.dot(k_ref[0, t - N_META:t, :], qzz_ref[...], preferred_element_type=F32)
        bmeta = bias_ref[0, nb + 1 + jnp.minimum(a, 2), META_BLK - N_META:, :]
        s0 = s0 + jnp.concatenate([bmeta, bmeta], axis=1)
        m0 = jnp.max(s0, axis=0, keepdims=True)
        m_ref[...] = m0
        vt_meta = vt_ref[0, 0, 0, :, t - META_BLK:t][:, META_BLK - N_META:]
        acc_ref[...] = jnp.dot(vt_meta, probs(s0, m0), preferred_element_type=F32)

    col_chunks = [slice(c, c + COL_CHUNK) for c in range(0, 2 * t, COL_CHUNK)]
    odd = lax.rem(a, 2)

    def finish_previous():
        _attn_finalize(a - 1, lam, acc_ref, sg_ref, o_ref, lam_init)

    @pl.when(a == 0)
    def _():
        init()

    @pl.when(a == 1)
    def _():
        finish_previous()
        init()
        produce_diag(1, sb_ref)
        consume_diag(sb_ref, 1)

    @pl.when((a >= 2) & (odd == 1))
    def _():
        finish_previous()
        init()
        mtb = produce(1, sb_ref)
        mt_ref[...] = produce(2, sa_ref)
        consume(sb_ref, mtb, vt_ref[0, 1, 0])

    @pl.when((a >= 2) & (odd == 0))
    def _():
        finish_previous()
        init()
        mt_ref[...] = produce(1, sa_ref)

    @pl.when(a >= 2)
    def _():
        def pair(j, mta):
            mtb = produce(j + 1, sb_ref)
            consume(sa_ref, mta, vt_ref[0, j, 0])
            mta_next = produce(j + 2, sa_ref)
            consume(sb_ref, mtb, vt_ref[0, j + 1, 0])
            return mta_next

        n_pairs = (a - 2) // 2
        odd_pairs = lax.rem(n_pairs, 2)

        @pl.when(odd_pairs == 1)
        def _():
            mt_ref[...] = pair(1 + odd, mt_ref[...])

        n_quads = n_pairs // 2
        odd_quads = lax.rem(n_quads, 2)

        @pl.when(odd_quads == 1)
        def _():
            j = 1 + odd + 2 * odd_pairs
            mt_ref[...] = pair(j + 2, pair(j, mt_ref[...]))

        j0 = 1 + odd + 2 * odd_pairs + 4 * odd_quads

        def oct_body(i, mta):
            j = j0 + 8 * i
            for step in range(0, 8, 2):
                mta = pair(j + step, mta)
            return mta

        mta = lax.fori_loop(0, n_quads // 2, oct_body, mt_ref[...])
        produce_diag(a, sb_ref)
        consume(sa_ref, fix_prev(sa_ref, mta), vt_ref[0, a - 1, 0])
        consume_diag(sb_ref, a)


def _attn_finalize(a, lam, acc_ref, sg_ref, o_ref, lam_init):
    t = T_SEQ
    hd = o_ref.shape[2]
    on = acc_ref[0:hd, :] * (1.0 / acc_ref[hd:hd + 1, :])
    o = on[:, :t] - lam * on[:, t:]
    o = (o * lax.rsqrt(jnp.mean(o * o, axis=0, keepdims=True) + SUBLN_EPS)) * sg_ref[...]
    o = o * (1.0 - lam_init)
    o_ref[0, pl.ds(pl.multiple_of(a * t, t), t), :] = o.T.astype(BF16)


def _attention(qt, k, vt, bias, lamv, sg, lam_init):
    b, lh, d = k.shape
    nt = lh // T_SEQ
    hd = d // DA_HEADS
    hv = vt.shape[3]
    seq = pl.BlockSpec((1, lh, hd), lambda bi, hh: (bi, 0, hh))
    return pl.pallas_call(
        functools.partial(_attn_kernel, lam_init=lam_init),
        grid=(b, DA_HEADS),
        in_specs=[
            pl.BlockSpec((1, nt, hd, T_SEQ), lambda bi, hh: (bi, 0, hh, 0)),
            seq,
            pl.BlockSpec((1, nt, 1, hv, T_SEQ), lambda bi, hh: (bi, 0, hh, 0, 0)),
            pl.BlockSpec((1,) + bias.shape[1:], lambda bi, hh: (hh, 0, 0, 0)),
            pl.BlockSpec(lamv.shape, lambda bi, hh: (0, 0)),
            pl.BlockSpec(sg.shape, lambda bi, hh: (0, 0)),
        ],
        out_specs=seq,
        out_shape=jax.ShapeDtypeStruct((b, lh, d), BF16),
        scratch_shapes=[pltpu.VMEM((hd, 2 * T_SEQ), BF16), pltpu.VMEM((1, 2 * T_SEQ), F32),
                        pltpu.VMEM((1, 2 * T_SEQ), F32), pltpu.VMEM((hv, 2 * T_SEQ), F32),
                        pltpu.VMEM((T_SEQ, 2 * T_SEQ), F32), pltpu.VMEM((T_SEQ, 2 * T_SEQ), F32)],
        compiler_params=pltpu.CompilerParams(
            dimension_semantics=("arbitrary", "arbitrary"), vmem_limit_bytes=VMEM_LIMIT_BYTES),
        name="diff_attn",
    )(qt, k, vt, bias, lamv, sg)


def _attn_out_ffn_kernel(h_ref, o_ref, wo_ref, gf_ref, wg_ref, wu_ref, wd_ref, out_ref, *,
                         tiles_per_seq):
    t, d = h_ref.shape

    def rows_of_tile(h, o):
        hmid = h + jnp.dot(o, wo_ref[...], preferred_element_type=F32)
        return _ffn(hmid, gf_ref, wg_ref, wu_ref, wd_ref)

    lead = lax.rem(pl.program_id(0), tiles_per_seq) == 0

    @pl.when(lead)
    def _():
        out_ref[0:t - LEAD_ROWS, :] = jnp.zeros((t - LEAD_ROWS, d), F32)
        out_ref[t - LEAD_ROWS:, :] = rows_of_tile(h_ref[t - LEAD_ROWS:, :], o_ref[t - LEAD_ROWS:, :])

    @pl.when(jnp.logical_not(lead))
    def _():
        out_ref[...] = rows_of_tile(h_ref[...], o_ref[...])


def _attn_out_ffn(h, o, wo, gf, wg, wu, wd):
    b, lh, d = h.shape
    rows = b * lh
    tile_f32 = pl.BlockSpec((T_SEQ, d), lambda i: (i, 0))
    out = pl.pallas_call(
        functools.partial(_attn_out_ffn_kernel, tiles_per_seq=lh // T_SEQ),
        grid=(rows // T_SEQ,),
        in_specs=[tile_f32, pl.BlockSpec((T_SEQ, d), lambda i: (i, 0)), _const_spec(wo.shape),
                  _const_spec(gf.shape), _layer_spec(*wg), _layer_spec(*wu), _layer_spec(*wd)],
        out_specs=tile_f32,
        out_shape=jax.ShapeDtypeStruct((rows, d), F32),
        compiler_params=pltpu.CompilerParams(
            dimension_semantics=("arbitrary",), vmem_limit_bytes=VMEM_LIMIT_BYTES),
        name="attn_out_ffn",
    )(h.reshape(rows, d), o.reshape(rows, d), wo, gf, wg[0], wu[0], wd[0])
    return out.reshape(b, lh, d)


def _conv_ffn_kernel(h_ref, gm_ref, win_ref, bin_ref, dww_ref, dwb_ref, lng_ref, lnb_ref,
                     wout_ref, bout_ref, gf_ref, wg_ref, wu_ref, wd_ref, o_ref, uext_ref, conv_ref):
    i = pl.program_id(1)
    t, d = h_ref.shape[1:]

    lead = max(LEAD_ROWS, CONV_ROWS)

    @pl.when(i == 0)
    def _():
        uext_ref[0:CONV_HALO, :] = jnp.zeros((CONV_HALO, d), F32)
        for rows in (lead, t):
            uext_ref[CONV_HALO + rows:CONV_HALO + rows + SUBLANES, :] = jnp.zeros((SUBLANES, d), F32)
        o_ref[0, 0:t - lead, :] = jnp.zeros((t - lead, d), F32)
        o_ref[0, t - lead:, :] = _conv_ffn_rows(
            h_ref[0, t - lead:, :], _positions(0, t)[t - lead:], gm_ref, win_ref, bin_ref,
            dww_ref, dwb_ref, lng_ref, lnb_ref, wout_ref, bout_ref, gf_ref, wg_ref, wu_ref, wd_ref,
            uext_ref, conv_ref)

    @pl.when(i > 0)
    def _():
        o_ref[0] = _conv_ffn_rows(
            h_ref[0], None, gm_ref, win_ref, bin_ref, dww_ref, dwb_ref, lng_ref, lnb_ref,
            wout_ref, bout_ref, gf_ref, wg_ref, wu_ref, wd_ref, uext_ref, conv_ref)


def _conv_ffn_rows(h, pos, gm_ref, win_ref, bin_ref, dww_ref, dwb_ref, lng_ref, lnb_ref,
                   wout_ref, bout_ref, gf_ref, wg_ref, wu_ref, wd_ref, uext_ref, conv_ref):
    rows, d = h.shape
    y = _rms(h, gm_ref[...], RMS_EPS).astype(BF16)
    ag = jnp.dot(y, win_ref[...], preferred_element_type=F32) + bin_ref[...]
    u = ag[:, :d] * jax.nn.sigmoid(ag[:, d:])
    if pos is not None:
        u = jnp.where(pos >= 0, u, 0.0)
    uext_ref[CONV_HALO:CONV_HALO + rows, :] = u

    base = CONV_HALO - (CONV_WIDTH - 1)
    span = CONV_ROWS + SUBLANES
    n_oct = (base + CONV_WIDTH - 1) // SUBLANES + 1

    def conv_body(c, carry):
        r0 = pl.multiple_of(c * CONV_ROWS, CONV_ROWS)
        for lb in range(d // LANES):
            cs = slice(lb * LANES, (lb + 1) * LANES)
            um = [uext_ref[pl.ds(r0 + SUBLANES * m, span), cs] for m in range(n_oct)]
            out = None
            for sh in range(SUBLANES):
                part = None
                for m in range(n_oct):
                    j = SUBLANES * m + sh - base
                    if 0 <= j < CONV_WIDTH:
                        term = dww_ref[j:j + 1, cs] * um[m]
                        part = term if part is None else part + term
                if sh:
                    part = pltpu.roll(part, span - sh, axis=0)
                out = part if out is None else out + part
            conv_ref[pl.ds(r0, CONV_ROWS), cs] = out[:CONV_ROWS] + dwb_ref[:, cs]
        return carry

    lax.fori_loop(0, rows // CONV_ROWS, conv_body, 0)
    uext_ref[0:CONV_HALO, :] = uext_ref[rows:rows + CONV_HALO, :]

    c = conv_ref[0:rows, :]
    mu = jnp.mean(c, axis=-1, keepdims=True)
    cc = c - mu
    var = jnp.mean(cc * cc, axis=-1, keepdims=True)
    z = (cc * lax.rsqrt(var + LN_EPS)) * lng_ref[...] + lnb_ref[...]
    z = (z * jax.nn.sigmoid(z)).astype(BF16)
    hmid = h + jnp.dot(z, wout_ref[...], preferred_element_type=F32) + bout_ref[...]
    return _ffn(hmid, gf_ref, wg_ref, wu_ref, wd_ref)


def _conv_ffn(h, gm, win, bin_, dww, dwb, lng, lnb, wout, bout, gf, wg, wu, wd):
    b, lh, d = h.shape
    nt = lh // T_SEQ
    tile = pl.BlockSpec((1, T_SEQ, d), lambda bi, i: (bi, i, 0))
    consts = [gm, win, bin_, dww, dwb, lng, lnb, wout, bout, gf]
    ffn = [wg, wu, wd]
    return pl.pallas_call(
        _conv_ffn_kernel,
        grid=(b, nt),
        in_specs=[tile] + [_const_spec(c.shape) for c in consts] + [_layer_spec(*w) for w in ffn],
        out_specs=tile,
        out_shape=jax.ShapeDtypeStruct((b, lh, d), F32),
        scratch_shapes=[pltpu.VMEM((T_SEQ + CONV_HALO + SUBLANES, d), F32), pltpu.VMEM((T_SEQ, d), F32)],
        compiler_params=pltpu.CompilerParams(
            dimension_semantics=("arbitrary", "arbitrary"), vmem_limit_bytes=VMEM_LIMIT_BYTES),
        name="conv_ffn",
    )(h, *consts, *(w[0] for w in ffn))


def kernel(x, meta_tokens, norm_mix_g, norm_ffn_g, final_norm_g, rel_bias_table, pool_w, pool_b, pool_scale, attn_w_qkv, attn_w_o, attn_lambda_q1, attn_lambda_k1, attn_lambda_q2, attn_lambda_k2, attn_subln_g, conv_w_in, conv_b_in, conv_dw_w, conv_dw_b, conv_ln_g, conv_ln_b, conv_w_out, conv_b_out, ffn_w_gate, ffn_w_up, ffn_w_down):
    b, seq, d = x.shape
    depth = norm_mix_g.shape[0]
    assert seq % T_SEQ == 0 and d % (LANES * len(POOL_WINDOWS)) == 0
    assert depth % N_MIXERS == 1, "the last layer must be a pooling layer (it applies the final norm)"

    row = lambda v: v.reshape(1, -1).astype(F32)
    h = x.astype(F32)
    bias = None
    ffn_stacks = [w.astype(BF16) for w in (ffn_w_gate, ffn_w_up, ffn_w_down)]
    for i in range(depth):
        mixer, j = i % N_MIXERS, i // N_MIXERS
        gm, gf = row(norm_mix_g[i]), row(norm_ffn_g[i])
        wg, wu, wd = ((w, i) for w in ffn_stacks)
        if mixer == 0:
            h = _pool_ffn(h, gm, pool_w[j].astype(BF16), row(pool_b[j]), row(pool_scale[j]),
                          gf, wg, wu, wd,
                          meta=meta_tokens.astype(F32) if i == 0 else None,
                          final_g=row(final_norm_g) if i == depth - 1 else None)
        elif mixer == 1:
            lam_init = 0.8 - 0.6 * math.exp(-0.3 * i)
            hd = d // DA_HEADS
            if bias is None:
                bias = _bias_tiles(rel_bias_table.astype(F32), _bias_buckets())
            qscale = (hd // 2) ** -0.5 * LOG2E
            q, k, vt = _qkv(h, gm, attn_w_qkv[j].astype(BF16), qscale)
            lamv = jnp.stack([attn_lambda_q1[j], attn_lambda_k1[j],
                              attn_lambda_q2[j], attn_lambda_k2[j]]).astype(F32)
            o = _attention(q, k, vt, bias, lamv, attn_subln_g[j].reshape(hd, 1).astype(F32), lam_init)
            h = _attn_out_ffn(h, o, attn_w_o[j].astype(BF16), gf, wg, wu, wd)
        else:
            h = _conv_ffn(h, gm, conv_w_in[j].astype(BF16), row(conv_b_in[j]),
                          conv_dw_w[j].astype(F32), row(conv_dw_b[j]), row(conv_ln_g[j]),
                          row(conv_ln_b[j]), conv_w_out[j].astype(BF16), row(conv_b_out[j]),
                          gf, wg, wu, wd)
    return h
```

```python
import functools
import math

import jax
import jax.numpy as jnp
from jax import lax
from jax.experimental import pallas as pl
from jax.experimental.pallas import tpu as pltpu

F32 = jnp.float32
BF16 = jnp.bfloat16

N_META = 16
CHUNK = 64
POOL_WINDOWS = (2, 4, 8, 16)
DA_HEADS = 8
REL_BUCKETS = 32
REL_MAX_DIST = 128
CONV_WIDTH = 31
RMS_EPS = 1e-6
LN_EPS = 1e-5
SUBLN_EPS = 1e-5
N_MIXERS = 3

LANES = 128
SUBLANES = 8
MXU_DIM = 256
VMEM_LIMIT_BYTES = 56 * 1024 * 1024

T_SEQ = 512
META_BLK = 128
ONES_ROWS = 16
COL_CHUNK = 2 * MXU_DIM
LEAD_ROWS = 32
POOL_HALO = 16
CONV_HALO = 32
CONV_ROWS = 128
NEG = -1e30
LOG2E = math.log2(math.e)
FAR_BUCKET = REL_BUCKETS // 2 - 1


def _ffn_chunks(f):
    step = 3 * MXU_DIM
    return [(s, min(s + step, f)) for s in range(0, f, step)]


def _rms(x, g, eps):
    ms = jnp.mean(x * x, axis=-1, keepdims=True)
    return (x * lax.rsqrt(ms + eps)) * g


def _ffn(hmid, g_ref, wg_ref, wu_ref, wd_ref):
    x = _rms(hmid, g_ref[...], RMS_EPS).astype(BF16)
    acc = None
    for f0, f1 in _ffn_chunks(wg_ref.shape[1]):
        gate = jnp.dot(x, wg_ref[:, f0:f1], preferred_element_type=F32)
        up = jnp.dot(x, wu_ref[:, f0:f1], preferred_element_type=F32)
        act = ((gate * jax.nn.sigmoid(gate)) * up).astype(BF16)
        part = jnp.dot(act, wd_ref[f0:f1, :], preferred_element_type=F32)
        acc = part if acc is None else acc + part
    return hmid + acc


def _positions(tile_idx, rows):
    r = lax.broadcasted_iota(jnp.int32, (rows, 1), 0)
    return tile_idx * rows + r - (T_SEQ - N_META)


def _pool_ffn_kernel(h_ref, gm_ref, pw_ref, pb_ref, ps_ref, gf_ref, wg_ref, wu_ref, wd_ref, *rest,
                     first, final):
    rest = list(rest)
    meta_ref = rest.pop(0) if first else None
    fg_ref = rest.pop(0) if final else None
    o_ref, halo_ref = rest
    i = pl.program_id(1)
    t, d = h_ref.shape[1:]

    def rows_of_tile(h, pos, halo):
        rows = h.shape[0]
        y = _rms(h, gm_ref[...], RMS_EPS)
        if pos is not None:
            y = jnp.where(pos >= 0, y, 0.0)
        ext = jnp.concatenate([halo, y], axis=0)
        halo_ref[...] = y[rows - POOL_HALO:, :]
        group = d // len(POOL_WINDOWS)
        mixed = []
        for gi, win in enumerate(POOL_WINDOWS):
            c0 = gi * group
            s = ext[:, c0:c0 + group]
            shift = 1
            while shift < win:
                s = s + pltpu.roll(s, shift, axis=0)
                shift *= 2
            if pos is None:
                mean = s[POOL_HALO:, :] * (1.0 / win)
            else:
                mean = s[POOL_HALO:, :] / jnp.clip(pos + 1, 1, win).astype(F32)
            pooled = mean - y[:, c0:c0 + group]
            mixed.append(jnp.dot(pooled.astype(BF16), pw_ref[gi], preferred_element_type=F32))
        mixed = jnp.concatenate(mixed, axis=1)
        hmid = h + (mixed + pb_ref[...]) * ps_ref[...]
        out = _ffn(hmid, gf_ref, wg_ref, wu_ref, wd_ref)
        if final:
            out = _rms(out, fg_ref[...], RMS_EPS)
        return out

    @pl.when(i == 0)
    def _():
        if first:
            h = jnp.concatenate([jnp.zeros((LEAD_ROWS - N_META, d), F32), meta_ref[...]], axis=0)
        else:
            h = h_ref[0, t - LEAD_ROWS:, :]
        pos = _positions(0, t)[t - LEAD_ROWS:]
        o_ref[0, 0:t - LEAD_ROWS, :] = jnp.zeros((t - LEAD_ROWS, d), F32)
        o_ref[0, t - LEAD_ROWS:, :] = rows_of_tile(h, pos, jnp.zeros((POOL_HALO, d), F32))

    @pl.when(i > 0)
    def _():
        o_ref[0] = rows_of_tile(h_ref[0], None, halo_ref[...])


def _const_spec(shape):
    nd = len(shape)
    return pl.BlockSpec(shape, lambda *_: (0,) * nd, pipeline_mode=pl.Buffered(1))


def _layer_spec(stack, layer):
    nd = stack.ndim - 1
    return pl.BlockSpec((None,) + stack.shape[1:], lambda *_: (layer,) + (0,) * nd,
                        pipeline_mode=pl.Buffered(1))


def _pool_ffn(h, gm, pw, pb, ps, gf, wg, wu, wd, meta=None, final_g=None):
    first, final = meta is not None, final_g is not None
    b, _, d = h.shape
    lh = h.shape[1] + T_SEQ if first else h.shape[1]
    nt = lh // T_SEQ
    tile = pl.BlockSpec((1, T_SEQ, d), lambda bi, i: (bi, i, 0))
    frames = pl.BlockSpec((1, T_SEQ, d), lambda bi, i: (bi, jnp.maximum(i - 1, 0), 0))
    in_specs = [frames if first else tile, _const_spec(gm.shape), _const_spec(pw.shape),
                _const_spec(pb.shape), _const_spec(ps.shape), _const_spec(gf.shape),
                _layer_spec(*wg), _layer_spec(*wu), _layer_spec(*wd)]
    args = [h, gm, pw, pb, ps, gf, wg[0], wu[0], wd[0]]
    if first:
        in_specs.append(_const_spec(meta.shape))
        args.append(meta)
    if final:
        in_specs.append(_const_spec(final_g.shape))
        args.append(final_g)
        out_spec = frames
        out_shape = jax.ShapeDtypeStruct((b, lh - T_SEQ, d), F32)
    else:
        out_spec = tile
        out_shape = jax.ShapeDtypeStruct((b, lh, d), F32)
    return pl.pallas_call(
        functools.partial(_pool_ffn_kernel, first=first, final=final),
        grid=(b, nt),
        in_specs=in_specs,
        out_specs=out_spec,
        out_shape=out_shape,
        scratch_shapes=[pltpu.VMEM((POOL_HALO, d), F32)],
        compiler_params=pltpu.CompilerParams(
            dimension_semantics=("arbitrary", "arbitrary"), vmem_limit_bytes=VMEM_LIMIT_BYTES),
        name="pool_ffn_final" if final else "pool_ffn",
    )(*args)


def _qkv_kernel(h_ref, g_ref, w_ref, q_ref, k_ref, vt_ref, *, qscale):
    h = h_ref[0]
    d = h.shape[1]
    y = _rms(h, g_ref[...], RMS_EPS).astype(BF16)
    q = jnp.dot(y, w_ref[:, 0:d], preferred_element_type=F32) * qscale
    k = jnp.dot(y, w_ref[:, d:2 * d], preferred_element_type=F32)
    v = jnp.dot(y, w_ref[:, 2 * d:3 * d], preferred_element_type=F32)
    q_ref[0] = q.astype(BF16)
    k_ref[0] = k.astype(BF16)
    vt = v.T.astype(BF16)
    heads, hv, rows = vt_ref.shape[2:]
    hd = d // heads
    for hh in range(heads):
        vt_ref[0, 0, hh, 0:hd, :] = vt[hh * hd:(hh + 1) * hd, :]
        vt_ref[0, 0, hh, hd:hv, :] = jnp.ones((hv - hd, rows), BF16)


def _qkv(h, g, w, qscale):
    b, lh, d = h.shape
    nt = lh // T_SEQ
    hv = d // DA_HEADS + ONES_ROWS
    tile = pl.BlockSpec((1, T_SEQ, d), lambda bi, i: (bi, i, 0))
    return pl.pallas_call(
        functools.partial(_qkv_kernel, qscale=qscale),
        grid=(b, nt),
        in_specs=[tile, _const_spec(g.shape), _const_spec(w.shape)],
        out_specs=[tile, tile,
                   pl.BlockSpec((1, 1, DA_HEADS, hv, T_SEQ), lambda bi, i: (bi, i, 0, 0, 0))],
        out_shape=[jax.ShapeDtypeStruct((b, lh, d), BF16), jax.ShapeDtypeStruct((b, lh, d), BF16),
                   jax.ShapeDtypeStruct((b, nt, DA_HEADS, hv, T_SEQ), BF16)],
        compiler_params=pltpu.CompilerParams(
            dimension_semantics=("arbitrary", "arbitrary"), vmem_limit_bytes=VMEM_LIMIT_BYTES),
        name="qkv",
    )(h, g, w)


def _t5_bucket(rel):
    half = REL_BUCKETS // 2
    max_exact = half // 2
    offset = jnp.where(rel > 0, half, 0)
    n = jnp.abs(rel)
    log_ratio = jnp.log(jnp.maximum(n, 1).astype(F32) / max_exact) / math.log(REL_MAX_DIST / max_exact)
    large = jnp.minimum(max_exact + (log_ratio * (half - max_exact)).astype(jnp.int32), half - 1)
    return offset + jnp.where(n < max_exact, n, large)


def _bias_buckets():
    t = T_SEQ
    kl = lax.broadcasted_iota(jnp.int32, (t, t), 0)
    ql = lax.broadcasted_iota(jnp.int32, (t, t), 1)
    prev = _t5_bucket(kl - t - ql)[t - META_BLK:]
    diag = jnp.where(kl // CHUNK <= ql // CHUNK, _t5_bucket(kl - ql), -1)
    kr = lax.broadcasted_iota(jnp.int32, (META_BLK, t), 0)
    qc = lax.broadcasted_iota(jnp.int32, (META_BLK, t), 1)
    mk = kr - (META_BLK - N_META)
    mq = qc - (t - N_META)
    lead = jnp.where(mq >= 0, _t5_bucket(mk - mq), FAR_BUCKET)
    first = _t5_bucket(mk - N_META - qc)
    later = jnp.full((META_BLK, t), FAR_BUCKET, jnp.int32)
    meta = [jnp.where(mk >= 0, m, -1) for m in (lead, first, later)]
    nb = t // META_BLK
    return jnp.concatenate([prev[None], diag.reshape(nb, META_BLK, t), jnp.stack(meta)], axis=0)


def _bias_kernel(tab_ref, bk_ref, o_ref):
    hh = pl.program_id(0)
    far = tab_ref[FAR_BUCKET, hh]
    for blk in range(bk_ref.shape[0]):
        b = bk_ref[blk]
        acc = jnp.zeros(b.shape, F32)
        for t in range(REL_BUCKETS):
            acc = jnp.where(b == t, tab_ref[t, hh] - far, acc)
        o_ref[0, blk] = jnp.where(b < 0, NEG, acc * LOG2E)


def _bias_tiles(table, buckets):
    nb, r, c = buckets.shape
    heads = table.shape[1]
    return pl.pallas_call(
        _bias_kernel,
        grid=(heads,),
        in_specs=[pl.BlockSpec(memory_space=pltpu.SMEM),
                  pl.BlockSpec((nb, r, c), lambda hh: (0, 0, 0))],
        out_specs=pl.BlockSpec((1, nb, r, c), lambda hh: (hh, 0, 0, 0)),
        out_shape=jax.ShapeDtypeStruct((heads, nb, r, c), F32),
        compiler_params=pltpu.CompilerParams(dimension_semantics=("arbitrary",)),
        name="rel_bias",
    )(table, buckets)


def _attn_kernel(q_ref, k_ref, vt_ref, bias_ref, lam_ref, sg_ref, o_ref,
                 qzz_ref, m_ref, mt_ref, acc_ref, sa_ref, sb_ref, *, lam_init):
    lv = lam_ref[...]
    lam = (jnp.exp(jnp.sum(lv[0:1] * lv[1:2], axis=1, keepdims=True))
           - jnp.exp(jnp.sum(lv[2:3] * lv[3:4], axis=1, keepdims=True)) + lam_init)

    def query_tile(a, carry):
        _attn_query_tile(a, lam, q_ref, k_ref, vt_ref, bias_ref, sg_ref, o_ref,
                         qzz_ref, m_ref, mt_ref, acc_ref, sa_ref, sb_ref, lam_init=lam_init)
        return carry

    n_tiles = q_ref.shape[1] // T_SEQ
    lax.fori_loop(0, n_tiles, query_tile, 0)
    _attn_finalize(n_tiles - 1, lam, acc_ref, sg_ref, o_ref, lam_init)


def _attn_query_tile(a, lam, q_ref, k_ref, vt_ref, bias_ref, sg_ref, o_ref,
                     qzz_ref, m_ref, mt_ref, acc_ref, sa_ref, sb_ref, *, lam_init):
    t = T_SEQ
    hd = q_ref.shape[2]
    nb = t // META_BLK
    rows = pl.ds(pl.multiple_of(a * t, t), t)
    bnear_ref = bias_ref

    qt = q_ref[0, rows, :].astype(F32).T
    row = lax.broadcasted_iota(jnp.int32, (hd, t), 0)
    q1 = jnp.where(row < hd // 2, qt, 0.0)
    q2 = jnp.where(row >= hd // 2, qt, 0.0)
    qzz_ref[...] = jnp.concatenate([q1, q2], axis=1).astype(BF16)

    def produce(j, buf):
        kt = k_ref[0, pl.ds(pl.multiple_of(j * t, t), t), :]
        mts = []
        for cs in col_chunks:
            s = jnp.dot(kt, qzz_ref[:, cs], preferred_element_type=F32)
            buf[:, cs] = s
            mts.append(jnp.max(s, axis=0, keepdims=True))
        return jnp.concatenate(mts, axis=1)

    def probs(s, m):
        return jnp.exp2(s - m).astype(BF16)

    def consume(buf, mt, vt):
        for cs in col_chunks:
            m_old = m_ref[:, cs]
            m_new = jnp.maximum(m_old, mt[:, cs])
            alpha = jnp.exp2(m_old - m_new)
            acc_ref[:, cs] = alpha * acc_ref[:, cs] + jnp.dot(
                vt, probs(buf[:, cs], m_new), preferred_element_type=F32)
            m_ref[:, cs] = m_new

    half = t // 2
    maps = (0, t)

    def produce_diag(j, buf):
        k0 = pl.multiple_of(j * t, t)
        k_top = k_ref[0, pl.ds(k0, half), :]
        k_bot = k_ref[0, pl.ds(k0 + half, half), :]
        for base in maps:
            buf[0:half, base:base + t] = jnp.dot(
                k_top, qzz_ref[:, base:base + t], preferred_element_type=F32)
            buf[half:t, base + half:base + t] = jnp.dot(
                k_bot, qzz_ref[:, base + half:base + t], preferred_element_type=F32)

    def consume_diag(buf, j):
        vt = vt_ref[0, j, 0]
        for base in maps:
            cs, hi = slice(base, base + t), slice(base + half, base + t)
            s_top = buf[0:half, cs] + bnear_ref[0, 1:1 + nb // 2].reshape(half, t)
            s_bot = buf[half:t, hi] + bnear_ref[0, 1 + nb // 2:1 + nb, :, half:t].reshape(half, half)
            mt_top = jnp.max(s_top, axis=0, keepdims=True)
            mt = jnp.concatenate(
                [mt_top[:, 0:half], jnp.maximum(mt_top[:, half:t], jnp.max(s_bot, axis=0, keepdims=True))],
                axis=1)
            m_old = m_ref[:, cs]
            m_new = jnp.maximum(m_old, mt)
            alpha = jnp.exp2(m_old - m_new)
            p_top = probs(s_top, m_new)
            p_bot = probs(s_bot, m_new[:, half:t])
            pv = jnp.dot(vt[:, 0:half], p_top, preferred_element_type=F32)
            pv_hi = pv[:, half:t] + jnp.dot(vt[:, half:t], p_bot, preferred_element_type=F32)
            acc_ref[:, cs] = alpha * acc_ref[:, cs] + jnp.concatenate([pv[:, 0:half], pv_hi], axis=1)
            m_ref[:, cs] = m_new

    def fix_prev(buf, mt):
        corner = bnear_ref[0, 0, :, 0:LANES]
        pieces, start = [], 0
        for base in (0, t):
            cs = slice(base, base + LANES)
            buf[t - META_BLK:, cs] = buf[t - META_BLK:, cs] + corner
            if base > start:
                pieces.append(mt[:, start:base])
            pieces.append(jnp.max(buf[:, cs], axis=0, keepdims=True))
            start = base + LANES
        return jnp.concatenate(pieces + [mt[:, start:]], axis=1)

    def init():
        s0 = jnp.dot(k_ref[0, t - N_META:t, :], qzz_ref[...], preferred_element_type=F32)
        bmeta = bias_ref[0, nb + 1 + jnp.minimum(a, 2), META_BLK - N_META:, :]
        s0 = s0 + jnp.concatenate([bmeta, bmeta], axis=1)
        m0 = jnp.max(s0, axis=0, keepdims=True)
        m_ref[...] = m0
        vt_meta = vt_ref[0, 0, 0, :, t - META_BLK:t][:, META_BLK - N_META:]
        acc_ref[...] = jnp.dot(vt_meta, probs(s0, m0), preferred_element_type=F32)

    col_chunks = [slice(c, c + COL_CHUNK) for c in range(0, 2 * t, COL_CHUNK)]
    odd = lax.rem(a, 2)

    def finish_previous():
        _attn_finalize(a - 1, lam, acc_ref, sg_ref, o_ref, lam_init)

    @pl.when(a == 0)
    def _():
        init()

    @pl.when(a == 1)
    def _():
        finish_previous()
        init()
        produce_diag(1, sb_ref)
        consume_diag(sb_ref, 1)

    @pl.when((a >= 2) & (odd == 1))
    def _():
        finish_previous()
        init()
        mtb = produce(1, sb_ref)
        mt_ref[...] = produce(2, sa_ref)
        consume(sb_ref, mtb, vt_ref[0, 1, 0])

    @pl.when((a >= 2) & (odd == 0))
    def _():
        finish_previous()
        init()
        mt_ref[...] = produce(1, sa_ref)

    @pl.when(a >= 2)
    def _():
        def pair(j, mta):
            mtb = produce(j + 1, sb_ref)
            consume(sa_ref, mta, vt_ref[0, j, 0])
            mta_next = produce(j + 2, sa_ref)
            consume(sb_ref, mtb, vt_ref[0, j + 1, 0])
            return mta_next

        n_pairs = (a - 2) // 2
        odd_pairs = lax.rem(n_pairs, 2)

        @pl.when(odd_pairs == 1)
        def _():
            mt_ref[...] = pair(1 + odd, mt_ref[...])

        n_quads = n_pairs // 2
        odd_quads = lax.rem(n_quads, 2)

        @pl.when(odd_quads == 1)
        def _():
            j = 1 + odd + 2 * odd_pairs
            mt_ref[...] = pair(j + 2, pair(j, mt_ref[...]))

        j0 = 1 + odd + 2 * odd_pairs + 4 * odd_quads

        def oct_body(i, mta):
            j = j0 + 8 * i
            for step in range(0, 8, 2):
                mta = pair(j + step, mta)
            return mta

        mta = lax.fori_loop(0, n_quads // 2, oct_body, mt_ref[...])
        produce_diag(a, sb_ref)
        consume(sa_ref, fix_prev(sa_ref, mta), vt_ref[0, a - 1, 0])
        consume_diag(sb_ref, a)


def _attn_finalize(a, lam, acc_ref, sg_ref, o_ref, lam_init):
    t = T_SEQ
    hd = o_ref.shape[2]
    on = acc_ref[0:hd, :] * (1.0 / acc_ref[hd:hd + 1, :])
    o = on[:, :t] - lam * on[:, t:]
    o = (o * lax.rsqrt(jnp.mean(o * o, axis=0, keepdims=True) + SUBLN_EPS)) * sg_ref[...]
    o = o * (1.0 - lam_init)
    o_ref[0, pl.ds(pl.multiple_of(a * t, t), t), :] = o.T.astype(BF16)


def _attention(q, k, vt, bias, lamv, sg, lam_init):
    b, lh, d = q.shape
    nt = lh // T_SEQ
    hd = d // DA_HEADS
    hv = vt.shape[3]
    seq = pl.BlockSpec((1, lh, hd), lambda bi, hh: (bi, 0, hh))
    return pl.pallas_call(
        functools.partial(_attn_kernel, lam_init=lam_init),
        grid=(b, DA_HEADS),
        in_specs=[
            seq,
            seq,
            pl.BlockSpec((1, nt, 1, hv, T_SEQ), lambda bi, hh: (bi, 0, hh, 0, 0)),
            pl.BlockSpec((1,) + bias.shape[1:], lambda bi, hh: (hh, 0, 0, 0)),
            pl.BlockSpec(lamv.shape, lambda bi, hh: (0, 0)),
            pl.BlockSpec(sg.shape, lambda bi, hh: (0, 0)),
        ],
        out_specs=seq,
        out_shape=jax.ShapeDtypeStruct((b, lh, d), BF16),
        scratch_shapes=[pltpu.VMEM((hd, 2 * T_SEQ), BF16), pltpu.VMEM((1, 2 * T_SEQ), F32),
                        pltpu.VMEM((1, 2 * T_SEQ), F32), pltpu.VMEM((hv, 2 * T_SEQ), F32),
                        pltpu.VMEM((T_SEQ, 2 * T_SEQ), F32), pltpu.VMEM((T_SEQ, 2 * T_SEQ), F32)],
        compiler_params=pltpu.CompilerParams(
            dimension_semantics=("arbitrary", "arbitrary"), vmem_limit_bytes=VMEM_LIMIT_BYTES),
        name="diff_attn",
    )(q, k, vt, bias, lamv, sg)


def _attn_out_ffn_kernel(h_ref, o_ref, wo_ref, gf_ref, wg_ref, wu_ref, wd_ref, out_ref, *,
                         tiles_per_seq):
    t, d = h_ref.shape

    def rows_of_tile(h, o):
        hmid = h + jnp.dot(o, wo_ref[...], preferred_element_type=F32)
        return _ffn(hmid, gf_ref, wg_ref, wu_ref, wd_ref)

    lead = lax.rem(pl.program_id(0), tiles_per_seq) == 0

    @pl.when(lead)
    def _():
        out_ref[0:t - LEAD_ROWS, :] = jnp.zeros((t - LEAD_ROWS, d), F32)
        out_ref[t - LEAD_ROWS:, :] = rows_of_tile(h_ref[t - LEAD_ROWS:, :], o_ref[t - LEAD_ROWS:, :])

    @pl.when(jnp.logical_not(lead))
    def _():
        out_ref[...] = rows_of_tile(h_ref[...], o_ref[...])


def _attn_out_ffn(h, o, wo, gf, wg, wu, wd):
    b, lh, d = h.shape
    rows = b * lh
    tile_f32 = pl.BlockSpec((T_SEQ, d), lambda i: (i, 0))
    out = pl.pallas_call(
        functools.partial(_attn_out_ffn_kernel, tiles_per_seq=lh // T_SEQ),
        grid=(rows // T_SEQ,),
        in_specs=[tile_f32, pl.BlockSpec((T_SEQ, d), lambda i: (i, 0)), _const_spec(wo.shape),
                  _const_spec(gf.shape), _layer_spec(*wg), _layer_spec(*wu), _layer_spec(*wd)],
        out_specs=tile_f32,
        out_shape=jax.ShapeDtypeStruct((rows, d), F32),
        compiler_params=pltpu.CompilerParams(
            dimension_semantics=("arbitrary",), vmem_limit_bytes=VMEM_LIMIT_BYTES),
        name="attn_out_ffn",
    )(h.reshape(rows, d), o.reshape(rows, d), wo, gf, wg[0], wu[0], wd[0])
    return out.reshape(b, lh, d)


def _conv_ffn_kernel(h_ref, gm_ref, win_ref, bin_ref, dww_ref, dwb_ref, lng_ref, lnb_ref,
                     wout_ref, bout_ref, gf_ref, wg_ref, wu_ref, wd_ref, o_ref, uext_ref, conv_ref):
    i = pl.program_id(1)
    t, d = h_ref.shape[1:]

    lead = max(LEAD_ROWS, CONV_ROWS)

    @pl.when(i == 0)
    def _():
        uext_ref[0:CONV_HALO, :] = jnp.zeros((CONV_HALO, d), F32)
        for rows in (lead, t):
            uext_ref[CONV_HALO + rows:CONV_HALO + rows + SUBLANES, :] = jnp.zeros((SUBLANES, d), F32)
        o_ref[0, 0:t - lead, :] = jnp.zeros((t - lead, d), F32)
        o_ref[0, t - lead:, :] = _conv_ffn_rows(
            h_ref[0, t - lead:, :], _positions(0, t)[t - lead:], gm_ref, win_ref, bin_ref,
            dww_ref, dwb_ref, lng_ref, lnb_ref, wout_ref, bout_ref, gf_ref, wg_ref, wu_ref, wd_ref,
            uext_ref, conv_ref)

    @pl.when(i > 0)
    def _():
        o_ref[0] = _conv_ffn_rows(
            h_ref[0], None, gm_ref, win_ref, bin_ref, dww_ref, dwb_ref, lng_ref, lnb_ref,
            wout_ref, bout_ref, gf_ref, wg_ref, wu_ref, wd_ref, uext_ref, conv_ref)


def _conv_ffn_rows(h, pos, gm_ref, win_ref, bin_ref, dww_ref, dwb_ref, lng_ref, lnb_ref,
                   wout_ref, bout_ref, gf_ref, wg_ref, wu_ref, wd_ref, uext_ref, conv_ref):
    rows, d = h.shape
    y = _rms(h, gm_ref[...], RMS_EPS).astype(BF16)
    ag = jnp.dot(y, win_ref[...], preferred_element_type=F32) + bin_ref[...]
    u = ag[:, :d] * jax.nn.sigmoid(ag[:, d:])
    if pos is not None:
        u = jnp.where(pos >= 0, u, 0.0)
    uext_ref[CONV_HALO:CONV_HALO + rows, :] = u

    base = CONV_HALO - (CONV_WIDTH - 1)
    span = CONV_ROWS + SUBLANES
    n_oct = (base + CONV_WIDTH - 1) // SUBLANES + 1

    def conv_body(c, carry):
        r0 = pl.multiple_of(c * CONV_ROWS, CONV_ROWS)
        for lb in range(d // LANES):
            cs = slice(lb * LANES, (lb + 1) * LANES)
            um = [uext_ref[pl.ds(r0 + SUBLANES * m, span), cs] for m in range(n_oct)]
            out = None
            for sh in range(SUBLANES):
                part = None
                for m in range(n_oct):
                    j = SUBLANES * m + sh - base
                    if 0 <= j < CONV_WIDTH:
                        term = dww_ref[j:j + 1, cs] * um[m]
                        part = term if part is None else part + term
                if sh:
                    part = pltpu.roll(part, span - sh, axis=0)
                out = part if out is None else out + part
            conv_ref[pl.ds(r0, CONV_ROWS), cs] = out[:CONV_ROWS] + dwb_ref[:, cs]
        return carry

    lax.fori_loop(0, rows // CONV_ROWS, conv_body, 0)
    uext_ref[0:CONV_HALO, :] = uext_ref[rows:rows + CONV_HALO, :]

    c = conv_ref[0:rows, :]
    mu = jnp.mean(c, axis=-1, keepdims=True)
    cc = c - mu
    var = jnp.mean(cc * cc, axis=-1, keepdims=True)
    z = (cc * lax.rsqrt(var + LN_EPS)) * lng_ref[...] + lnb_ref[...]
    z = (z * jax.nn.sigmoid(z)).astype(BF16)
    hmid = h + jnp.dot(z, wout_ref[...], preferred_element_type=F32) + bout_ref[...]
    return _ffn(hmid, gf_ref, wg_ref, wu_ref, wd_ref)


def _conv_ffn(h, gm, win, bin_, dww, dwb, lng, lnb, wout, bout, gf, wg, wu, wd):
    b, lh, d = h.shape
    nt = lh // T_SEQ
    tile = pl.BlockSpec((1, T_SEQ, d), lambda bi, i: (bi, i, 0))
    consts = [gm, win, bin_, dww, dwb, lng, lnb, wout, bout, gf]
    ffn = [wg, wu, wd]
    return pl.pallas_call(
        _conv_ffn_kernel,
        grid=(b, nt),
        in_specs=[tile] + [_const_spec(c.shape) for c in consts] + [_layer_spec(*w) for w in ffn],
        out_specs=tile,
        out_shape=jax.ShapeDtypeStruct((b, lh, d), F32),
        scratch_shapes=[pltpu.VMEM((T_SEQ + CONV_HALO + SUBLANES, d), F32), pltpu.VMEM((T_SEQ, d), F32)],
        compiler_params=pltpu.CompilerParams(
            dimension_semantics=("arbitrary", "arbitrary"), vmem_limit_bytes=VMEM_LIMIT_BYTES),
        name="conv_ffn",
    )(h, *consts, *(w[0] for w in ffn))


def kernel(x, meta_tokens, norm_mix_g, norm_ffn_g, final_norm_g, rel_bias_table, pool_w, pool_b, pool_scale, attn_w_qkv, attn_w_o, attn_lambda_q1, attn_lambda_k1, attn_lambda_q2, attn_lambda_k2, attn_subln_g, conv_w_in, conv_b_in, conv_dw_w, conv_dw_b, conv_ln_g, conv_ln_b, conv_w_out, conv_b_out, ffn_w_gate, ffn_w_up, ffn_w_down):
    b, seq, d = x.shape
    depth = norm_mix_g.shape[0]
    assert seq % T_SEQ == 0 and d % (LANES * len(POOL_WINDOWS)) == 0
    assert depth % N_MIXERS == 1, "the last layer must be a pooling layer (it applies the final norm)"

    row = lambda v: v.reshape(1, -1).astype(F32)
    h = x.astype(F32)
    bias = None
    ffn_stacks = [w.astype(BF16) for w in (ffn_w_gate, ffn_w_up, ffn_w_down)]
    for i in range(depth):
        mixer, j = i % N_MIXERS, i // N_MIXERS
        gm, gf = row(norm_mix_g[i]), row(norm_ffn_g[i])
        wg, wu, wd = ((w, i) for w in ffn_stacks)
        if mixer == 0:
            h = _pool_ffn(h, gm, pool_w[j].astype(BF16), row(pool_b[j]), row(pool_scale[j]),
                          gf, wg, wu, wd,
                          meta=meta_tokens.astype(F32) if i == 0 else None,
                          final_g=row(final_norm_g) if i == depth - 1 else None)
        elif mixer == 1:
            lam_init = 0.8 - 0.6 * math.exp(-0.3 * i)
            hd = d // DA_HEADS
            if bias is None:
                bias = _bias_tiles(rel_bias_table.astype(F32), _bias_buckets())
            qscale = (hd // 2) ** -0.5 * LOG2E
            q, k, vt = _qkv(h, gm, attn_w_qkv[j].astype(BF16), qscale)
            lamv = jnp.stack([attn_lambda_q1[j], attn_lambda_k1[j],
                              attn_lambda_q2[j], attn_lambda_k2[j]]).astype(F32)
            o = _attention(q, k, vt, bias, lamv, attn_subln_g[j].reshape(hd, 1).astype(F32), lam_init)
            h = _attn_out_ffn(h, o, attn_w_o[j].astype(BF16), gf, wg, wu, wd)
        else:
            h = _conv_ffn(h, gm, conv_w_in[j].astype(BF16), row(conv_b_in[j]),
                          conv_dw_w[j].astype(F32), row(conv_dw_b[j]), row(conv_ln_g[j]),
                          row(conv_ln_b[j]), conv_w_out[j].astype(BF16), row(conv_b_out[j]),
                          gf, wg, wu, wd)
    return h
```
